```python
import math
import jax, jax.numpy as jnp
from jax import lax
import numpy as np

D_MODEL = 4096
BATCH = 4
SEQ = 2048
DEPTH = 2

GRID_W = 64
CTX_LEN = 256
HEAD_DIM = 128
N_BRANCH = 4
BRANCH_W = D_MODEL // N_BRANCH
A_HEADS = BRANCH_W // HEAD_DIM
A_KV_HEADS = A_HEADS // 4
A_GROUP = A_HEADS // A_KV_HEADS
S5_P = 16
S5_GROUPS = BRANCH_W // S5_P
S5_N = 64
C_HEADS = BRANCH_W // (2 * HEAD_DIM)
R_HEADS = BRANCH_W // (2 * HEAD_DIM)
R_DV = 2 * HEAD_DIM
RET_CHUNK = 128
Q_BLOCK = 128
ROPE_BASE = 10000.0
ROPE_FREQS = HEAD_DIM // 4
D_FF = 256 * ((8 * D_MODEL // 3 + 255) // 256)
N_EXPERTS = 8
TOP_K = 2
MOE_FF = D_MODEL // 2
N_DENSE = (DEPTH + 1) // 2
N_MOE = DEPTH // 2
EPS = 1e-6
IN_SPLITS = (A_HEADS * HEAD_DIM, A_KV_HEADS * HEAD_DIM, A_KV_HEADS * HEAD_DIM,
             BRANCH_W,
             C_HEADS * 2 * HEAD_DIM, C_HEADS * 2 * HEAD_DIM, C_HEADS * 2 * HEAD_DIM,
             R_HEADS * HEAD_DIM, R_HEADS * HEAD_DIM, R_HEADS * R_DV, BRANCH_W)
IN_COLS = sum(IN_SPLITS)

kernel_name = "hybrid_prefix_flow_trunk"


def rms_norm(x, g):
    xf = x.astype(jnp.float32)
    y = xf * lax.rsqrt(jnp.mean(xf * xf, axis=-1, keepdims=True) + EPS)
    return (y * g.astype(jnp.float32)).astype(x.dtype)


def modulate(x, g, shift, scale):
    return rms_norm(x, g) * (1.0 + scale) + shift


def grid_rope_tables(n_tokens):
    rows = n_tokens // GRID_W
    row = jnp.broadcast_to(jnp.arange(rows)[:, None], (rows, GRID_W)).reshape(-1)
    col = jnp.broadcast_to(jnp.arange(GRID_W)[None, :], (rows, GRID_W)).reshape(-1)
    inv = ROPE_BASE ** (-jnp.arange(ROPE_FREQS, dtype=jnp.float32) / ROPE_FREQS)
    ang = jnp.stack([row, col], axis=-1).astype(jnp.float32)[:, :, None] * inv
    return jnp.cos(ang), jnp.sin(ang)


def apply_rope(x, cos, sin):
    shp = x.shape
    xr = x.astype(jnp.float32).reshape(shp[:-1] + (2, 2, ROPE_FREQS))
    bshape = (shp[1],) + (1,) * (x.ndim - 3) + (2, ROPE_FREQS)
    cs, sn = cos.reshape(bshape), sin.reshape(bshape)
    x1, x2 = xr[..., 0, :], xr[..., 1, :]
    out = jnp.stack([x1 * cs - x2 * sn, x2 * cs + x1 * sn], axis=-2)
    return out.reshape(shp).astype(x.dtype)


def over_query_blocks(f, q):
    n = q.shape[-2]
    nb = n // Q_BLOCK
    qb = jnp.moveaxis(q.reshape(q.shape[:-2] + (nb, Q_BLOCK, q.shape[-1])), -3, 0)
    ob = jnp.moveaxis(lax.map(f, qb), 0, -3)
    return ob.reshape(ob.shape[:-3] + (n, ob.shape[-1]))


def gqa_branch(q_c, k_c, v_c, q_l, k_l, v_l, q_norm, k_norm, cos, sin, need_ctx):
    def heads(t, n):
        return t.reshape(t.shape[:2] + (n, HEAD_DIM))

    def groups(q):
        b, t = q.shape[:2]
        return q.reshape(b, t, A_KV_HEADS, A_GROUP, HEAD_DIM).transpose(0, 2, 3, 1, 4)

    def ungroup(o):
        return o.transpose(0, 3, 1, 2, 4).reshape(o.shape[0], o.shape[3], A_HEADS * HEAD_DIM)

    scale = HEAD_DIM ** -0.5

    def attend(qb, k, v):
        s = jnp.einsum('bkgqd,bskd->bkgqs', qb, k, preferred_element_type=jnp.float32) * scale
        p = jax.nn.softmax(s, axis=-1).astype(v.dtype)
        return jnp.einsum('bkgqs,bskd->bkgqd', p, v)

    qc = rms_norm(heads(q_c, A_HEADS), q_norm)
    kc = rms_norm(heads(k_c, A_KV_HEADS), k_norm)
    vc = heads(v_c, A_KV_HEADS)
    ql = apply_rope(rms_norm(heads(q_l, A_HEADS), q_norm), cos, sin)
    kl = apply_rope(rms_norm(heads(k_l, A_KV_HEADS), k_norm), cos, sin)
    k_all = jnp.concatenate([kc, kl], axis=1)
    v_all = jnp.concatenate([vc, heads(v_l, A_KV_HEADS)], axis=1)
    y_l = ungroup(over_query_blocks(lambda qb: attend(qb, k_all, v_all), groups(ql)))
    y_c = ungroup(attend(groups(qc), kc, vc)) if need_ctx else None
    return y_c, y_l


def _linear_recurrence_combine(e1, e2):
    a1, b1 = e1
    a2, b2 = e2
    return a1 * a2, a2 * b1 + b2


def s5_states(u, a_bar, b_bar, s0, reverse):
    t = u.shape[1]
    bu = jnp.einsum('btgp,gnp->btgn', u.astype(jnp.complex64), b_bar)
    edge = t - 1 if reverse else 0
    bu = bu.at[:, edge].add(a_bar * s0)
    a = jnp.broadcast_to(a_bar, (1, t) + a_bar.shape)
    _, states = lax.associative_scan(_linear_recurrence_combine, (a, bu), reverse=reverse, axis=1)
    return states


def s5_branch(u_c, u_l, a_re, a_im, log_step, b_re, b_im, c_re, c_im, d_skip, w_glu, need_ctx):
    f32 = jnp.float32
    lam = lax.complex(a_re.astype(f32), a_im.astype(f32))
    step = jnp.exp(log_step.astype(f32))[..., None]
    a_bar = jnp.exp(lam * step)
    b_bar = ((a_bar - 1.0) / lam)[..., None] * lax.complex(b_re.astype(f32), b_im.astype(f32))
    c_mat = lax.complex(c_re.astype(f32), c_im.astype(f32))

    def grouped(u):
        return u.astype(f32).reshape(u.shape[:2] + (S5_GROUPS, S5_P))

    uc, ul = grouped(u_c), grouped(u_l)
    zero = jnp.zeros((u_l.shape[0], S5_GROUPS, S5_N), jnp.complex64)
    xc_f = s5_states(uc, a_bar[0], b_bar[0], zero, False)
    xc_b = s5_states(uc, a_bar[1], b_bar[1], zero, True)
    xl_f = s5_states(ul, a_bar[0], b_bar[0], xc_f[:, -1], False)
    xl_b = s5_states(ul, a_bar[1], b_bar[1], xc_b[:, 0], True)

    def finish(x_f, x_b, u):
        y = (jnp.einsum('btgn,gpn->btgp', x_f, c_mat[0]).real
             + jnp.einsum('btgn,gpn->btgp', x_b, c_mat[1]).real)
        y = y.reshape(u.shape) + d_skip.astype(f32) * u.astype(f32)
        z = jax.nn.gelu(y).astype(u.dtype)
        return z * jax.nn.sigmoid(z @ w_glu)

    y_l = finish(xl_f, xl_b, u_l)
    y_c = finish(xc_f, xc_b, u_c) if need_ctx else None
    return y_c, y_l


def diff_branch(q_c, k_c, v_c, q_l, k_l, v_l, lam_params, sub_norm, lam_init, cos, sin, need_ctx):
    def qk_heads(t):
        return t.reshape(t.shape[:2] + (C_HEADS, 2, HEAD_DIM))

    def v_heads(t):
        return t.reshape(t.shape[:2] + (C_HEADS, 2 * HEAD_DIM))

    lp = lam_params.astype(jnp.float32)
    lam = jnp.exp(jnp.sum(lp[0] * lp[1])) - jnp.exp(jnp.sum(lp[2] * lp[3])) + lam_init
    scale = HEAD_DIM ** -0.5

    def attend(qb, k, v):
        s = jnp.einsum('bhmqd,bshmd->bhmqs', qb, k, preferred_element_type=jnp.float32) * scale
        p = jax.nn.softmax(s, axis=-1)
        w = (p[:, :, 0] - lam * p[:, :, 1]).astype(v.dtype)
        return jnp.einsum('bhqs,bshe->bhqe', w, v)

    def finish(o):
        o = rms_norm(o, sub_norm) * (1.0 - lam_init)
        return o.transpose(0, 2, 1, 3).reshape(o.shape[0], o.shape[2], BRANCH_W)

    def to_q(q):
        return q.transpose(0, 2, 3, 1, 4)

    qc, kc, vc = qk_heads(q_c), qk_heads(k_c), v_heads(v_c)
    ql = apply_rope(qk_heads(q_l), cos, sin)
    kl = apply_rope(qk_heads(k_l), cos, sin)
    k_all = jnp.concatenate([kc, kl], axis=1)
    v_all = jnp.concatenate([vc, v_heads(v_l)], axis=1)
    y_l = finish(over_query_blocks(lambda qb: attend(qb, k_all, v_all), to_q(ql)))
    y_c = finish(attend(to_q(qc), kc, vc)) if need_ctx else None
    return y_c, y_l


def retention_scan(q, k, v, log_gamma, s0, need_out):
    b, t, h, dk = q.shape
    nc = t // RET_CHUNK
    f32 = jnp.float32
    qf = q.astype(f32).reshape(b, nc, RET_CHUNK, h, dk)
    kf = k.astype(f32).reshape(b, nc, RET_CHUNK, h, dk)
    vf = v.astype(f32).reshape(b, nc, RET_CHUNK, h, v.shape[-1])
    pos = jnp.arange(RET_CHUNK, dtype=f32)
    k_tail = kf * jnp.exp((RET_CHUNK - 1.0 - pos)[:, None] * log_gamma)[:, :, None]
    chunk_kv = jnp.einsum('bnjhd,bnjhe->nbhde', k_tail, vf)
    chunk_decay = jnp.exp(RET_CHUNK * log_gamma)[:, None, None]

    def step(s, u):
        return chunk_decay * s + u, s

    s_final, s_in = lax.scan(step, s0.astype(f32), chunk_kv)
    if not need_out:
        return None, s_final
    rel = pos[:, None] - pos[None, :]
    dmat = jnp.where(rel >= 0, jnp.exp(jnp.maximum(rel, 0.0)[None] * log_gamma[:, None, None]), 0.0)
    scores = jnp.einsum('bnihd,bnjhd->bnhij', qf, kf) * dmat
    intra = jnp.einsum('bnhij,bnjhe->bnihe', scores, vf)
    q_head = qf * jnp.exp((pos + 1.0)[:, None] * log_gamma)[:, :, None]
    cross = jnp.einsum('bnihd,nbhde->bnihe', q_head, s_in)
    return (intra + cross).reshape(b, t, h, -1), s_final


def retention_branch(q_c, k_c, v_c, g_c, q_l, k_l, v_l, g_l, decay_logit, out_norm, cos, sin, need_ctx):
    def qk_heads(t):
        return t.reshape(t.shape[:2] + (R_HEADS, HEAD_DIM))

    def v_heads(t):
        return t.reshape(t.shape[:2] + (R_HEADS, R_DV))

    kscale = HEAD_DIM ** -0.5
    log_gamma = jax.nn.log_sigmoid(decay_logit.astype(jnp.float32))

    def flip(t):
        return jnp.flip(t, axis=1)

    qc, kc, vc = qk_heads(q_c), qk_heads(k_c) * kscale, v_heads(v_c)
    ql = apply_rope(qk_heads(q_l), cos, sin)
    kl = apply_rope(qk_heads(k_l), cos, sin) * kscale
    vl = v_heads(v_l)
    s0 = jnp.zeros((q_l.shape[0], R_HEADS, HEAD_DIM, R_DV), jnp.float32)
    oc_f, sc_f = retention_scan(qc, kc, vc, log_gamma[0], s0, need_ctx)
    oc_b, sc_b = retention_scan(flip(qc), flip(kc), flip(vc), log_gamma[1], s0, need_ctx)
    ol_f, _ = retention_scan(ql, kl, vl, log_gamma[0], sc_f, True)
    ol_b, _ = retention_scan(flip(ql), flip(kl), flip(vl), log_gamma[1], sc_b, True)

    def finish(o, g):
        y = rms_norm(o, out_norm).astype(g.dtype) * jax.nn.silu(v_heads(g))
        return y.reshape(g.shape)

    y_l = finish(ol_f + flip(ol_b), g_l)
    y_c = finish(oc_f + flip(oc_b), g_c) if need_ctx else None
    return y_c, y_l


def token_mixers(h_c, h_l, cos, sin, lam_init, need_ctx, w_in, q_norm, k_norm,
                 s5_a_re, s5_a_im, s5_log_step, s5_b_re, s5_b_im, s5_c_re, s5_c_im, s5_d, s5_w_glu,
                 diff_lambda, diff_norm, ret_decay_logit, ret_norm, w_branch, w_merge_gate, w_out):
    points = np.cumsum(IN_SPLITS)[:-1].tolist()
    pc = jnp.split(h_c @ w_in, points, axis=-1)
    pl = jnp.split(h_l @ w_in, points, axis=-1)
    ya = gqa_branch(pc[0], pc[1], pc[2], pl[0], pl[1], pl[2], q_norm, k_norm, cos, sin, need_ctx)
    yb = s5_branch(pc[3], pl[3], s5_a_re, s5_a_im, s5_log_step, s5_b_re, s5_b_im, s5_c_re, s5_c_im,
                   s5_d, s5_w_glu, need_ctx)
    yc = diff_branch(pc[4], pc[5], pc[6], pl[4], pl[5], pl[6], diff_lambda, diff_norm, lam_init,
                     cos, sin, need_ctx)
    yd = retention_branch(pc[7], pc[8], pc[9], pc[10], pl[7], pl[8], pl[9], pl[10],
                          ret_decay_logit, ret_norm, cos, sin, need_ctx)

    def merge(h, ys):
        acc = jnp.zeros_like(h)
        for j in range(N_BRANCH):
            acc = acc + jax.nn.sigmoid(h @ w_merge_gate[j]) * (ys[j] @ w_branch[j])
        return acc @ w_out

    y_l = merge(h_l, (ya[1], yb[1], yc[1], yd[1]))
    y_c = merge(h_c, (ya[0], yb[0], yc[0], yd[0])) if need_ctx else None
    return y_c, y_l


def swiglu(h, w1, w3, w2):
    return (jax.nn.silu(h @ w1) * (h @ w3)) @ w2


def moe_swiglu(h, router_w, router_b, w1, w3, w2):
    logits = jnp.einsum('btd,de->bte', h, router_w, preferred_element_type=jnp.float32)
    logits = logits + router_b.astype(jnp.float32)
    top_val, top_idx = lax.top_k(logits, TOP_K)
    top_w = jax.nn.softmax(top_val, axis=-1)
    gate = jnp.einsum('btk,btke->bte', top_w,
                      jax.nn.one_hot(top_idx, N_EXPERTS, dtype=jnp.float32)).astype(h.dtype)
    out = jnp.zeros_like(h)
    for e in range(N_EXPERTS):
        out = out + gate[..., e:e + 1] * swiglu(h, w1[e], w3[e], w2[e])
    return out


def channel_mixer(h, layer, ffn_w1, ffn_w3, ffn_w2, moe_router_w, moe_router_b, moe_w1, moe_w3, moe_w2):
    j = layer // 2
    if layer % 2 == 0:
        return swiglu(h, ffn_w1[j], ffn_w3[j], ffn_w2[j])
    return moe_swiglu(h, moe_router_w[j], moe_router_b[j], moe_w1[j], moe_w3[j], moe_w2[j])


def setup_inputs(seed: int = 0) -> dict:
    key = jax.random.key(seed)
    keys = iter(jax.random.split(key, 48))
    f32 = jnp.float32
    D = D_MODEL
    G, N, P = S5_GROUPS, S5_N, S5_P

    def nrm(shape, scale):
        return jax.random.normal(next(keys), shape, f32) * scale

    def gain(shape):
        return 1.0 + nrm(shape, 0.02)

    ret_base = jnp.log(2.0 ** (5.0 + jnp.arange(R_HEADS, dtype=f32)) - 1.0)
    return {
        'x': nrm((BATCH, SEQ, D), 1.0),
        'c': nrm((BATCH, D), 1.0),
        'ctx': nrm((BATCH, CTX_LEN, D), 1.0),
        'c_ctx': nrm((D,), 1.0),
        'ada_w': nrm((DEPTH, D, 6 * D), 0.4 * D ** -0.5),
        'ada_b': nrm((DEPTH, 6 * D), 0.02),
        'norm1_g': gain((DEPTH, D)),
        'norm2_g': gain((DEPTH, D)),
        'w_in': nrm((DEPTH, D, IN_COLS), D ** -0.5),
        'attn_q_norm': gain((DEPTH, HEAD_DIM)),
        'attn_k_norm': gain((DEPTH, HEAD_DIM)),
        's5_a_re': -0.5 + nrm((DEPTH, 2, G, N), 0.01),
        's5_a_im': math.pi * jnp.arange(N, dtype=f32) + nrm((DEPTH, 2, G, N), 0.01),
        's5_log_step': jax.random.uniform(next(keys), (DEPTH, 2, G), f32, math.log(1e-3), math.log(1e-1)),
        's5_b_re': nrm((DEPTH, 2, G, N, P), (2 * P) ** -0.5),
        's5_b_im': nrm((DEPTH, 2, G, N, P), (2 * P) ** -0.5),
        's5_c_re': nrm((DEPTH, 2, G, P, N), N ** -0.5),
        's5_c_im': nrm((DEPTH, 2, G, P, N), N ** -0.5),
        's5_d': nrm((DEPTH, BRANCH_W), 1.0),
        's5_w_glu': nrm((DEPTH, BRANCH_W, BRANCH_W), BRANCH_W ** -0.5),
        'diff_lambda': nrm((DEPTH, 4, HEAD_DIM), 0.1),
        'diff_norm': gain((DEPTH, 2 * HEAD_DIM)),
        'ret_decay_logit': ret_base + nrm((DEPTH, 2, R_HEADS), 0.01),
        'ret_norm': gain((DEPTH, R_DV)),
        'w_branch': nrm((DEPTH, N_BRANCH, BRANCH_W, D), BRANCH_W ** -0.5),
        'w_merge_gate': nrm((DEPTH, N_BRANCH, D, D), D ** -0.5),
        'w_out': nrm((DEPTH, D, D), D ** -0.5),
        'ffn_w1': nrm((N_DENSE, D, D_FF), D ** -0.5),
        'ffn_w3': nrm((N_DENSE, D, D_FF), D ** -0.5),
        'ffn_w2': nrm((N_DENSE, D_FF, D), D_FF ** -0.5),
        'moe_router_w': nrm((N_MOE, D, N_EXPERTS), D ** -0.5),
        'moe_router_b': nrm((N_MOE, N_EXPERTS), 0.01),
        'moe_w1': nrm((N_MOE, N_EXPERTS, D, MOE_FF), D ** -0.5),
        'moe_w3': nrm((N_MOE, N_EXPERTS, D, MOE_FF), D ** -0.5),
        'moe_w2': nrm((N_MOE, N_EXPERTS, MOE_FF, D), MOE_FF ** -0.5),
        'final_norm_g': gain((D,)),
    }


def reference(x, c, ctx, c_ctx, ada_w, ada_b, norm1_g, norm2_g, w_in, attn_q_norm, attn_k_norm,
              s5_a_re, s5_a_im, s5_log_step, s5_b_re, s5_b_im, s5_c_re, s5_c_im, s5_d, s5_w_glu,
              diff_lambda, diff_norm, ret_decay_logit, ret_norm, w_branch, w_merge_gate, w_out,
              ffn_w1, ffn_w3, ffn_w2, moe_router_w, moe_router_b, moe_w1, moe_w3, moe_w2, final_norm_g):
    cos, sin = grid_rope_tables(x.shape[1])
    x_l, x_c = x, ctx
    for i in range(DEPTH):
        need_ctx = i < DEPTH - 1
        mod_l = jnp.split((jax.nn.silu(c) @ ada_w[i] + ada_b[i])[:, None, :], 6, axis=-1)
        mod_c = jnp.split(jax.nn.silu(c_ctx) @ ada_w[i] + ada_b[i], 6, axis=-1)
        h_l = modulate(x_l, norm1_g[i], mod_l[0], mod_l[1])
        h_c = modulate(x_c, norm1_g[i], mod_c[0], mod_c[1])
        m_c, m_l = token_mixers(h_c, h_l, cos, sin, 0.8 - 0.6 * math.exp(-0.3 * i), need_ctx,
                                w_in[i], attn_q_norm[i], attn_k_norm[i],
                                s5_a_re[i], s5_a_im[i], s5_log_step[i], s5_b_re[i], s5_b_im[i],
                                s5_c_re[i], s5_c_im[i], s5_d[i], s5_w_glu[i],
                                diff_lambda[i], diff_norm[i], ret_decay_logit[i], ret_norm[i],
                                w_branch[i], w_merge_gate[i], w_out[i])
        x_l = x_l + mod_l[2] * m_l
        x_l = x_l + mod_l[5] * channel_mixer(modulate(x_l, norm2_g[i], mod_l[3], mod_l[4]), i,
                                             ffn_w1, ffn_w3, ffn_w2, moe_router_w, moe_router_b,
                                             moe_w1, moe_w3, moe_w2)
        if need_ctx:
            x_c = x_c + mod_c[2] * m_c
            x_c = x_c + mod_c[5] * channel_mixer(modulate(x_c, norm2_g[i], mod_c[3], mod_c[4]), i,
                                                 ffn_w1, ffn_w3, ffn_w2, moe_router_w, moe_router_b,
                                                 moe_w1, moe_w3, moe_w2)
    return rms_norm(x_l, final_norm_g)
```

```python
import functools
import math
from typing import Callable, NamedTuple

import jax
import jax.numpy as jnp
from jax import lax
from jax.experimental import pallas as pl
from jax.experimental.pallas import tpu as pltpu

F32 = jnp.float32
BF16 = jnp.bfloat16

HEAD_DIM = 128
GRID_W = 64
ROPE_BASE = 10000.0
ROPE_FREQS = HEAD_DIM // 4
EPS = 1e-6
A_GROUP = 4
S5_P = 16
S5_N = 64
S5_L = 16
S5_GROUPS_PER_STEP = 8
S5_ROW_PAD = 8
RET_CHUNK = 128
N_EXPERTS = 8
ROUTER_LANES = 128
V7X_VMEM_BYTES = 64 * 1024 * 1024
VMEM_HEADROOM_BYTES = 8 * 1024 * 1024


def _vmem_limit(nbytes):
    return int(min(nbytes + VMEM_HEADROOM_BYTES, V7X_VMEM_BYTES - VMEM_HEADROOM_BYTES))


def _nbytes(shape, dtype):
    return math.prod(shape) * jnp.dtype(dtype).itemsize


class RowTable(NamedTuple):
    index_of_row: Callable
    group_rows: int


def _tile(total, preferred):
    return math.gcd(total, preferred)


def _mm_kernel(*refs, nk, x_silu, epi):
    it = iter(refs)
    x_ref, w_ref = next(it), next(it)
    bias_ref = next(it) if epi == "bias" else None
    res_ref, gate_ref = (next(it), next(it)) if epi == "resgate" else (None, None)
    o_ref = next(it)
    acc_ref = next(it) if nk > 1 else None

    x = x_ref[...]
    if x_silu:
        x = x * jax.nn.sigmoid(x)
    part = jnp.dot(x.astype(BF16), w_ref[...].astype(BF16), preferred_element_type=F32)

    def finish(acc):
        if epi == "bias":
            acc = acc + bias_ref[...]
        elif epi == "resgate":
            acc = res_ref[...] + gate_ref[...] * acc
        o_ref[...] = acc.astype(o_ref.dtype)

    if nk == 1:
        finish(part)
    else:
        k = pl.program_id(2)

        @pl.when(k == 0)
        def _():
            acc_ref[...] = part

        @pl.when(k > 0)
        def _():
            acc_ref[...] += part

        @pl.when(k == nk - 1)
        def _():
            finish(acc_ref[...])


def _mm(x, w, wlead=(), *, m_rows=None, tm, tn, tk=None, nk=1, k0=0, order="nm", epi="none",
        out_dtype=F32, x_silu=False, bias=None, res=None, gate=None, tab=None):
    m_total = x.shape[0] if m_rows is None else m_rows
    n_total = w.shape[-1]
    tk = x.shape[1] if tk is None else tk
    tm, tn = _tile(m_total, tm), _tile(n_total, tn)
    if tab is not None:
        tm = _tile(tab.group_rows, tm)
    gm, gn = m_total // tm, n_total // tn
    assert gm * tm == m_total and gn * tn == n_total, (m_total, tm, n_total, tn)
    nlead = len(wlead)

    if order == "nm":
        grid = (gn, gm, nk)
        mnk = lambda a, b, k: (b, a, k)
    else:
        grid = (gm, gn, nk)
        mnk = lambda a, b, k: (a, b, k)

    def imap(f):
        return lambda a, b, k: f(*mnk(a, b, k))

    in_specs = [
        pl.BlockSpec((tm, tk), imap(lambda m, n, k: (m, k0 + k))),
        pl.BlockSpec((None,) * nlead + (tk, tn), imap(lambda m, n, k: tuple(wlead) + (k0 + k, n))),
    ]
    args = [x, w]
    est = 2 * _nbytes((tm, tk), x.dtype) + 2 * _nbytes((tk, tn), w.dtype) + _nbytes((tk, tn), BF16)
    if epi == "bias":
        in_specs.append(pl.BlockSpec((1, tn), imap(lambda m, n, k: (0, n))))
        args.append(bias)
    elif epi == "resgate":
        in_specs.append(pl.BlockSpec((tm, tn), imap(lambda m, n, k: (m, n))))
        in_specs.append(pl.BlockSpec((None, 1, tn), imap(lambda m, n, k: (tab.index_of_row(m * tm), 0, n))))
        args += [res, gate]
        est += 2 * _nbytes((tm, tn), F32)
    est += 2 * _nbytes((tm, tn), out_dtype) + 2 * _nbytes((tm, tn), F32)
    scratch = [pltpu.VMEM((tm, tn), F32)] if nk > 1 else []
    return pl.pallas_call(
        functools.partial(_mm_kernel, nk=nk, x_silu=x_silu, epi=epi),
        grid=grid,
        in_specs=in_specs,
        out_specs=pl.BlockSpec((tm, tn), imap(lambda m, n, k: (m, n))),
        out_shape=jax.ShapeDtypeStruct((m_total, n_total), out_dtype),
        scratch_shapes=scratch,
        compiler_params=pltpu.CompilerParams(
            dimension_semantics=("parallel", "parallel", "arbitrary"),
            vmem_limit_bytes=_vmem_limit(est)),
    )(*args)


def _rms(x):
    return x * lax.rsqrt(jnp.mean(x * x, axis=-1, keepdims=True) + EPS)


def _modnorm_kernel(x_ref, g_ref, sh_ref, sc_ref, *rest, router):
    h = _rms(x_ref[...]) * g_ref[...] * (1.0 + sc_ref[...]) + sh_ref[...]
    if not router:
        (o_ref,) = rest
        o_ref[...] = h.astype(o_ref.dtype)
        return
    rw_ref, rb_ref, o_ref, gate_ref = rest
    o_ref[...] = h.astype(o_ref.dtype)
    logits = jnp.dot(h, rw_ref[...], preferred_element_type=F32,
                     precision=lax.Precision.HIGHEST) + rb_ref[...]
    lane = lax.broadcasted_iota(jnp.int32, logits.shape, 1).astype(F32)
    neg = jnp.float32(-jnp.inf)
    logits = jnp.where(lane < N_EXPERTS, logits, neg)
    v1 = jnp.max(logits, axis=-1, keepdims=True)
    i1 = jnp.min(jnp.where(logits == v1, lane, float(ROUTER_LANES)), axis=-1, keepdims=True)
    rest_l = jnp.where(lane == i1, neg, logits)
    v2 = jnp.max(rest_l, axis=-1, keepdims=True)
    i2 = jnp.min(jnp.where(rest_l == v2, lane, float(ROUTER_LANES)), axis=-1, keepdims=True)
    e2 = jnp.exp(v2 - v1)
    w1 = 1.0 / (1.0 + e2)
    gate_ref[...] = jnp.where(lane == i1, w1, jnp.where(lane == i2, e2 * w1, 0.0))


def _modnorm(x, g, shift_tab, scale_tab, tab, *, tm=256, router_w=None, router_b=None):
    m_total, d = x.shape
    tm = _tile(tab.group_rows, _tile(m_total, tm))
    row = pl.BlockSpec((tm, d), lambda m: (m, 0))
    vec = pl.BlockSpec((1, d), lambda m: (0, 0))
    per_group = pl.BlockSpec((None, 1, d), lambda m: (tab.index_of_row(m * tm), 0, 0))
    in_specs, args = [row, vec, per_group, per_group], [x, g.reshape(1, d), shift_tab, scale_tab]
    out_specs, out_shape = row, jax.ShapeDtypeStruct((m_total, d), BF16)
    router = router_w is not None
    est = 2 * _nbytes((tm, d), F32) * 3
    if router:
        rw = jnp.zeros((d, ROUTER_LANES), F32).at[:, :N_EXPERTS].set(router_w)
        rb = jnp.zeros((1, ROUTER_LANES), F32).at[0, :N_EXPERTS].set(router_b)
        in_specs += [pl.BlockSpec((d, ROUTER_LANES), lambda m: (0, 0)),
                     pl.BlockSpec((1, ROUTER_LANES), lambda m: (0, 0))]
        args += [rw, rb]
        out_specs = (row, pl.BlockSpec((tm, ROUTER_LANES), lambda m: (m, 0)))
        out_shape = (out_shape, jax.ShapeDtypeStruct((m_total, ROUTER_LANES), F32))
        est += 2 * _nbytes((d, ROUTER_LANES), F32) * 4
    return pl.pallas_call(
        functools.partial(_modnorm_kernel, router=router),
        grid=(m_total // tm,), in_specs=in_specs, out_specs=out_specs, out_shape=out_shape,
        compiler_params=pltpu.CompilerParams(dimension_semantics=("parallel",),
                                             vmem_limit_bytes=_vmem_limit(est)),
    )(*args)


def _final_norm_kernel(x_ref, g_ref, o_ref):
    o_ref[...] = _rms(x_ref[...]) * g_ref[...]


def _final_norm(x, g, *, tm=256):
    m_total, d = x.shape
    tm = _tile(m_total, tm)
    row = pl.BlockSpec((tm, d), lambda m: (m, 0))
    return pl.pallas_call(
        _final_norm_kernel, grid=(m_total // tm,),
        in_specs=[row, pl.BlockSpec((1, d), lambda m: (0, 0))], out_specs=row,
        out_shape=jax.ShapeDtypeStruct((m_total, d), F32),
        compiler_params=pltpu.CompilerParams(dimension_semantics=("parallel",),
                                             vmem_limit_bytes=_vmem_limit(6 * _nbytes((tm, d), F32))),
    )(x, g.reshape(1, d))


def _prep_kernel(x_ref, cos_ref, sin_ref, gain_ref, nflag_ref, o_ref):
    x = x_ref[...]
    inv = lax.rsqrt(jnp.mean(x * x, axis=-1, keepdims=True) + EPS)
    nf = nflag_ref[...]
    y = x * (nf * inv + (1.0 - nf)) * gain_ref[...]
    lane = lax.broadcasted_iota(jnp.int32, y.shape, 1)
    first_half = (lane % (2 * ROPE_FREQS)) < ROPE_FREQS
    partner = jnp.where(first_half, pltpu.roll(y, HEAD_DIM - ROPE_FREQS, 1), pltpu.roll(y, ROPE_FREQS, 1))
    o_ref[...] = (y * cos_ref[...] + partner * sin_ref[...]).astype(o_ref.dtype)


def _prep_src_block(j):
    return jnp.where(j < 10, j, jnp.where(j < 26, j + 10, j + 18))


N_PREP_BLOCKS = 34


def _prep(p, cos_t, sin_t, gain, nflag, *, tm=512):
    m_total = p.shape[0]
    tm = _tile(m_total, tm)
    return pl.pallas_call(
        _prep_kernel, grid=(m_total // tm, N_PREP_BLOCKS),
        in_specs=[pl.BlockSpec((tm, HEAD_DIM), lambda m, j: (m, _prep_src_block(j))),
                  pl.BlockSpec((tm, HEAD_DIM), lambda m, j: (m, 0)),
                  pl.BlockSpec((tm, HEAD_DIM), lambda m, j: (m, 0)),
                  pl.BlockSpec((None, 1, HEAD_DIM), lambda m, j: (j, 0, 0)),
                  pl.BlockSpec((None, 1, HEAD_DIM), lambda m, j: (j, 0, 0))],
        out_specs=pl.BlockSpec((tm, HEAD_DIM), lambda m, j: (m, j)),
        out_shape=jax.ShapeDtypeStruct((m_total, N_PREP_BLOCKS * HEAD_DIM), BF16),
        compiler_params=pltpu.CompilerParams(dimension_semantics=("parallel", "arbitrary")),
    )(p, cos_t, sin_t, gain, nflag)


def _rope_tables(t_lat, n_batch, t_ctx):
    rows = t_lat // GRID_W
    row = jnp.broadcast_to(jnp.arange(rows)[:, None], (rows, GRID_W)).reshape(-1)
    col = jnp.broadcast_to(jnp.arange(GRID_W)[None, :], (rows, GRID_W)).reshape(-1)
    inv = ROPE_BASE ** (-jnp.arange(ROPE_FREQS, dtype=F32) / ROPE_FREQS)
    ang = jnp.stack([row, col], axis=-1).astype(F32)[:, :, None] * inv
    cos, sin = jnp.cos(ang), jnp.sin(ang)
    cos128 = jnp.concatenate([cos, cos], axis=-1).reshape(t_lat, HEAD_DIM)
    sin128 = jnp.concatenate([-sin, sin], axis=-1).reshape(t_lat, HEAD_DIM)
    cos_t = jnp.concatenate([jnp.tile(cos128, (n_batch, 1)), jnp.ones((n_batch * t_ctx, HEAD_DIM), F32)])
    sin_t = jnp.concatenate([jnp.tile(sin128, (n_batch, 1)), jnp.zeros((n_batch * t_ctx, HEAD_DIM), F32)])
    return cos_t, sin_t


_NT = (((1,), (1,)), ((), ()))


def _softmax_parts(q, ks, scale):
    ss = [lax.dot_general(q, k, _NT, preferred_element_type=F32) * scale for k in ks]
    m = functools.reduce(jnp.maximum, [jnp.max(s, axis=-1, keepdims=True) for s in ss])
    es = [jnp.exp(s - m) for s in ss]
    l = functools.reduce(jnp.add, [jnp.sum(e, axis=-1, keepdims=True) for e in es])
    return es, l


def _gqa_kernel(q_ref, *rest, nseg, scale):
    k_refs, v_refs, o_ref = rest[:nseg], rest[nseg:2 * nseg], rest[2 * nseg]
    ks = [r[...] for r in k_refs]
    vs = [r[...].astype(BF16) for r in v_refs]
    for g in range(A_GROUP):
        cols = slice(g * HEAD_DIM, (g + 1) * HEAD_DIM)
        es, l = _softmax_parts(q_ref[:, cols], ks, scale)
        o = functools.reduce(jnp.add, [jnp.dot(e.astype(BF16), v, preferred_element_type=F32)
                                       for e, v in zip(es, vs)])
        o_ref[:, cols] = (o / l).astype(o_ref.dtype)


def _diff_kernel(lam_ref, q_ref, *rest, nseg, scale, out_scale):
    k_refs, v_refs = rest[:nseg], rest[nseg:2 * nseg]
    gain_ref, o_ref = rest[2 * nseg], rest[2 * nseg + 1]
    lam = lam_ref[0]
    vs = [r[...].astype(BF16) for r in v_refs]
    parts = []
    for m in range(2):
        cols = slice(m * HEAD_DIM, (m + 1) * HEAD_DIM)
        es, l = _softmax_parts(q_ref[:, cols], [r[:, cols] for r in k_refs], scale)
        parts.append((es, 1.0 / l))
    (e0, r0), (e1, r1) = parts
    o = functools.reduce(jnp.add, [
        jnp.dot((a * r0 - lam * (b * r1)).astype(BF16), v, preferred_element_type=F32)
        for a, b, v in zip(e0, e1, vs)])
    o_ref[...] = (_rms(o) * gain_ref[...] * out_scale).astype(o_ref.dtype)


def _attention(kernel, qk, p, *, n_batch, t_lat, t_ctx, heads, q_w, q_blk0, k_w, k_blk0, v_w, v_blk0,
               out_w, ctx_queries, tq, extra_in=(), extra_specs=(), lead_in=(), lead_specs=()):
    n_lat = n_batch * t_lat
    ctx_row0 = n_lat // t_ctx
    tq = _tile(t_lat, tq)
    if ctx_queries:
        nq, q_rows = 1, t_ctx
        q_map = lambda b, h, i: (ctx_row0 + b, q_blk0 + h)
        out_rows, o_map = n_batch * t_ctx, (lambda b, h, i: (b, h))
        segs = [(t_ctx, lambda b: ctx_row0 + b)]
    else:
        nq, q_rows = t_lat // tq, tq
        q_map = lambda b, h, i: (b * nq + i, q_blk0 + h)
        out_rows, o_map = n_lat, (lambda b, h, i: (b * nq + i, h))
        segs = [(t_ctx, lambda b: ctx_row0 + b), (t_lat, lambda b: b)]
    k_specs = [pl.BlockSpec((rows, k_w), functools.partial(lambda rf, b, h, i: (rf(b), k_blk0 + h), rf))
               for rows, rf in segs]
    v_specs = [pl.BlockSpec((rows, v_w), functools.partial(lambda rf, b, h, i: (rf(b), v_blk0 + h), rf))
               for rows, rf in segs]
    nseg = len(segs)
    est = 8 * _nbytes((q_rows, t_lat + t_ctx), F32) + 4 * _nbytes((t_lat + t_ctx, v_w), F32)
    return pl.pallas_call(
        functools.partial(kernel, nseg=nseg),
        grid=(n_batch, heads, nq),
        in_specs=list(lead_specs) + [pl.BlockSpec((q_rows, q_w), q_map)] + k_specs + v_specs + list(extra_specs),
        out_specs=pl.BlockSpec((q_rows, out_w), o_map),
        out_shape=jax.ShapeDtypeStruct((out_rows, heads * out_w), BF16),
        compiler_params=pltpu.CompilerParams(dimension_semantics=("parallel", "parallel", "arbitrary"),
                                             vmem_limit_bytes=_vmem_limit(est)),
    )(*lead_in, qk, *([qk] * nseg), *([p] * nseg), *extra_in)


def _ret_kernel(lg_ref, ql_ref, qc_ref, kl_ref, kc_ref, vl_ref, vc_ref, gl_ref, gc_ref, gain_ref,
                yl_ref, yc_ref, ol_scr, oc_scr, s_scr, *, t_lat, t_ctx):
    h = pl.program_id(1)
    lg_f, lg_b = lg_ref[0, h], lg_ref[1, h]
    c = RET_CHUNK
    pos_i = lax.broadcasted_iota(jnp.int32, (c, c), 0).astype(F32)
    pos_j = lax.broadcasted_iota(jnp.int32, (c, c), 1).astype(F32)
    rel = pos_i - pos_j
    d_both = (jnp.where(rel >= 0, jnp.exp(jnp.maximum(rel, 0.0) * lg_f), 0.0)
              + jnp.where(rel <= 0, jnp.exp(jnp.maximum(-rel, 0.0) * lg_b), 0.0))
    pos = lax.broadcasted_iota(jnp.int32, (c, 1), 0).astype(F32)
    qdec = (jnp.exp((pos + 1.0) * lg_f), jnp.exp((c - pos) * lg_b))
    kdec = (jnp.exp((c - 1.0 - pos) * lg_f), jnp.exp(pos * lg_b))
    one = jnp.ones((1, 1), F32)
    sdec = (jnp.exp(one * (c * lg_f)), jnp.exp(one * (c * lg_b)))

    chunks = [(qc_ref, kc_ref, vc_ref, oc_scr, i) for i in range(t_ctx // c)]
    chunks += [(ql_ref, kl_ref, vl_ref, ol_scr, i) for i in range(t_lat // c)]
    back = ([ch for ch in chunks if ch[0] is qc_ref][::-1] + [ch for ch in chunks if ch[0] is ql_ref][::-1])

    for d, order in ((0, chunks), (1, back)):
        s_scr[...] = jnp.zeros_like(s_scr)
        for q_ref, k_ref, v_ref, o_scr, i in order:
            rows = pl.ds(i * c, c)
            q, k = q_ref[rows, :], k_ref[rows, :]
            v = v_ref[rows, :].astype(BF16)
            state = s_scr[...]
            cross = jnp.dot((q.astype(F32) * qdec[d]).astype(BF16), state.astype(BF16),
                            preferred_element_type=F32)
            if d == 0:
                scores = lax.dot_general(q, k, _NT, preferred_element_type=F32) * d_both
                o_scr[rows, :] = cross + jnp.dot(scores.astype(BF16), v, preferred_element_type=F32)
            else:
                o_scr[rows, :] += cross
            kv = jnp.dot((k.astype(F32) * kdec[d]).T.astype(BF16), v, preferred_element_type=F32)
            s_scr[...] = sdec[d] * state + kv

    gain = gain_ref[...]
    for o_scr, g_ref, y_ref in ((ol_scr, gl_ref, yl_ref), (oc_scr, gc_ref, yc_ref)):
        g = g_ref[...]
        y_ref[...] = (_rms(o_scr[...]) * gain * (g * jax.nn.sigmoid(g))).astype(y_ref.dtype)


def _retention(qk, p, log_gamma, gain, *, n_batch, t_lat, t_ctx, heads, q_blk0, k_blk0, v_blk0, g_blk0, dv):
    n_lat = n_batch * t_lat
    ctx_row0 = n_lat // t_ctx
    lat = lambda w, blk0: pl.BlockSpec((t_lat, w), lambda b, h: (b, blk0 + h))
    ctx = lambda w, blk0: pl.BlockSpec((t_ctx, w), lambda b, h: (ctx_row0 + b, blk0 + h))
    est = 6 * _nbytes((t_lat + t_ctx, dv), F32) * 2
    return pl.pallas_call(
        functools.partial(_ret_kernel, t_lat=t_lat, t_ctx=t_ctx),
        grid=(n_batch, heads),
        in_specs=[pl.BlockSpec(memory_space=pltpu.SMEM),
                  lat(HEAD_DIM, q_blk0), ctx(HEAD_DIM, q_blk0), lat(HEAD_DIM, k_blk0), ctx(HEAD_DIM, k_blk0),
                  lat(dv, v_blk0), ctx(dv, v_blk0), lat(dv, g_blk0), ctx(dv, g_blk0),
                  pl.BlockSpec((1, dv), lambda b, h: (0, 0))],
        out_specs=(pl.BlockSpec((t_lat, dv), lambda b, h: (b, h)),
                   pl.BlockSpec((t_ctx, dv), lambda b, h: (b, h))),
        out_shape=(jax.ShapeDtypeStruct((n_lat, heads * dv), BF16),
                   jax.ShapeDtypeStruct((n_batch * t_ctx, heads * dv), BF16)),
        scratch_shapes=[pltpu.VMEM((t_lat, dv), F32), pltpu.VMEM((t_ctx, dv), F32),
                        pltpu.VMEM((HEAD_DIM, dv), F32)],
        compiler_params=pltpu.CompilerParams(dimension_semantics=("parallel", "arbitrary"),
                                             vmem_limit_bytes=_vmem_limit(est)),
    )(log_gamma, qk, qk, qk, qk, p, p, p, p, gain.reshape(1, dv))


def _s5_tables(a_re, a_im, log_step, b_re, b_im, c_re, c_im):
    hp = lax.Precision.HIGHEST
    n_dir, n_grp, n_st = a_re.shape
    ln = S5_L
    lam = lax.complex(a_re.astype(F32), a_im.astype(F32))
    step = jnp.exp(log_step.astype(F32))[..., None]
    a_bar = jnp.exp(lam * step)
    b_bar = ((a_bar - 1.0) / lam)[..., None] * lax.complex(b_re.astype(F32), b_im.astype(F32))
    c_mat = lax.complex(c_re.astype(F32), c_im.astype(F32))
    taus = jnp.arange(ln + 1, dtype=F32)
    apow = jnp.exp((lam * step)[..., None] * taus)

    kern = jnp.einsum("dgpn,dgnt,dgnq->dgtpq", c_mat, apow[..., :ln], b_bar, precision=hp).real
    ti = jnp.arange(ln)[:, None]
    tj = jnp.arange(ln)[None, :]
    t_f = kern[0][:, jnp.clip(ti - tj, 0)] * (ti >= tj)[None, :, :, None, None]
    t_b = kern[1][:, jnp.clip(tj - ti, 0)] * (tj >= ti)[None, :, :, None, None]
    m_intra = (t_f + t_b).transpose(0, 2, 4, 1, 3).reshape(n_grp, ln * S5_P, ln * S5_P)

    def pair_rows(x):
        g2 = x.reshape(n_grp // 2, 2, *x.shape[1:])
        z = jnp.zeros_like(g2[:, 0])
        return jnp.concatenate([jnp.concatenate([g2[:, 0], z], axis=2),
                                jnp.concatenate([z, g2[:, 1]], axis=2)], axis=1)

    def in_mat(d, tau_of_j):
        w = apow[d][:, :, tau_of_j][..., None] * b_bar[d][:, :, None, :]
        w = w.transpose(0, 2, 3, 1).reshape(n_grp, ln * S5_P, n_st)
        return pair_rows(w.real), pair_rows(w.imag)

    def out_mat(d, tau_of_i):
        w = c_mat[d][:, :, :, None] * apow[d][:, None, :, tau_of_i]
        w = w.transpose(0, 2, 3, 1).reshape(n_grp, n_st, ln * S5_P)
        return pair_rows(w.real), pair_rows(-w.imag)

    idx = jnp.arange(ln)
    mats = [m_intra, *in_mat(0, ln - 1 - idx), *in_mat(1, idx), *out_mat(0, idx + 1), *out_mat(1, ln - idx)]
    dec = [apow[0][..., ln].real, apow[0][..., ln].imag, apow[1][..., ln].real, apow[1][..., ln].imag]
    return [m.astype(BF16) for m in mats], [x.reshape(1, n_grp * n_st) for x in dec]


def _s5_kernel(u_ref, m_ref, bfr_ref, bfi_ref, bbr_ref, bbi_ref, cfr_ref, cfi_ref, cbr_ref, cbi_ref,
               dfr_ref, dfi_ref, dbr_ref, dbi_ref, y_ref, vfr, vfi, vbr, vbi, *, n_r, n_ctx_r):
    gw = S5_L * S5_P
    sw = 2 * S5_N
    n_pair = S5_GROUPS_PER_STEP // 2
    for g in range(S5_GROUPS_PER_STEP):
        cols = slice(g * gw, (g + 1) * gw)
        y_ref[:, cols] = jnp.dot(u_ref[:, cols], m_ref[g], preferred_element_type=F32)
    for q in range(n_pair):
        up = u_ref[:, q * 2 * gw:(q + 1) * 2 * gw]
        for v_scr, w_ref in ((vfr, bfr_ref), (vfi, bfi_ref), (vbr, bbr_ref), (vbi, bbi_ref)):
            v_scr[:, q * sw:(q + 1) * sw] = jnp.dot(up, w_ref[q], preferred_element_type=F32)

    def scan(v_re, v_im, d_re, d_im, row_of):
        def step(i, carry):
            s_re, s_im = carry
            rows = pl.ds(pl.multiple_of(row_of(i) * S5_ROW_PAD, S5_ROW_PAD), S5_ROW_PAD)
            n_re, n_im = v_re[rows, :], v_im[rows, :]
            v_re[rows, :] = s_re
            v_im[rows, :] = s_im
            return d_re * s_re - d_im * s_im + n_re, d_re * s_im + d_im * s_re + n_im

        zero = jnp.zeros((S5_ROW_PAD, v_re.shape[1]), F32)
        lax.fori_loop(0, n_r, step, (zero, zero))

    scan(vfr, vfi, dfr_ref[...], dfi_ref[...], lambda i: i)
    scan(vbr, vbi, dbr_ref[...], dbi_ref[...],
         lambda i: jnp.where(i < n_ctx_r, n_ctx_r - 1 - i, n_r + n_ctx_r - 1 - i))

    for q in range(n_pair):
        lanes = slice(q * sw, (q + 1) * sw)
        acc = functools.reduce(jnp.add, [
            jnp.dot(v_scr[:, lanes].astype(BF16), w_ref[q], preferred_element_type=F32)
            for v_scr, w_ref in ((vfr, cfr_ref), (vfi, cfi_ref), (vbr, cbr_ref), (vbi, cbi_ref))])
        y_ref[:, q * 2 * gw:(q + 1) * 2 * gw] += acc


def _s5_core(u_chunked, mats, decs, *, n_r, n_ctx_r):
    rows, cols = u_chunked.shape
    gps = S5_GROUPS_PER_STEP
    gw = S5_L * S5_P
    n_steps = cols // (gps * gw)
    sl = gps * S5_N
    mat_specs = [pl.BlockSpec((gps,) + mats[0].shape[1:], lambda j: (j, 0, 0))]
    mat_specs += [pl.BlockSpec((gps // 2,) + m.shape[1:], lambda j: (j, 0, 0)) for m in mats[1:]]
    dec_specs = [pl.BlockSpec((1, sl), lambda j: (0, j)) for _ in decs]
    est = (2 * _nbytes((rows, gps * gw), BF16) + 2 * _nbytes((rows, gps * gw), F32)
           + 4 * _nbytes((rows, sl), F32) + 4 * _nbytes((gps, gw, gw), BF16) * 3)
    return pl.pallas_call(
        functools.partial(_s5_kernel, n_r=n_r, n_ctx_r=n_ctx_r),
        grid=(n_steps,),
        in_specs=[pl.BlockSpec((rows, gps * gw), lambda j: (0, j))] + mat_specs + dec_specs,
        out_specs=pl.BlockSpec((rows, gps * gw), lambda j: (0, j)),
        out_shape=jax.ShapeDtypeStruct((rows, cols), F32),
        scratch_shapes=[pltpu.VMEM((rows, sl), F32) for _ in range(4)],
        compiler_params=pltpu.CompilerParams(dimension_semantics=("parallel",),
                                             vmem_limit_bytes=_vmem_limit(est)),
    )(u_chunked, *mats, *decs)


def _s5_finish_kernel(y_ref, u_ref, d_ref, w_ref, o_ref):
    z = jax.nn.gelu(y_ref[...] + d_ref[...] * u_ref[...])
    gl = jnp.dot(z.astype(BF16), w_ref[...].astype(BF16), preferred_element_type=F32)
    o_ref[...] = (z * jax.nn.sigmoid(gl)).astype(o_ref.dtype)


def _s5_finish(y, u, d_skip, w_glu, wlead, *, tm=512):
    m_total, w = y.shape
    tm = _tile(m_total, tm)
    row = pl.BlockSpec((tm, w), lambda m: (m, 0))
    nlead = len(wlead)
    return pl.pallas_call(
        _s5_finish_kernel, grid=(m_total // tm,),
        in_specs=[row, row, pl.BlockSpec((1, w), lambda m: (0, 0)),
                  pl.BlockSpec((None,) * nlead + (w, w), lambda m: tuple(wlead) + (0, 0))],
        out_specs=row, out_shape=jax.ShapeDtypeStruct((m_total, w), BF16),
        compiler_params=pltpu.CompilerParams(
            dimension_semantics=("parallel",),
            vmem_limit_bytes=_vmem_limit(8 * _nbytes((tm, w), F32) + 3 * _nbytes((w, w), F32))),
    )(y, u, d_skip.reshape(1, w), w_glu)


def _s5_branch(u, params, layer, *, n_batch, t_lat, t_ctx):
    n_lat, bw = n_batch * t_lat, u.shape[1]
    n_grp = bw // S5_P
    t_all = t_ctx + t_lat
    n_r, n_ctx_r = t_all // S5_L, t_ctx // S5_L
    seq = jnp.concatenate([u[n_lat:].reshape(n_batch, t_ctx, bw), u[:n_lat].reshape(n_batch, t_lat, bw)], axis=1)
    uc = seq.reshape(n_batch, n_r, S5_L, n_grp, S5_P).transpose(1, 0, 3, 2, 4).astype(BF16)
    uc = jnp.pad(uc, ((0, 0), (0, S5_ROW_PAD - n_batch), (0, 0), (0, 0), (0, 0)))
    uc = uc.reshape(n_r * S5_ROW_PAD, n_grp * S5_L * S5_P)
    mats, decs = _s5_tables(*(params[k][layer] for k in
                              ("s5_a_re", "s5_a_im", "s5_log_step", "s5_b_re", "s5_b_im", "s5_c_re", "s5_c_im")))
    yc = _s5_core(uc, mats, decs, n_r=n_r, n_ctx_r=n_ctx_r)
    yc = yc.reshape(n_r, S5_ROW_PAD, n_grp, S5_L, S5_P)[:, :n_batch].transpose(1, 0, 3, 2, 4)
    yc = yc.reshape(n_batch, t_all, bw)
    y = jnp.concatenate([yc[:, t_ctx:].reshape(n_lat, bw), yc[:, :t_ctx].reshape(n_batch * t_ctx, bw)])
    return _s5_finish(y, u, params["s5_d"][layer], params["s5_w_glu"], (layer,))


def _merge_kernel(h_ref, y_ref, wg_ref, wb_ref, o_ref, acc_ref, *, n_branch):
    j = pl.program_id(2)
    gate = jnp.dot(h_ref[...], wg_ref[...].astype(BF16), preferred_element_type=F32)
    val = jnp.dot(y_ref[...], wb_ref[...].astype(BF16), preferred_element_type=F32)
    term = jax.nn.sigmoid(gate) * val

    @pl.when(j == 0)
    def _():
        acc_ref[...] = term

    @pl.when(j > 0)
    def _():
        acc_ref[...] += term

    @pl.when(j == n_branch - 1)
    def _():
        o_ref[...] = acc_ref[...].astype(o_ref.dtype)


def _merge(h, ycat, w_gate, w_branch, layer, *, m_rows, tm=1024, tn=256):
    d = h.shape[1]
    tm = _tile(m_rows, tm)
    n_branch, bw = w_branch.shape[1], w_branch.shape[2]
    est = (2 * _nbytes((tm, d), BF16) + 2 * _nbytes((tm, bw), BF16) + 3 * _nbytes((d, tn), F32)
           + 3 * _nbytes((bw, tn), F32) + 5 * _nbytes((tm, tn), F32))
    return pl.pallas_call(
        functools.partial(_merge_kernel, n_branch=n_branch),
        grid=(m_rows // tm, d // tn, n_branch),
        in_specs=[pl.BlockSpec((tm, d), lambda m, n, j: (m, 0)),
                  pl.BlockSpec((tm, bw), lambda m, n, j: (m, j)),
                  pl.BlockSpec((None, None, d, tn), lambda m, n, j: (layer, j, 0, n)),
                  pl.BlockSpec((None, None, bw, tn), lambda m, n, j: (layer, j, 0, n))],
        out_specs=pl.BlockSpec((tm, tn), lambda m, n, j: (m, n)),
        out_shape=jax.ShapeDtypeStruct((m_rows, d), BF16),
        scratch_shapes=[pltpu.VMEM((tm, tn), F32)],
        compiler_params=pltpu.CompilerParams(dimension_semantics=("parallel", "parallel", "arbitrary"),
                                             vmem_limit_bytes=_vmem_limit(est)),
    )(h, ycat, w_gate, w_branch)


def _swiglu_up_kernel(x_ref, w1_ref, w3_ref, *rest, gated):
    x = x_ref[...]
    a = jnp.dot(x, w1_ref[...].astype(BF16), preferred_element_type=F32)
    b = jnp.dot(x, w3_ref[...].astype(BF16), preferred_element_type=F32)
    u = a * jax.nn.sigmoid(a) * b
    if gated:
        gate_ref, o_ref = rest
        u = u * gate_ref[...]
    else:
        (o_ref,) = rest
    o_ref[...] = u.astype(o_ref.dtype)


def _swiglu_up(x, w1, w3, wlead, *, n_expert=1, gate=None, tm=1024, tn=256):
    m_total, d = x.shape
    tm = _tile(m_total, tm)
    ff = w1.shape[-1]
    nf = ff // tn
    nlead = len(wlead)
    has_e = n_expert > 1
    w_spec = pl.BlockSpec((None,) * (nlead + has_e) + (d, tn),
                          lambda n, m: tuple(wlead) + ((n // nf,) if has_e else ()) + (0, n % nf))
    in_specs, args = [pl.BlockSpec((tm, d), lambda n, m: (m, 0)), w_spec, w_spec], [x, w1, w3]
    if gate is not None:
        in_specs.append(pl.BlockSpec((None, tm, 1), lambda n, m: (n // nf, m, 0)))
        args.append(gate)
    est = 2 * _nbytes((tm, d), BF16) + 6 * _nbytes((d, tn), F32) + 6 * _nbytes((tm, tn), F32)
    return pl.pallas_call(
        functools.partial(_swiglu_up_kernel, gated=gate is not None),
        grid=(n_expert * nf, m_total // tm),
        in_specs=in_specs,
        out_specs=pl.BlockSpec((tm, tn), lambda n, m: (m, n)),
        out_shape=jax.ShapeDtypeStruct((m_total, n_expert * ff), BF16),
        compiler_params=pltpu.CompilerParams(dimension_semantics=("parallel", "arbitrary"),
                                             vmem_limit_bytes=_vmem_limit(est)),
    )(*args)


def kernel(x, c, ctx, c_ctx, ada_w, ada_b, norm1_g, norm2_g, w_in, attn_q_norm, attn_k_norm, s5_a_re, s5_a_im, s5_log_step, s5_b_re, s5_b_im, s5_c_re, s5_c_im, s5_d, s5_w_glu, diff_lambda, diff_norm, ret_decay_logit, ret_norm, w_branch, w_merge_gate, w_out, ffn_w1, ffn_w3, ffn_w2, moe_router_w, moe_router_b, moe_w1, moe_w3, moe_w2, final_norm_g):
    n_batch, t_lat, d = x.shape
    t_ctx = ctx.shape[1]
    depth = w_in.shape[0]
    n_lat, n_ctx = n_batch * t_lat, n_batch * t_ctx
    bw = d // 4
    s5_params = dict(s5_a_re=s5_a_re, s5_a_im=s5_a_im, s5_log_step=s5_log_step, s5_b_re=s5_b_re,
                     s5_b_im=s5_b_im, s5_c_re=s5_c_re, s5_c_im=s5_c_im, s5_d=s5_d, s5_w_glu=s5_w_glu)

    tab = RowTable(lambda row: jnp.where(row < n_lat, row // t_lat, n_batch), t_lat)

    cos_t, sin_t = _rope_tables(t_lat, n_batch, t_ctx)
    ada_rows = 8
    c_all = jnp.zeros((ada_rows, d), F32).at[:n_batch].set(c).at[n_batch].set(c_ctx)
    xs = jnp.concatenate([x.reshape(n_lat, d), ctx.reshape(n_ctx, d)])

    for i in range(depth):
        need_ctx = i < depth - 1
        lam_init = 0.8 - 0.6 * math.exp(-0.3 * i)
        mod = _mm(c_all, ada_w, (i,), tm=ada_rows, tn=512, order="nm", epi="bias", x_silu=True,
                  bias=ada_b[i].reshape(1, -1))
        tabs = [mod[:n_batch + 1, k * d:(k + 1) * d].reshape(n_batch + 1, 1, d) for k in range(6)]
        shift1, scale1, gate1, shift2, scale2, gate2 = tabs

        h = _modnorm(xs, norm1_g[i], shift1, scale1, tab)
        p = _mm(h, w_in, (i,), tm=1024, tn=512, order="nm")
        ones = jnp.ones((HEAD_DIM,), F32)
        kscale = HEAD_DIM ** -0.5
        gain = jnp.stack([attn_q_norm[i]] * 8 + [attn_k_norm[i]] * 2 + [ones] * 20 + [ones * kscale] * 4)
        nflag = jnp.concatenate([jnp.ones((10, HEAD_DIM), F32), jnp.zeros((24, HEAD_DIM), F32)])
        qk = _prep(p, cos_t, sin_t, gain.reshape(-1, 1, HEAD_DIM), nflag.reshape(-1, 1, HEAD_DIM))

        att = functools.partial(_attention, qk=qk, p=p, n_batch=n_batch, t_lat=t_lat, t_ctx=t_ctx, tq=512)
        gqa = functools.partial(att, functools.partial(_gqa_kernel, scale=kscale), heads=2,
                                q_w=4 * HEAD_DIM, q_blk0=0, k_w=HEAD_DIM, k_blk0=8, v_w=HEAD_DIM, v_blk0=10,
                                out_w=4 * HEAD_DIM)
        lp = diff_lambda[i].astype(F32)
        lam = (jnp.exp(jnp.sum(lp[0] * lp[1])) - jnp.exp(jnp.sum(lp[2] * lp[3])) + lam_init).reshape(1)
        dgain = diff_norm[i].reshape(1, 2 * HEAD_DIM)
        dif = functools.partial(att, functools.partial(_diff_kernel, scale=kscale, out_scale=1.0 - lam_init),
                                heads=4, q_w=2 * HEAD_DIM, q_blk0=5, k_w=2 * HEAD_DIM, k_blk0=9,
                                v_w=2 * HEAD_DIM, v_blk0=18, out_w=2 * HEAD_DIM,
                                lead_in=(lam,), lead_specs=(pl.BlockSpec(memory_space=pltpu.SMEM),),
                                extra_in=(dgain,),
                                extra_specs=(pl.BlockSpec((1, 2 * HEAD_DIM), lambda b, hh, ii: (0, 0)),))
        ya_l, yc_l = gqa(ctx_queries=False), dif(ctx_queries=False)
        log_gamma = jax.nn.log_sigmoid(ret_decay_logit[i].astype(F32))
        yd_l, yd_c = _retention(qk, p, log_gamma, ret_norm[i], n_batch=n_batch, t_lat=t_lat, t_ctx=t_ctx,
                                heads=4, q_blk0=26, k_blk0=30, v_blk0=26, g_blk0=30, dv=2 * HEAD_DIM)
        s5_col0 = 12 * HEAD_DIM
        yb = _s5_branch(p[:, s5_col0:s5_col0 + bw], s5_params, i, n_batch=n_batch, t_lat=t_lat, t_ctx=t_ctx)
        if need_ctx:
            ya = jnp.concatenate([ya_l, gqa(ctx_queries=True)])
            yc = jnp.concatenate([yc_l, dif(ctx_queries=True)])
            yd = jnp.concatenate([yd_l, yd_c])
            rows = n_lat + n_ctx
        else:
            ya, yc, yd, yb = ya_l, yc_l, yd_l, yb[:n_lat]
            rows = n_lat
        ycat = jnp.concatenate([ya, yb, yc, yd], axis=1)
        acc = _merge(h, ycat, w_merge_gate, w_branch, i, m_rows=rows)
        xs = _mm(acc, w_out, (i,), tm=1024, tn=512, order="nm", epi="resgate", res=xs, gate=gate1,
                 tab=tab, m_rows=rows)

        if i % 2 == 0:
            j = i // 2
            h2 = _modnorm(xs, norm2_g[i], shift2, scale2, tab)
            u = _swiglu_up(h2, ffn_w1, ffn_w3, (j,))
            half = u.shape[1] // 2
            for kc in range(2):
                xs = _mm(u, ffn_w2, (j,), tm=512, tn=512, tk=half, k0=kc, order="nm", epi="resgate",
                         res=xs, gate=gate2, tab=tab)
        else:
            j = i // 2
            h2, gates = _modnorm(xs, norm2_g[i], shift2, scale2, tab,
                                 router_w=moe_router_w[j], router_b=moe_router_b[j])
            gate_e = gates[:, :N_EXPERTS].T.reshape(N_EXPERTS, rows, 1)
            u = _swiglu_up(h2, moe_w1, moe_w3, (j,), n_expert=N_EXPERTS, gate=gate_e)
            w2 = moe_w2[j].reshape(-1, d)
            xs = _mm(u, w2, tm=1024, tn=1024, tk=1024, nk=u.shape[1] // 1024, order="mn", epi="resgate",
                     res=xs, gate=gate2, tab=tab)

    return _final_norm(xs[:n_lat], final_norm_g).reshape(n_batch, t_lat, d)
```

```python
import functools
import math
from typing import Callable, NamedTuple

import jax
import jax.numpy as jnp
from jax import lax
from jax.experimental import pallas as pl
from jax.experimental.pallas import tpu as pltpu

F32 = jnp.float32
BF16 = jnp.bfloat16

HEAD_DIM = 128
GRID_W = 64
ROPE_BASE = 10000.0
ROPE_FREQS = HEAD_DIM // 4
EPS = 1e-6
A_GROUP = 4
S5_P = 16
S5_N = 64
S5_L = 8
LANES = 128
RET_CHUNK = 128
N_EXPERTS = 8
ROUTER_LANES = 128
V7X_VMEM_BYTES = 64 * 1024 * 1024
VMEM_HEADROOM_BYTES = 8 * 1024 * 1024


def _vmem_limit(nbytes):
    return int(min(nbytes + VMEM_HEADROOM_BYTES, V7X_VMEM_BYTES - VMEM_HEADROOM_BYTES))


def _nbytes(shape, dtype):
    return math.prod(shape) * jnp.dtype(dtype).itemsize


class RowTable(NamedTuple):
    index_of_row: Callable
    group_rows: int


def _tile(total, preferred):
    return math.gcd(total, preferred)


def _mm_kernel(*refs, x_silu, epi):
    it = iter(refs)
    x_ref, w_ref = next(it), next(it)
    bias_ref = next(it) if epi == "bias" else None
    res_ref, gate_ref = (next(it), next(it)) if epi == "resgate" else (None, None)
    o_ref = next(it)

    x = x_ref[...]
    if x_silu:
        x = x * jax.nn.sigmoid(x)
    acc = jnp.dot(x.astype(BF16), w_ref[...].astype(BF16), preferred_element_type=F32)
    if epi == "bias":
        acc = acc + bias_ref[...]
    elif epi == "resgate":
        acc = res_ref[...] + gate_ref[...] * acc
    o_ref[...] = acc.astype(o_ref.dtype)


def _mm(x, w, wlead=(), *, name, m_rows=None, tm, tn, tk=None, k0=0, epi="none",
        out_dtype=F32, x_silu=False, bias=None, res=None, gate=None, tab=None):
    m_total = x.shape[0] if m_rows is None else m_rows
    n_total = w.shape[-1]
    tk = x.shape[1] if tk is None else tk
    tm, tn = _tile(m_total, tm), _tile(n_total, tn)
    if tab is not None:
        tm = _tile(tab.group_rows, tm)
    nlead = len(wlead)
    in_specs = [
        pl.BlockSpec((tm, tk), lambda n, m: (m, k0)),
        pl.BlockSpec((None,) * nlead + (tk, tn), lambda n, m: tuple(wlead) + (k0, n)),
    ]
    args = [x, w]
    est = 2 * _nbytes((tm, tk), x.dtype) + 2 * _nbytes((tk, tn), w.dtype) + _nbytes((tk, tn), BF16)
    if epi == "bias":
        in_specs.append(pl.BlockSpec((1, tn), lambda n, m: (0, n)))
        args.append(bias)
    elif epi == "resgate":
        in_specs.append(pl.BlockSpec((tm, tn), lambda n, m: (m, n)))
        in_specs.append(pl.BlockSpec((None, 1, tn), lambda n, m: (tab.index_of_row(m * tm), 0, n)))
        args += [res, gate]
        est += 2 * _nbytes((tm, tn), F32)
    est += 2 * _nbytes((tm, tn), out_dtype) + 2 * _nbytes((tm, tn), F32)
    return pl.pallas_call(
        functools.partial(_mm_kernel, x_silu=x_silu, epi=epi),
        grid=(n_total // tn, m_total // tm),
        in_specs=in_specs,
        out_specs=pl.BlockSpec((tm, tn), lambda n, m: (m, n)),
        out_shape=jax.ShapeDtypeStruct((m_total, n_total), out_dtype),
        compiler_params=pltpu.CompilerParams(
            dimension_semantics=("parallel", "arbitrary"),
            vmem_limit_bytes=_vmem_limit(est)),
        name=name,
    )(*args)


def _rms(x):
    return x * lax.rsqrt(jnp.mean(x * x, axis=-1, keepdims=True) + EPS)


def _modnorm_kernel(x_ref, g_ref, sh_ref, sc_ref, *rest, router):
    h = _rms(x_ref[...]) * g_ref[...] * (1.0 + sc_ref[...]) + sh_ref[...]
    if not router:
        (o_ref,) = rest
        o_ref[...] = h.astype(o_ref.dtype)
        return
    rw_ref, rb_ref, o_ref, gate_ref = rest
    o_ref[...] = h.astype(o_ref.dtype)
    logits = jnp.dot(h, rw_ref[...], preferred_element_type=F32,
                     precision=lax.Precision.HIGHEST) + rb_ref[...]
    lane = lax.broadcasted_iota(jnp.int32, logits.shape, 1).astype(F32)
    neg = jnp.float32(-jnp.inf)
    logits = jnp.where(lane < N_EXPERTS, logits, neg)
    v1 = jnp.max(logits, axis=-1, keepdims=True)
    i1 = jnp.min(jnp.where(logits == v1, lane, float(ROUTER_LANES)), axis=-1, keepdims=True)
    rest_l = jnp.where(lane == i1, neg, logits)
    v2 = jnp.max(rest_l, axis=-1, keepdims=True)
    i2 = jnp.min(jnp.where(rest_l == v2, lane, float(ROUTER_LANES)), axis=-1, keepdims=True)
    e2 = jnp.exp(v2 - v1)
    w1 = 1.0 / (1.0 + e2)
    w2 = e2 * w1
    routed = jnp.where(lane == i1, w1, jnp.where(lane == i2, w2, 0.0))
    for off, val in enumerate((i1, i2, w1, w2)):
        routed = jnp.where(lane == float(N_EXPERTS + off), val, routed)
    gate_ref[...] = routed


def _modnorm(x, g, shift_tab, scale_tab, tab, *, tm=256, router_w=None, router_b=None):
    m_total, d = x.shape
    tm = _tile(tab.group_rows, _tile(m_total, tm))
    row = pl.BlockSpec((tm, d), lambda m: (m, 0))
    vec = pl.BlockSpec((1, d), lambda m: (0, 0))
    per_group = pl.BlockSpec((None, 1, d), lambda m: (tab.index_of_row(m * tm), 0, 0))
    in_specs, args = [row, vec, per_group, per_group], [x, g.reshape(1, d), shift_tab, scale_tab]
    router = router_w is not None
    out_specs, out_shape = row, jax.ShapeDtypeStruct((m_total, d), F32 if router else BF16)
    est = 2 * _nbytes((tm, d), F32) * 3
    if router:
        rw = jnp.zeros((d, ROUTER_LANES), F32).at[:, :N_EXPERTS].set(router_w)
        rb = jnp.zeros((1, ROUTER_LANES), F32).at[0, :N_EXPERTS].set(router_b)
        in_specs += [pl.BlockSpec((d, ROUTER_LANES), lambda m: (0, 0)),
                     pl.BlockSpec((1, ROUTER_LANES), lambda m: (0, 0))]
        args += [rw, rb]
        out_specs = (row, pl.BlockSpec((tm, ROUTER_LANES), lambda m: (m, 0)))
        out_shape = (out_shape, jax.ShapeDtypeStruct((m_total, ROUTER_LANES), F32))
        est += 2 * _nbytes((d, ROUTER_LANES), F32) * 4
    return pl.pallas_call(
        functools.partial(_modnorm_kernel, router=router),
        grid=(m_total // tm,), in_specs=in_specs, out_specs=out_specs, out_shape=out_shape,
        compiler_params=pltpu.CompilerParams(dimension_semantics=("parallel",),
                                             vmem_limit_bytes=_vmem_limit(est)),
        name="modnorm_router" if router else "modnorm",
    )(*args)


def _final_norm_kernel(x_ref, g_ref, o_ref):
    o_ref[...] = _rms(x_ref[...]) * g_ref[...]


def _final_norm(x, g, *, tm=256):
    m_total, d = x.shape
    tm = _tile(m_total, tm)
    row = pl.BlockSpec((tm, d), lambda m: (m, 0))
    return pl.pallas_call(
        _final_norm_kernel, grid=(m_total // tm,),
        in_specs=[row, pl.BlockSpec((1, d), lambda m: (0, 0))], out_specs=row,
        out_shape=jax.ShapeDtypeStruct((m_total, d), F32),
        compiler_params=pltpu.CompilerParams(dimension_semantics=("parallel",),
                                             vmem_limit_bytes=_vmem_limit(6 * _nbytes((tm, d), F32))),
        name="final_norm",
    )(x, g.reshape(1, d))


def _prep_kernel(x_ref, cos_ref, sin_ref, gain_ref, nflag_ref, o_ref):
    x = x_ref[...]
    inv = lax.rsqrt(jnp.mean(x * x, axis=-1, keepdims=True) + EPS)
    nf = nflag_ref[...]
    y = x * (nf * inv + (1.0 - nf)) * gain_ref[...]
    lane = lax.broadcasted_iota(jnp.int32, y.shape, 1)
    first_half = (lane % (2 * ROPE_FREQS)) < ROPE_FREQS
    partner = jnp.where(first_half, pltpu.roll(y, HEAD_DIM - ROPE_FREQS, 1), pltpu.roll(y, ROPE_FREQS, 1))
    o_ref[...] = (y * cos_ref[...] + partner * sin_ref[...]).astype(o_ref.dtype)


def _prep_src_block(j):
    return jnp.where(j < 10, j, jnp.where(j < 26, j + 10, j + 18))


N_PREP_BLOCKS = 34


def _prep(p, cos_t, sin_t, gain, nflag, *, tm=512):
    m_total = p.shape[0]
    tm = _tile(m_total, tm)
    return pl.pallas_call(
        _prep_kernel, grid=(m_total // tm, N_PREP_BLOCKS),
        in_specs=[pl.BlockSpec((tm, HEAD_DIM), lambda m, j: (m, _prep_src_block(j))),
                  pl.BlockSpec((tm, HEAD_DIM), lambda m, j: (m, 0)),
                  pl.BlockSpec((tm, HEAD_DIM), lambda m, j: (m, 0)),
                  pl.BlockSpec((None, 1, HEAD_DIM), lambda m, j: (j, 0, 0)),
                  pl.BlockSpec((None, 1, HEAD_DIM), lambda m, j: (j, 0, 0))],
        out_specs=pl.BlockSpec((tm, HEAD_DIM), lambda m, j: (m, j)),
        out_shape=jax.ShapeDtypeStruct((m_total, N_PREP_BLOCKS * HEAD_DIM), BF16),
        compiler_params=pltpu.CompilerParams(dimension_semantics=("parallel", "arbitrary")),
        name="qk_prep",
    )(p, cos_t, sin_t, gain, nflag)


def _rope_tables(t_lat, n_batch, t_ctx):
    rows = t_lat // GRID_W
    row = jnp.broadcast_to(jnp.arange(rows)[:, None], (rows, GRID_W)).reshape(-1)
    col = jnp.broadcast_to(jnp.arange(GRID_W)[None, :], (rows, GRID_W)).reshape(-1)
    inv = ROPE_BASE ** (-jnp.arange(ROPE_FREQS, dtype=F32) / ROPE_FREQS)
    ang = jnp.stack([row, col], axis=-1).astype(F32)[:, :, None] * inv
    cos, sin = jnp.cos(ang), jnp.sin(ang)
    cos128 = jnp.concatenate([cos, cos], axis=-1).reshape(t_lat, HEAD_DIM)
    sin128 = jnp.concatenate([-sin, sin], axis=-1).reshape(t_lat, HEAD_DIM)
    cos_t = jnp.concatenate([jnp.tile(cos128, (n_batch, 1)), jnp.ones((n_batch * t_ctx, HEAD_DIM), F32)])
    sin_t = jnp.concatenate([jnp.tile(sin128, (n_batch, 1)), jnp.zeros((n_batch * t_ctx, HEAD_DIM), F32)])
    return cos_t, sin_t


_NT = (((1,), (1,)), ((), ()))


def _softmax_parts(q, ks, scale):
    ss = [lax.dot_general(q, k, _NT, preferred_element_type=F32) * scale for k in ks]
    m = functools.reduce(jnp.maximum, [jnp.max(s, axis=-1, keepdims=True) for s in ss])
    es = [jnp.exp(s - m) for s in ss]
    l = functools.reduce(jnp.add, [jnp.sum(e, axis=-1, keepdims=True) for e in es])
    return es, l


def _gqa_kernel(q_ref, *rest, nseg, scale):
    k_refs, v_refs, o_ref = rest[:nseg], rest[nseg:2 * nseg], rest[2 * nseg]
    ks = [r[...] for r in k_refs]
    vs = [r[...].astype(BF16) for r in v_refs]
    for g in range(A_GROUP):
        cols = slice(g * HEAD_DIM, (g + 1) * HEAD_DIM)
        es, l = _softmax_parts(q_ref[:, cols], ks, scale)
        o = functools.reduce(jnp.add, [jnp.dot(e.astype(BF16), v, preferred_element_type=F32)
                                       for e, v in zip(es, vs)])
        o_ref[:, cols] = (o / l).astype(o_ref.dtype)


def _diff_kernel(lam_ref, q_ref, *rest, nseg, scale, out_scale):
    k_refs, v_refs = rest[:nseg], rest[nseg:2 * nseg]
    gain_ref, o_ref = rest[2 * nseg], rest[2 * nseg + 1]
    lam = lam_ref[0]
    vs = [r[...].astype(BF16) for r in v_refs]
    parts = []
    for m in range(2):
        cols = slice(m * HEAD_DIM, (m + 1) * HEAD_DIM)
        es, l = _softmax_parts(q_ref[:, cols], [r[:, cols] for r in k_refs], scale)
        parts.append((es, 1.0 / l))
    (e0, r0), (e1, r1) = parts
    o = functools.reduce(jnp.add, [
        jnp.dot((a * r0 - lam * (b * r1)).astype(BF16), v, preferred_element_type=F32)
        for a, b, v in zip(e0, e1, vs)])
    o_ref[...] = (_rms(o) * gain_ref[...] * out_scale).astype(o_ref.dtype)


def _attention(kernel, qk, p, *, name, n_batch, t_lat, t_ctx, heads, q_w, q_blk0, k_w, k_blk0, v_w, v_blk0,
               out_w, ctx_queries, tq, extra_in=(), extra_specs=(), lead_in=(), lead_specs=()):
    n_lat = n_batch * t_lat
    ctx_row0 = n_lat // t_ctx
    tq = _tile(t_lat, tq)
    if ctx_queries:
        nq, q_rows = 1, t_ctx
        q_map = lambda b, h, i: (ctx_row0 + b, q_blk0 + h)
        out_rows, o_map = n_batch * t_ctx, (lambda b, h, i: (b, h))
        segs = [(t_ctx, lambda b: ctx_row0 + b)]
    else:
        nq, q_rows = t_lat // tq, tq
        q_map = lambda b, h, i: (b * nq + i, q_blk0 + h)
        out_rows, o_map = n_lat, (lambda b, h, i: (b * nq + i, h))
        segs = [(t_ctx, lambda b: ctx_row0 + b), (t_lat, lambda b: b)]
    k_specs = [pl.BlockSpec((rows, k_w), functools.partial(lambda rf, b, h, i: (rf(b), k_blk0 + h), rf))
               for rows, rf in segs]
    v_specs = [pl.BlockSpec((rows, v_w), functools.partial(lambda rf, b, h, i: (rf(b), v_blk0 + h), rf))
               for rows, rf in segs]
    nseg = len(segs)
    est = 8 * _nbytes((q_rows, t_lat + t_ctx), F32) + 4 * _nbytes((t_lat + t_ctx, v_w), F32)
    return pl.pallas_call(
        functools.partial(kernel, nseg=nseg),
        grid=(n_batch, heads, nq),
        in_specs=list(lead_specs) + [pl.BlockSpec((q_rows, q_w), q_map)] + k_specs + v_specs + list(extra_specs),
        out_specs=pl.BlockSpec((q_rows, out_w), o_map),
        out_shape=jax.ShapeDtypeStruct((out_rows, heads * out_w), BF16),
        compiler_params=pltpu.CompilerParams(dimension_semantics=("parallel", "parallel", "arbitrary"),
                                             vmem_limit_bytes=_vmem_limit(est)),
        name=name + ("_ctx" if ctx_queries else "_lat"),
    )(*lead_in, qk, *([qk] * nseg), *([p] * nseg), *extra_in)


def _ret_kernel(lg_ref, ql_ref, qc_ref, kl_ref, kc_ref, vl_ref, vc_ref, gl_ref, gc_ref, gain_ref,
                yl_ref, yc_ref, ol_scr, oc_scr, s_scr, *, t_lat, t_ctx):
    h = pl.program_id(1)
    lg_f, lg_b = lg_ref[0, h], lg_ref[1, h]
    c = RET_CHUNK
    pos_i = lax.broadcasted_iota(jnp.int32, (c, c), 0).astype(F32)
    pos_j = lax.broadcasted_iota(jnp.int32, (c, c), 1).astype(F32)
    rel = pos_i - pos_j
    d_both = (jnp.where(rel >= 0, jnp.exp(jnp.maximum(rel, 0.0) * lg_f), 0.0)
              + jnp.where(rel <= 0, jnp.exp(jnp.maximum(-rel, 0.0) * lg_b), 0.0))
    pos = lax.broadcasted_iota(jnp.int32, (c, 1), 0).astype(F32)
    qdec = (jnp.exp((pos + 1.0) * lg_f), jnp.exp((c - pos) * lg_b))
    kdec = (jnp.exp((c - 1.0 - pos) * lg_f), jnp.exp(pos * lg_b))
    one = jnp.ones((1, 1), F32)
    sdec = (jnp.exp(one * (c * lg_f)), jnp.exp(one * (c * lg_b)))

    chunks = [(qc_ref, kc_ref, vc_ref, oc_scr, i) for i in range(t_ctx // c)]
    chunks += [(ql_ref, kl_ref, vl_ref, ol_scr, i) for i in range(t_lat // c)]
    back = ([ch for ch in chunks if ch[0] is qc_ref][::-1] + [ch for ch in chunks if ch[0] is ql_ref][::-1])

    for d, order in ((0, chunks), (1, back)):
        s_scr[...] = jnp.zeros_like(s_scr)
        for q_ref, k_ref, v_ref, o_scr, i in order:
            rows = pl.ds(i * c, c)
            q, k = q_ref[rows, :], k_ref[rows, :]
            v = v_ref[rows, :].astype(BF16)
            state = s_scr[...]
            cross = jnp.dot((q.astype(F32) * qdec[d]).astype(BF16), state.astype(BF16),
                            preferred_element_type=F32)
            if d == 0:
                scores = lax.dot_general(q, k, _NT, preferred_element_type=F32) * d_both
                o_scr[rows, :] = cross + jnp.dot(scores.astype(BF16), v, preferred_element_type=F32)
            else:
                o_scr[rows, :] += cross
            kv = jnp.dot((k.astype(F32) * kdec[d]).T.astype(BF16), v, preferred_element_type=F32)
            s_scr[...] = sdec[d] * state + kv

    gain = gain_ref[...]
    for o_scr, g_ref, y_ref in ((ol_scr, gl_ref, yl_ref), (oc_scr, gc_ref, yc_ref)):
        g = g_ref[...]
        y_ref[...] = (_rms(o_scr[...]) * gain * (g * jax.nn.sigmoid(g))).astype(y_ref.dtype)


def _retention(qk, p, log_gamma, gain, *, n_batch, t_lat, t_ctx, heads, q_blk0, k_blk0, v_blk0, g_blk0, dv):
    n_lat = n_batch * t_lat
    ctx_row0 = n_lat // t_ctx
    lat = lambda w, blk0: pl.BlockSpec((t_lat, w), lambda b, h: (b, blk0 + h))
    ctx = lambda w, blk0: pl.BlockSpec((t_ctx, w), lambda b, h: (ctx_row0 + b, blk0 + h))
    est = 6 * _nbytes((t_lat + t_ctx, dv), F32) * 2
    return pl.pallas_call(
        functools.partial(_ret_kernel, t_lat=t_lat, t_ctx=t_ctx),
        grid=(n_batch, heads),
        in_specs=[pl.BlockSpec(memory_space=pltpu.SMEM),
                  lat(HEAD_DIM, q_blk0), ctx(HEAD_DIM, q_blk0), lat(HEAD_DIM, k_blk0), ctx(HEAD_DIM, k_blk0),
                  lat(dv, v_blk0), ctx(dv, v_blk0), lat(dv, g_blk0), ctx(dv, g_blk0),
                  pl.BlockSpec((1, dv), lambda b, h: (0, 0))],
        out_specs=(pl.BlockSpec((t_lat, dv), lambda b, h: (b, h)),
                   pl.BlockSpec((t_ctx, dv), lambda b, h: (b, h))),
        out_shape=(jax.ShapeDtypeStruct((n_lat, heads * dv), BF16),
                   jax.ShapeDtypeStruct((n_batch * t_ctx, heads * dv), BF16)),
        scratch_shapes=[pltpu.VMEM((t_lat, dv), F32), pltpu.VMEM((t_ctx, dv), F32),
                        pltpu.VMEM((HEAD_DIM, dv), F32)],
        compiler_params=pltpu.CompilerParams(dimension_semantics=("parallel", "arbitrary"),
                                             vmem_limit_bytes=_vmem_limit(est)),
        name="retention",
    )(log_gamma, qk, qk, qk, qk, p, p, p, p, gain.reshape(1, dv))


def _s5_tables(a_re, a_im, log_step, b_re, b_im, c_re, c_im):
    hp = lax.Precision.HIGHEST
    n_dir, n_grp, n_st = a_re.shape
    ln = S5_L
    tg = LANES // S5_P
    n_tile = n_grp // tg
    lam = lax.complex(a_re.astype(F32), a_im.astype(F32))
    step = jnp.exp(log_step.astype(F32))[..., None]
    a_bar = jnp.exp(lam * step)
    b_bar = ((a_bar - 1.0) / lam)[..., None] * lax.complex(b_re.astype(F32), b_im.astype(F32))
    c_mat = lax.complex(c_re.astype(F32), c_im.astype(F32))
    taus = jnp.arange(ln + 1, dtype=F32)
    apow = jnp.exp((lam * step)[..., None] * taus)

    kern = jnp.einsum("dgpn,dgnt,dgnq->dgtpq", c_mat, apow[..., :ln], b_bar, precision=hp).real
    ti = jnp.arange(ln)[:, None]
    tj = jnp.arange(ln)[None, :]
    t_f = kern[0][:, jnp.clip(ti - tj, 0)] * (ti >= tj)[None, :, :, None, None]
    t_b = kern[1][:, jnp.clip(tj - ti, 0)] * (tj >= ti)[None, :, :, None, None]
    toep = (t_f + t_b).reshape(n_tile, tg, ln, ln, S5_P, S5_P)
    eye = jnp.eye(tg, dtype=F32)
    m_intra = jnp.einsum("tgijpq,gh->tjgqihp", toep, eye).reshape(n_tile, ln, LANES, ln * LANES)

    def in_mat(d, tau_of_j):
        w = apow[d][:, :, tau_of_j][..., None] * b_bar[d][:, :, None, :]
        wri = jnp.stack([w.real, w.imag]).reshape(2, n_tile, tg, n_st, ln, S5_P)
        return jnp.einsum("ctgnjq,gh->tjgqchn", wri, eye).reshape(n_tile, ln, LANES, 2 * tg * n_st)

    def out_mat(d, tau_of_i):
        w = c_mat[d][:, :, :, None] * apow[d][:, None, :, tau_of_i]
        wri = jnp.stack([w.real, -w.imag]).reshape(2, n_tile, tg, S5_P, n_st, ln)
        return jnp.einsum("ctgpni,gh->tcgnihp", wri, eye).reshape(n_tile, 2 * tg * n_st, ln * LANES)

    idx = jnp.arange(ln)
    mats = [m_intra, in_mat(0, ln - 1 - idx), in_mat(1, idx), out_mat(0, idx + 1), out_mat(1, ln - idx)]
    dec = [apow[0][..., ln].real, apow[0][..., ln].imag, apow[1][..., ln].real, apow[1][..., ln].imag]
    return [m.astype(BF16) for m in mats], [x.reshape(n_tile, 1, tg * n_st) for x in dec]


def _s5_kernel(u_ref, m_ref, bcf_ref, bcb_ref, ccf_ref, ccb_ref, dfr_ref, dfi_ref, dbr_ref, dbi_ref,
               y_ref, vf, vb, *, n_batch, t_lat, t_ctx):
    ln = S5_L
    n_lat = n_batch * t_lat
    n_cr, n_lr = t_ctx // ln, t_lat // ln
    n_st = vf.shape[0]
    n_re = n_st // 2

    def chunk_tokens(b, j):
        ctx_rows = u_ref[pl.ds(n_lat + b * t_ctx + j, n_cr, stride=ln), :]
        lat_rows = u_ref[pl.ds(b * t_lat + j, n_lr, stride=ln), :]
        return jnp.concatenate([ctx_rows, lat_rows], axis=0).astype(BF16)

    def mix(us, w_ref):
        return functools.reduce(jnp.add, [jnp.dot(us[j], w_ref[j], preferred_element_type=F32)
                                          for j in range(ln)])

    def batch_rows(b):
        return pl.ds(b, n_cr + n_lr, stride=n_batch)

    for b in range(n_batch):
        us = [chunk_tokens(b, j) for j in range(ln)]
        for v, w_ref in ((vf, bcf_ref), (vb, bcb_ref)):
            contrib = mix(us, w_ref)
            for c in range(n_st):
                v[c, batch_rows(b), :] = contrib[:, c * LANES:(c + 1) * LANES]

    tile_rows = 8
    tile_chunks = tile_rows // n_batch
    n_ct, n_tiles = n_cr // tile_chunks, (n_cr + n_lr) // tile_chunks

    def scan(v, d_re, d_im, tile_of_step, backward):
        order = range(tile_chunks - 1, -1, -1) if backward else range(tile_chunks)

        def step(s, carry):
            rows = pl.ds(pl.multiple_of(tile_of_step(s) * tile_rows, tile_rows), tile_rows)
            new_carry = []
            for c in range(n_re):
                s_re, s_im = carry[2 * c], carry[2 * c + 1]
                t_re, t_im = v[c, rows, :], v[n_re + c, rows, :]
                dr, di = d_re[:, c * LANES:(c + 1) * LANES], d_im[:, c * LANES:(c + 1) * LANES]
                o_re, o_im = [None] * tile_chunks, [None] * tile_chunks
                for k in order:
                    sub = slice(k * n_batch, (k + 1) * n_batch)
                    o_re[k], o_im[k] = s_re, s_im
                    s_re, s_im = dr * s_re - di * s_im + t_re[sub], dr * s_im + di * s_re + t_im[sub]
                v[c, rows, :] = jnp.concatenate(o_re, axis=0)
                v[n_re + c, rows, :] = jnp.concatenate(o_im, axis=0)
                new_carry += [s_re, s_im]
            return tuple(new_carry)

        zero = jnp.zeros((n_batch, LANES), F32)
        lax.fori_loop(0, n_tiles, step, (zero,) * n_st)

    scan(vf, dfr_ref[...], dfi_ref[...], lambda s: s, False)
    scan(vb, dbr_ref[...], dbi_ref[...],
         lambda s: jnp.where(s < n_ct, n_ct - 1 - s, n_tiles + n_ct - 1 - s), True)

    def entering_state(v, b):
        return jnp.concatenate([v[c, batch_rows(b), :] for c in range(n_st)], axis=1).astype(BF16)

    for b in range(n_batch):
        us = [chunk_tokens(b, j) for j in range(ln)]
        y = (mix(us, m_ref)
             + jnp.dot(entering_state(vf, b), ccf_ref[...], preferred_element_type=F32)
             + jnp.dot(entering_state(vb, b), ccb_ref[...], preferred_element_type=F32))
        for i in range(ln):
            lanes = slice(i * LANES, (i + 1) * LANES)
            y_ref[pl.ds(n_lat + b * t_ctx + i, n_cr, stride=ln), :] = y[:n_cr, lanes]
            y_ref[pl.ds(b * t_lat + i, n_lr, stride=ln), :] = y[n_cr:, lanes]


def _s5_core(p, col_blk0, mats, decs, *, n_batch, t_lat, t_ctx):
    rows = p.shape[0]
    n_tile = mats[0].shape[0]
    n_chunk = (t_lat + t_ctx) // S5_L
    assert 8 % n_batch == 0 and (t_ctx // S5_L) % (8 // n_batch) == 0 and (t_lat // S5_L) % (8 // n_batch) == 0
    state = mats[1].shape[-1]
    mat_specs = [pl.BlockSpec((None,) + m.shape[1:], lambda t, nd=m.ndim: (t,) + (0,) * (nd - 1)) for m in mats]
    dec_specs = [pl.BlockSpec((None, 1, state // 2), lambda t: (t, 0, 0)) for _ in decs]
    est = (4 * _nbytes((rows, LANES), F32) + 2 * sum(_nbytes(m.shape[1:], BF16) for m in mats)
           + 2 * _nbytes((n_chunk * n_batch, state), F32) + 6 * _nbytes((n_chunk, state), F32))
    return pl.pallas_call(
        functools.partial(_s5_kernel, n_batch=n_batch, t_lat=t_lat, t_ctx=t_ctx),
        grid=(n_tile,),
        in_specs=[pl.BlockSpec((rows, LANES), lambda t: (0, col_blk0 + t))] + mat_specs + dec_specs,
        out_specs=pl.BlockSpec((rows, LANES), lambda t: (0, t)),
        out_shape=jax.ShapeDtypeStruct((rows, n_tile * LANES), F32),
        scratch_shapes=[pltpu.VMEM((state // LANES, n_chunk * n_batch, LANES), F32) for _ in range(2)],
        compiler_params=pltpu.CompilerParams(dimension_semantics=("arbitrary",),
                                             vmem_limit_bytes=_vmem_limit(est)),
        name="s5_core",
    )(p, *mats, *decs)


def _s5_finish_kernel(y_ref, ua_ref, ub_ref, d_ref, w_ref, o_ref):
    u = jnp.concatenate([ua_ref[...], ub_ref[...]], axis=1)
    z = jax.nn.gelu(y_ref[...] + d_ref[...] * u)
    gl = jnp.dot(z.astype(BF16), w_ref[...].astype(BF16), preferred_element_type=F32)
    o_ref[...] = (z * jax.nn.sigmoid(gl)).astype(o_ref.dtype)


def _s5_finish(y, p, col_half0, d_skip, w_glu, wlead, *, tm=512):
    m_total, w = y.shape
    tm = _tile(m_total, tm)
    row = pl.BlockSpec((tm, w), lambda m: (m, 0))
    nlead = len(wlead)
    return pl.pallas_call(
        _s5_finish_kernel, grid=(m_total // tm,),
        in_specs=[row, pl.BlockSpec((tm, w // 2), lambda m: (m, col_half0)),
                  pl.BlockSpec((tm, w // 2), lambda m: (m, col_half0 + 1)),
                  pl.BlockSpec((1, w), lambda m: (0, 0)),
                  pl.BlockSpec((None,) * nlead + (w, w), lambda m: tuple(wlead) + (0, 0))],
        out_specs=row, out_shape=jax.ShapeDtypeStruct((m_total, w), BF16),
        compiler_params=pltpu.CompilerParams(
            dimension_semantics=("parallel",),
            vmem_limit_bytes=_vmem_limit(8 * _nbytes((tm, w), F32) + 3 * _nbytes((w, w), F32))),
        name="s5_finish",
    )(y, p, p, d_skip.reshape(1, w), w_glu)


def _s5_branch(p, col0, bw, params, layer, *, n_batch, t_lat, t_ctx):
    mats, decs = _s5_tables(*(params[k][layer] for k in
                              ("s5_a_re", "s5_a_im", "s5_log_step", "s5_b_re", "s5_b_im", "s5_c_re", "s5_c_im")))
    y = _s5_core(p, col0 // LANES, mats, decs, n_batch=n_batch, t_lat=t_lat, t_ctx=t_ctx)
    return _s5_finish(y, p, col0 // (bw // 2), params["s5_d"][layer], params["s5_w_glu"], (layer,))


def _merge_kernel(h_ref, y_ref, wg_ref, wb_ref, o_ref, acc_ref, *, n_branch):
    j = pl.program_id(2)
    gate = jnp.dot(h_ref[...], wg_ref[...].astype(BF16), preferred_element_type=F32)
    val = jnp.dot(y_ref[...], wb_ref[...].astype(BF16), preferred_element_type=F32)
    term = jax.nn.sigmoid(gate) * val

    @pl.when(j == 0)
    def _():
        acc_ref[...] = term

    @pl.when(j > 0)
    def _():
        acc_ref[...] += term

    @pl.when(j == n_branch - 1)
    def _():
        o_ref[...] = acc_ref[...].astype(o_ref.dtype)


def _merge(h, ycat, w_gate, w_branch, layer, *, m_rows, tm=1024, tn=256):
    d = h.shape[1]
    tm = _tile(m_rows, tm)
    n_branch, bw = w_branch.shape[1], w_branch.shape[2]
    est = (2 * _nbytes((tm, d), BF16) + 2 * _nbytes((tm, bw), BF16) + 3 * _nbytes((d, tn), F32)
           + 3 * _nbytes((bw, tn), F32) + 5 * _nbytes((tm, tn), F32))
    return pl.pallas_call(
        functools.partial(_merge_kernel, n_branch=n_branch),
        grid=(m_rows // tm, d // tn, n_branch),
        in_specs=[pl.BlockSpec((tm, d), lambda m, n, j: (m, 0)),
                  pl.BlockSpec((tm, bw), lambda m, n, j: (m, j)),
                  pl.BlockSpec((None, None, d, tn), lambda m, n, j: (layer, j, 0, n)),
                  pl.BlockSpec((None, None, bw, tn), lambda m, n, j: (layer, j, 0, n))],
        out_specs=pl.BlockSpec((tm, tn), lambda m, n, j: (m, n)),
        out_shape=jax.ShapeDtypeStruct((m_rows, d), BF16),
        scratch_shapes=[pltpu.VMEM((tm, tn), F32)],
        compiler_params=pltpu.CompilerParams(dimension_semantics=("parallel", "parallel", "arbitrary"),
                                             vmem_limit_bytes=_vmem_limit(est)),
        name="merge",
    )(h, ycat, w_gate, w_branch)


def _swiglu_tile(x, w1_ref, w3_ref):
    a = jnp.dot(x, w1_ref[...].astype(BF16), preferred_element_type=F32)
    b = jnp.dot(x, w3_ref[...].astype(BF16), preferred_element_type=F32)
    return a * jax.nn.sigmoid(a) * b


def _swiglu_up_kernel(x_ref, w1_ref, w3_ref, o_ref):
    o_ref[...] = _swiglu_tile(x_ref[...], w1_ref, w3_ref).astype(o_ref.dtype)


def _swiglu_up(x, w1, w3, wlead, *, tm=1024, tn=256):
    m_total, d = x.shape
    tm = _tile(m_total, tm)
    ff = w1.shape[-1]
    nlead = len(wlead)
    w_spec = pl.BlockSpec((None,) * nlead + (d, tn), lambda n, m: tuple(wlead) + (0, n))
    est = 2 * _nbytes((tm, d), BF16) + 6 * _nbytes((d, tn), F32) + 6 * _nbytes((tm, tn), F32)
    return pl.pallas_call(
        _swiglu_up_kernel,
        grid=(ff // tn, m_total // tm),
        in_specs=[pl.BlockSpec((tm, d), lambda n, m: (m, 0)), w_spec, w_spec],
        out_specs=pl.BlockSpec((tm, tn), lambda n, m: (m, n)),
        out_shape=jax.ShapeDtypeStruct((m_total, ff), BF16),
        compiler_params=pltpu.CompilerParams(dimension_semantics=("parallel", "arbitrary"),
                                             vmem_limit_bytes=_vmem_limit(est)),
        name="ffn_up",
    )(x, w1, w3)


MOE_TOP_K = 2
MOE_ROW_TILE = 512
MOE_GATHER_ROWS = 256
MOE_COMBINE_ROWS = 128


def _moe_routing(gates, n_tok):
    tile = MOE_ROW_TILE
    n_rows = MOE_TOP_K * n_tok + N_EXPERTS * tile
    experts = jnp.concatenate([gates[:, N_EXPERTS + k] for k in range(MOE_TOP_K)]).astype(jnp.int32)
    onehot = (experts[:, None] == jnp.arange(N_EXPERTS, dtype=jnp.int32)[None, :]).astype(jnp.int32)
    before = jnp.cumsum(onehot, axis=0) - onehot
    counts = jnp.sum(onehot, axis=0)
    padded = ((counts + tile - 1) // tile) * tile
    ends = jnp.cumsum(padded)
    dest = jnp.sum(onehot * (before + (ends - padded)[None, :]), axis=1)
    tokens = jnp.tile(jnp.arange(n_tok, dtype=jnp.int32), MOE_TOP_K)
    src = jnp.zeros((n_rows,), jnp.int32).at[dest].set(tokens)
    tile_start = jnp.arange(n_rows // tile, dtype=jnp.int32) * tile
    tile_expert = jnp.minimum(jnp.sum((tile_start[:, None] >= ends[None, :]).astype(jnp.int32), axis=1),
                              N_EXPERTS - 1)
    return src, dest, tile_expert, (ends[-1:] // tile).astype(jnp.int32)


def _row_copy(src_hbm, row, dst_vmem, slot, sem):
    return pltpu.make_async_copy(src_hbm.at[pl.ds(row, 1)], dst_vmem.at[pl.ds(slot, 1)], sem)


def _moe_gather_kernel(src_ref, nused_ref, x_hbm, o_ref, buf, sem, *, rows, steps_per_tile):
    step = pl.program_id(0)
    used = step < nused_ref[0] * steps_per_tile

    @pl.when(used)
    def _():
        base = step * rows

        def start(r, carry):
            _row_copy(x_hbm, src_ref[base + r], buf, r, sem).start()
            return carry

        def wait(r, carry):
            _row_copy(x_hbm, 0, buf, r, sem).wait()
            return carry

        lax.fori_loop(0, rows, start, 0)
        lax.fori_loop(0, rows, wait, 0)
        o_ref[...] = buf[...].astype(o_ref.dtype)

    @pl.when(jnp.logical_not(used))
    def _():
        o_ref[...] = jnp.zeros_like(o_ref)


def _moe_gather(x, src, n_used):
    n_rows, d = src.shape[0], x.shape[1]
    rows = MOE_GATHER_ROWS
    return pl.pallas_call(
        functools.partial(_moe_gather_kernel, rows=rows, steps_per_tile=MOE_ROW_TILE // rows),
        grid_spec=pltpu.PrefetchScalarGridSpec(
            num_scalar_prefetch=2, grid=(n_rows // rows,),
            in_specs=[pl.BlockSpec(memory_space=pl.ANY)],
            out_specs=pl.BlockSpec((rows, d), lambda i, s, nu: (i, 0)),
            scratch_shapes=[pltpu.VMEM((rows, d), F32), pltpu.SemaphoreType.DMA]),
        out_shape=jax.ShapeDtypeStruct((n_rows, d), BF16),
        compiler_params=pltpu.CompilerParams(
            dimension_semantics=("arbitrary",),
            vmem_limit_bytes=_vmem_limit(_nbytes((rows, d), F32) + 2 * _nbytes((rows, d), BF16))),
        name="moe_gather",
    )(src, n_used, x)


def _moe_up_kernel(te_ref, nused_ref, x_ref, w1_ref, w3_ref, o_ref):
    used = pl.program_id(1) < nused_ref[0]

    @pl.when(used)
    def _():
        o_ref[...] = _swiglu_tile(x_ref[...], w1_ref, w3_ref).astype(o_ref.dtype)

    @pl.when(jnp.logical_not(used))
    def _():
        o_ref[...] = jnp.zeros_like(o_ref)


def _moe_down_kernel(te_ref, nused_ref, u_ref, w2_ref, o_ref):
    used = pl.program_id(1) < nused_ref[0]

    @pl.when(used)
    def _():
        o_ref[...] = jnp.dot(u_ref[...], w2_ref[...].astype(BF16), preferred_element_type=F32)

    @pl.when(jnp.logical_not(used))
    def _():
        o_ref[...] = jnp.zeros_like(o_ref)


def _moe_grouped(kernel, x, ws, layer, tile_expert, n_used, *, tn, out_dtype, name):
    n_rows, kdim = x.shape
    n_out = ws[0].shape[-1]
    tm = MOE_ROW_TILE

    def row_blk(m, nu):
        return jnp.minimum(m, nu[0] - 1)

    w_spec = pl.BlockSpec((None, None, kdim, tn), lambda n, m, te, nu: (layer, te[row_blk(m, nu)], 0, n))
    est = (2 * _nbytes((tm, kdim), BF16) + len(ws) * 3 * _nbytes((kdim, tn), F32)
           + (2 + len(ws)) * _nbytes((tm, tn), F32))
    return pl.pallas_call(
        kernel,
        grid_spec=pltpu.PrefetchScalarGridSpec(
            num_scalar_prefetch=2, grid=(n_out // tn, n_rows // tm),
            in_specs=[pl.BlockSpec((tm, kdim), lambda n, m, te, nu: (row_blk(m, nu), 0))] + [w_spec] * len(ws),
            out_specs=pl.BlockSpec((tm, tn), lambda n, m, te, nu: (m, n))),
        out_shape=jax.ShapeDtypeStruct((n_rows, n_out), out_dtype),
        compiler_params=pltpu.CompilerParams(dimension_semantics=("parallel", "arbitrary"),
                                             vmem_limit_bytes=_vmem_limit(est)),
        name=name,
    )(tile_expert, n_used, x, *ws)


def _moe_combine_kernel(dest_ref, y_hbm, xs_ref, route_ref, gate_ref, o_ref, buf, sem, *, rows, n_tok):
    base = pl.program_id(0) * rows
    for k in range(MOE_TOP_K):
        def start(r, carry, k=k):
            _row_copy(y_hbm, dest_ref[k * n_tok + base + r], buf.at[k], r, sem).start()
            return carry

        lax.fori_loop(0, rows, start, 0)
    for k in range(MOE_TOP_K):
        def wait(r, carry, k=k):
            _row_copy(y_hbm, 0, buf.at[k], r, sem).wait()
            return carry

        lax.fori_loop(0, rows, wait, 0)
    route = route_ref[...]
    w_lane0 = N_EXPERTS + MOE_TOP_K
    mix = functools.reduce(jnp.add, [route[:, w_lane0 + k:w_lane0 + k + 1] * buf[k] for k in range(MOE_TOP_K)])
    o_ref[...] = xs_ref[...] + gate_ref[...] * mix


def _moe_combine(xs, y_rows, dest, route, gate, tab):
    n_tok, d = xs.shape
    rows = _tile(tab.group_rows, MOE_COMBINE_ROWS)
    row = lambda w: pl.BlockSpec((rows, w), lambda i, dst: (i, 0))
    est = (MOE_TOP_K + 6) * _nbytes((rows, d), F32)
    return pl.pallas_call(
        functools.partial(_moe_combine_kernel, rows=rows, n_tok=n_tok),
        grid_spec=pltpu.PrefetchScalarGridSpec(
            num_scalar_prefetch=1, grid=(n_tok // rows,),
            in_specs=[pl.BlockSpec(memory_space=pl.ANY), row(d), row(route.shape[1]),
                      pl.BlockSpec((None, 1, d), lambda i, dst: (tab.index_of_row(i * rows), 0, 0))],
            out_specs=row(d),
            scratch_shapes=[pltpu.VMEM((MOE_TOP_K, rows, d), F32), pltpu.SemaphoreType.DMA]),
        out_shape=jax.ShapeDtypeStruct((n_tok, d), F32),
        compiler_params=pltpu.CompilerParams(dimension_semantics=("arbitrary",),
                                             vmem_limit_bytes=_vmem_limit(est)),
        name="moe_combine",
    )(dest, y_rows, xs, route, gate)


def _moe_sparse(xs, h2, route, w1, w3, w2, layer, gate, tab):
    n_tok = xs.shape[0]
    src, dest, tile_expert, n_used = _moe_routing(route, n_tok)
    xg = _moe_gather(h2, src, n_used)
    ug = _moe_grouped(_moe_up_kernel, xg, (w1, w3), layer, tile_expert, n_used, tn=256, out_dtype=BF16,
                      name="moe_up")
    yg = _moe_grouped(_moe_down_kernel, ug, (w2,), layer, tile_expert, n_used, tn=512, out_dtype=F32,
                      name="moe_down")
    return _moe_combine(xs, yg, dest, route, gate, tab)


def kernel(x, c, ctx, c_ctx, ada_w, ada_b, norm1_g, norm2_g, w_in, attn_q_norm, attn_k_norm, s5_a_re, s5_a_im, s5_log_step, s5_b_re, s5_b_im, s5_c_re, s5_c_im, s5_d, s5_w_glu, diff_lambda, diff_norm, ret_decay_logit, ret_norm, w_branch, w_merge_gate, w_out, ffn_w1, ffn_w3, ffn_w2, moe_router_w, moe_router_b, moe_w1, moe_w3, moe_w2, final_norm_g):
    n_batch, t_lat, d = x.shape
    t_ctx = ctx.shape[1]
    depth = w_in.shape[0]
    n_lat, n_ctx = n_batch * t_lat, n_batch * t_ctx
    bw = d // 4
    s5_params = dict(s5_a_re=s5_a_re, s5_a_im=s5_a_im, s5_log_step=s5_log_step, s5_b_re=s5_b_re,
                     s5_b_im=s5_b_im, s5_c_re=s5_c_re, s5_c_im=s5_c_im, s5_d=s5_d, s5_w_glu=s5_w_glu)

    tab = RowTable(lambda row: jnp.where(row < n_lat, row // t_lat, n_batch), t_lat)

    cos_t, sin_t = _rope_tables(t_lat, n_batch, t_ctx)
    ada_rows = 8
    c_all = jnp.zeros((ada_rows, d), F32).at[:n_batch].set(c).at[n_batch].set(c_ctx)
    xs = jnp.concatenate([x.reshape(n_lat, d), ctx.reshape(n_ctx, d)])

    for i in range(depth):
        need_ctx = i < depth - 1
        lam_init = 0.8 - 0.6 * math.exp(-0.3 * i)
        mod = _mm(c_all, ada_w, (i,), name="ada_mod", tm=ada_rows, tn=512, epi="bias", x_silu=True,
                  bias=ada_b[i].reshape(1, -1))
        tabs = [mod[:n_batch + 1, k * d:(k + 1) * d].reshape(n_batch + 1, 1, d) for k in range(6)]
        shift1, scale1, gate1, shift2, scale2, gate2 = tabs

        h = _modnorm(xs, norm1_g[i], shift1, scale1, tab)
        p = _mm(h, w_in, (i,), name="in_proj", tm=1024, tn=512)
        ones = jnp.ones((HEAD_DIM,), F32)
        kscale = HEAD_DIM ** -0.5
        gain = jnp.stack([attn_q_norm[i]] * 8 + [attn_k_norm[i]] * 2 + [ones] * 20 + [ones * kscale] * 4)
        nflag = jnp.concatenate([jnp.ones((10, HEAD_DIM), F32), jnp.zeros((24, HEAD_DIM), F32)])
        qk = _prep(p, cos_t, sin_t, gain.reshape(-1, 1, HEAD_DIM), nflag.reshape(-1, 1, HEAD_DIM))

        att = functools.partial(_attention, qk=qk, p=p, n_batch=n_batch, t_lat=t_lat, t_ctx=t_ctx, tq=512)
        gqa = functools.partial(att, functools.partial(_gqa_kernel, scale=kscale), name="gqa", heads=2,
                                q_w=4 * HEAD_DIM, q_blk0=0, k_w=HEAD_DIM, k_blk0=8, v_w=HEAD_DIM, v_blk0=10,
                                out_w=4 * HEAD_DIM)
        lp = diff_lambda[i].astype(F32)
        lam = (jnp.exp(jnp.sum(lp[0] * lp[1])) - jnp.exp(jnp.sum(lp[2] * lp[3])) + lam_init).reshape(1)
        dgain = diff_norm[i].reshape(1, 2 * HEAD_DIM)
        dif = functools.partial(att, functools.partial(_diff_kernel, scale=kscale, out_scale=1.0 - lam_init),
                                name="diff",
                                heads=4, q_w=2 * HEAD_DIM, q_blk0=5, k_w=2 * HEAD_DIM, k_blk0=9,
                                v_w=2 * HEAD_DIM, v_blk0=18, out_w=2 * HEAD_DIM,
                                lead_in=(lam,), lead_specs=(pl.BlockSpec(memory_space=pltpu.SMEM),),
                                extra_in=(dgain,),
                                extra_specs=(pl.BlockSpec((1, 2 * HEAD_DIM), lambda b, hh, ii: (0, 0)),))
        ya_l, yc_l = gqa(ctx_queries=False), dif(ctx_queries=False)
        log_gamma = jax.nn.log_sigmoid(ret_decay_logit[i].astype(F32))
        yd_l, yd_c = _retention(qk, p, log_gamma, ret_norm[i], n_batch=n_batch, t_lat=t_lat, t_ctx=t_ctx,
                                heads=4, q_blk0=26, k_blk0=30, v_blk0=26, g_blk0=30, dv=2 * HEAD_DIM)
        s5_col0 = 12 * HEAD_DIM
        yb = _s5_branch(p, s5_col0, bw, s5_params, i, n_batch=n_batch, t_lat=t_lat, t_ctx=t_ctx)
        if need_ctx:
            ya = jnp.concatenate([ya_l, gqa(ctx_queries=True)])
            yc = jnp.concatenate([yc_l, dif(ctx_queries=True)])
            yd = jnp.concatenate([yd_l, yd_c])
            rows = n_lat + n_ctx
        else:
            ya, yc, yd, yb = ya_l, yc_l, yd_l, yb[:n_lat]
            rows = n_lat
        ycat = jnp.concatenate([ya, yb, yc, yd], axis=1)
        acc = _merge(h, ycat, w_merge_gate, w_branch, i, m_rows=rows)
        xs = _mm(acc, w_out, (i,), name="out_proj", tm=1024, tn=512, epi="resgate", res=xs, gate=gate1,
                 tab=tab, m_rows=rows)

        if i % 2 == 0:
            j = i // 2
            h2 = _modnorm(xs, norm2_g[i], shift2, scale2, tab)
            u = _swiglu_up(h2, ffn_w1, ffn_w3, (j,))
            half = u.shape[1] // 2
            for kc in range(2):
                xs = _mm(u, ffn_w2, (j,), name="ffn_down", tm=512, tn=512, tk=half, k0=kc, epi="resgate",
                         res=xs, gate=gate2, tab=tab)
        else:
            j = i // 2
            h2, route = _modnorm(xs, norm2_g[i], shift2, scale2, tab,
                                 router_w=moe_router_w[j], router_b=moe_router_b[j])
            xs = _moe_sparse(xs, h2, route, moe_w1, moe_w3, moe_w2, j, gate2, tab)

    return _final_norm(xs[:n_lat], final_norm_g).reshape(n_batch, t_lat, d)
```

```python
import functools
import math
from typing import Callable, NamedTuple

import jax
import jax.numpy as jnp
from jax import lax
from jax.experimental import pallas as pl
from jax.experimental.pallas import tpu as pltpu

F32 = jnp.float32
BF16 = jnp.bfloat16

HEAD_DIM = 128
GRID_W = 64
ROPE_BASE = 10000.0
ROPE_FREQS = HEAD_DIM // 4
EPS = 1e-6
A_GROUP = 4
S5_P = 16
S5_N = 64
S5_L = 8
LANES = 128
RET_CHUNK = 128
N_EXPERTS = 8
ROUTER_LANES = 128
V7X_VMEM_BYTES = 64 * 1024 * 1024
VMEM_HEADROOM_BYTES = 8 * 1024 * 1024


def _vmem_limit(nbytes):
    return int(min(nbytes + VMEM_HEADROOM_BYTES, V7X_VMEM_BYTES - VMEM_HEADROOM_BYTES))


def _nbytes(shape, dtype):
    return math.prod(shape) * jnp.dtype(dtype).itemsize


class RowTable(NamedTuple):
    index_of_row: Callable
    group_rows: int


def _tile(total, preferred):
    return math.gcd(total, preferred)


def _mm_kernel(*refs, x_silu, epi):
    it = iter(refs)
    x_ref, w_ref = next(it), next(it)
    bias_ref = next(it) if epi == "bias" else None
    res_ref, gate_ref = (next(it), next(it)) if epi == "resgate" else (None, None)
    o_ref = next(it)

    x = x_ref[...]
    if x_silu:
        x = x * jax.nn.sigmoid(x)
    acc = jnp.dot(x.astype(BF16), w_ref[...].astype(BF16), preferred_element_type=F32)
    if epi == "bias":
        acc = acc + bias_ref[...]
    elif epi == "resgate":
        acc = res_ref[...] + gate_ref[...] * acc
    o_ref[...] = acc.astype(o_ref.dtype)


def _mm(x, w, wlead=(), *, name, m_rows=None, tm, tn, tk=None, k0=0, epi="none",
        out_dtype=F32, x_silu=False, bias=None, res=None, gate=None, tab=None):
    m_total = x.shape[0] if m_rows is None else m_rows
    n_total = w.shape[-1]
    tk = x.shape[1] if tk is None else tk
    tm, tn = _tile(m_total, tm), _tile(n_total, tn)
    if tab is not None:
        tm = _tile(tab.group_rows, tm)
    nlead = len(wlead)
    in_specs = [
        pl.BlockSpec((tm, tk), lambda n, m: (m, k0)),
        pl.BlockSpec((None,) * nlead + (tk, tn), lambda n, m: tuple(wlead) + (k0, n)),
    ]
    args = [x, w]
    est = 2 * _nbytes((tm, tk), x.dtype) + 2 * _nbytes((tk, tn), w.dtype) + _nbytes((tk, tn), BF16)
    if epi == "bias":
        in_specs.append(pl.BlockSpec((1, tn), lambda n, m: (0, n)))
        args.append(bias)
    elif epi == "resgate":
        in_specs.append(pl.BlockSpec((tm, tn), lambda n, m: (m, n)))
        in_specs.append(pl.BlockSpec((None, 1, tn), lambda n, m: (tab.index_of_row(m * tm), 0, n)))
        args += [res, gate]
        est += 2 * _nbytes((tm, tn), F32)
    est += 2 * _nbytes((tm, tn), out_dtype) + 2 * _nbytes((tm, tn), F32)
    return pl.pallas_call(
        functools.partial(_mm_kernel, x_silu=x_silu, epi=epi),
        grid=(n_total // tn, m_total // tm),
        in_specs=in_specs,
        out_specs=pl.BlockSpec((tm, tn), lambda n, m: (m, n)),
        out_shape=jax.ShapeDtypeStruct((m_total, n_total), out_dtype),
        compiler_params=pltpu.CompilerParams(
            dimension_semantics=("parallel", "arbitrary"),
            vmem_limit_bytes=_vmem_limit(est)),
        name=name,
    )(*args)


def _rms(x):
    return x * lax.rsqrt(jnp.mean(x * x, axis=-1, keepdims=True) + EPS)


def _modnorm_kernel(x_ref, g_ref, sh_ref, sc_ref, *rest, router):
    h = _rms(x_ref[...]) * g_ref[...] * (1.0 + sc_ref[...]) + sh_ref[...]
    if not router:
        (o_ref,) = rest
        o_ref[...] = h.astype(o_ref.dtype)
        return
    rw_ref, rb_ref, o_ref, gate_ref = rest
    o_ref[...] = h.astype(o_ref.dtype)
    logits = jnp.dot(h, rw_ref[...], preferred_element_type=F32,
                     precision=lax.Precision.HIGHEST) + rb_ref[...]
    lane = lax.broadcasted_iota(jnp.int32, logits.shape, 1).astype(F32)
    neg = jnp.float32(-jnp.inf)
    logits = jnp.where(lane < N_EXPERTS, logits, neg)
    v1 = jnp.max(logits, axis=-1, keepdims=True)
    i1 = jnp.min(jnp.where(logits == v1, lane, float(ROUTER_LANES)), axis=-1, keepdims=True)
    rest_l = jnp.where(lane == i1, neg, logits)
    v2 = jnp.max(rest_l, axis=-1, keepdims=True)
    i2 = jnp.min(jnp.where(rest_l == v2, lane, float(ROUTER_LANES)), axis=-1, keepdims=True)
    e2 = jnp.exp(v2 - v1)
    w1 = 1.0 / (1.0 + e2)
    w2 = e2 * w1
    routed = jnp.where(lane == i1, w1, jnp.where(lane == i2, w2, 0.0))
    for off, val in enumerate((i1, i2, w1, w2)):
        routed = jnp.where(lane == float(N_EXPERTS + off), val, routed)
    gate_ref[...] = routed


def _modnorm(x, g, shift_tab, scale_tab, tab, *, tm=256, router_w=None, router_b=None):
    m_total, d = x.shape
    tm = _tile(tab.group_rows, _tile(m_total, tm))
    row = pl.BlockSpec((tm, d), lambda m: (m, 0))
    vec = pl.BlockSpec((1, d), lambda m: (0, 0))
    per_group = pl.BlockSpec((None, 1, d), lambda m: (tab.index_of_row(m * tm), 0, 0))
    in_specs, args = [row, vec, per_group, per_group], [x, g.reshape(1, d), shift_tab, scale_tab]
    router = router_w is not None
    out_specs, out_shape = row, jax.ShapeDtypeStruct((m_total, d), F32 if router else BF16)
    est = 2 * _nbytes((tm, d), F32) * 3
    if router:
        rw = jnp.zeros((d, ROUTER_LANES), F32).at[:, :N_EXPERTS].set(router_w)
        rb = jnp.zeros((1, ROUTER_LANES), F32).at[0, :N_EXPERTS].set(router_b)
        in_specs += [pl.BlockSpec((d, ROUTER_LANES), lambda m: (0, 0)),
                     pl.BlockSpec((1, ROUTER_LANES), lambda m: (0, 0))]
        args += [rw, rb]
        out_specs = (row, pl.BlockSpec((tm, ROUTER_LANES), lambda m: (m, 0)))
        out_shape = (out_shape, jax.ShapeDtypeStruct((m_total, ROUTER_LANES), F32))
        est += 2 * _nbytes((d, ROUTER_LANES), F32) * 4
    return pl.pallas_call(
        functools.partial(_modnorm_kernel, router=router),
        grid=(m_total // tm,), in_specs=in_specs, out_specs=out_specs, out_shape=out_shape,
        compiler_params=pltpu.CompilerParams(dimension_semantics=("parallel",),
                                             vmem_limit_bytes=_vmem_limit(est)),
        name="modnorm_router" if router else "modnorm",
    )(*args)


def _final_norm_kernel(x_ref, g_ref, o_ref):
    o_ref[...] = _rms(x_ref[...]) * g_ref[...]


def _final_norm(x, g, *, tm=256):
    m_total, d = x.shape
    tm = _tile(m_total, tm)
    row = pl.BlockSpec((tm, d), lambda m: (m, 0))
    return pl.pallas_call(
        _final_norm_kernel, grid=(m_total // tm,),
        in_specs=[row, pl.BlockSpec((1, d), lambda m: (0, 0))], out_specs=row,
        out_shape=jax.ShapeDtypeStruct((m_total, d), F32),
        compiler_params=pltpu.CompilerParams(dimension_semantics=("parallel",),
                                             vmem_limit_bytes=_vmem_limit(6 * _nbytes((tm, d), F32))),
        name="final_norm",
    )(x, g.reshape(1, d))


class PrepHead(NamedTuple):
    src_col: int
    norm: str
    scale: float


PREP_SRC_W = 512
ATTN_SCALE = HEAD_DIM ** -0.5


def _prep_plan():
    h = HEAD_DIM
    plan = [PrepHead(j * h, "q", ATTN_SCALE) for j in range(8)]
    plan += [PrepHead((8 + j) * h, "k", 1.0) for j in range(2)]
    plan += [PrepHead((20 + j) * h, "", ATTN_SCALE) for j in range(8)]
    plan += [PrepHead((28 + j) * h, "", 1.0) for j in range(8)]
    plan += [PrepHead((44 + j) * h, "", 1.0) for j in range(4)]
    plan += [PrepHead((48 + j) * h, "", ATTN_SCALE) for j in range(4)]
    return plan


def _prep_kernel(*refs, plan, src_blocks):
    src_refs = dict(zip(src_blocks, refs[:len(src_blocks)]))
    cos_ref, sin_ref, qn_ref, kn_ref, o_ref = refs[len(src_blocks):]
    cos, sin = cos_ref[...], sin_ref[...]
    lane = lax.broadcasted_iota(jnp.int32, cos.shape, 1)
    first_half = (lane % (2 * ROPE_FREQS)) < ROPE_FREQS
    gains = {"q": qn_ref[...], "k": kn_ref[...]}
    for j, head in enumerate(plan):
        blk, off = divmod(head.src_col, PREP_SRC_W)
        y = src_refs[blk][:, off:off + HEAD_DIM]
        if head.norm:
            y = _rms(y) * gains[head.norm]
        if head.scale != 1.0:
            y = y * head.scale
        partner = jnp.where(first_half, pltpu.roll(y, HEAD_DIM - ROPE_FREQS, 1), pltpu.roll(y, ROPE_FREQS, 1))
        o_ref[:, j * HEAD_DIM:(j + 1) * HEAD_DIM] = (y * cos + partner * sin).astype(o_ref.dtype)


def _prep(p, cos_t, sin_t, q_gain, k_gain, *, tm=256):
    m_total = p.shape[0]
    tm = _tile(m_total, tm)
    plan = _prep_plan()
    src_blocks = sorted({head.src_col // PREP_SRC_W for head in plan})
    row = lambda w, blk: pl.BlockSpec((tm, w), lambda m: (m, blk))
    vec = pl.BlockSpec((1, HEAD_DIM), lambda m: (0, 0))
    out_w = len(plan) * HEAD_DIM
    est = 2 * len(src_blocks) * _nbytes((tm, PREP_SRC_W), F32) + 2 * _nbytes((tm, out_w), BF16)
    return pl.pallas_call(
        functools.partial(_prep_kernel, plan=plan, src_blocks=src_blocks),
        grid=(m_total // tm,),
        in_specs=[row(PREP_SRC_W, blk) for blk in src_blocks] + [row(HEAD_DIM, 0), row(HEAD_DIM, 0), vec, vec],
        out_specs=row(out_w, 0),
        out_shape=jax.ShapeDtypeStruct((m_total, out_w), BF16),
        compiler_params=pltpu.CompilerParams(dimension_semantics=("parallel",),
                                             vmem_limit_bytes=_vmem_limit(est)),
        name="qk_prep",
    )(*([p] * len(src_blocks)), cos_t, sin_t, q_gain.reshape(1, HEAD_DIM), k_gain.reshape(1, HEAD_DIM))


def _rope_tables(t_lat, n_batch, t_ctx):
    rows = t_lat // GRID_W
    row = jnp.broadcast_to(jnp.arange(rows)[:, None], (rows, GRID_W)).reshape(-1)
    col = jnp.broadcast_to(jnp.arange(GRID_W)[None, :], (rows, GRID_W)).reshape(-1)
    inv = ROPE_BASE ** (-jnp.arange(ROPE_FREQS, dtype=F32) / ROPE_FREQS)
    ang = jnp.stack([row, col], axis=-1).astype(F32)[:, :, None] * inv
    cos, sin = jnp.cos(ang), jnp.sin(ang)
    cos128 = jnp.concatenate([cos, cos], axis=-1).reshape(t_lat, HEAD_DIM)
    sin128 = jnp.concatenate([-sin, sin], axis=-1).reshape(t_lat, HEAD_DIM)
    cos_t = jnp.concatenate([jnp.tile(cos128, (n_batch, 1)), jnp.ones((n_batch * t_ctx, HEAD_DIM), F32)])
    sin_t = jnp.concatenate([jnp.tile(sin128, (n_batch, 1)), jnp.zeros((n_batch * t_ctx, HEAD_DIM), F32)])
    return cos_t, sin_t


_NT = (((1,), (1,)), ((), ()))


def _softmax_parts(q, ks):
    ss = [lax.dot_general(q, k, _NT, preferred_element_type=F32) for k in ks]
    m = functools.reduce(jnp.maximum, [jnp.max(s, axis=-1, keepdims=True) for s in ss])
    es = [jnp.exp(s - m) for s in ss]
    l = functools.reduce(jnp.add, [jnp.sum(e, axis=-1, keepdims=True) for e in es])
    return es, l


def _gqa_kernel(q_ref, *rest, nseg):
    k_refs, v_refs, o_ref = rest[:nseg], rest[nseg:2 * nseg], rest[2 * nseg]
    ks = [r[...] for r in k_refs]
    vs = [r[...].astype(BF16) for r in v_refs]
    for g in range(A_GROUP):
        cols = slice(g * HEAD_DIM, (g + 1) * HEAD_DIM)
        es, l = _softmax_parts(q_ref[:, cols], ks)
        o = functools.reduce(jnp.add, [jnp.dot(e.astype(BF16), v, preferred_element_type=F32)
                                       for e, v in zip(es, vs)])
        o_ref[:, cols] = (o / l).astype(o_ref.dtype)


def _diff_kernel(lam_ref, q_ref, *rest, nseg, out_scale):
    k_refs, v_refs = rest[:nseg], rest[nseg:2 * nseg]
    gain_ref, o_ref = rest[2 * nseg], rest[2 * nseg + 1]
    lam = lam_ref[0]
    vs = [r[...].astype(BF16) for r in v_refs]
    parts = []
    for m in range(2):
        cols = slice(m * HEAD_DIM, (m + 1) * HEAD_DIM)
        es, l = _softmax_parts(q_ref[:, cols], [r[:, cols] for r in k_refs])
        parts.append((es, 1.0 / l))
    (e0, r0), (e1, r1) = parts
    o = functools.reduce(jnp.add, [
        jnp.dot((a * r0 - lam * (b * r1)).astype(BF16), v, preferred_element_type=F32)
        for a, b, v in zip(e0, e1, vs)])
    o_ref[...] = (_rms(o) * gain_ref[...] * out_scale).astype(o_ref.dtype)


def _attention(kernel, qk, p, *, name, n_batch, t_lat, t_ctx, heads, q_w, q_blk0, k_w, k_blk0, v_w, v_blk0,
               out_w, ctx_queries, tq, extra_in=(), extra_specs=(), lead_in=(), lead_specs=()):
    n_lat = n_batch * t_lat
    ctx_row0 = n_lat // t_ctx
    tq = _tile(t_lat, tq)
    if ctx_queries:
        nq, q_rows = 1, t_ctx
        q_map = lambda b, h, i: (ctx_row0 + b, q_blk0 + h)
        out_rows, o_map = n_batch * t_ctx, (lambda b, h, i: (b, h))
        segs = [(t_ctx, lambda b: ctx_row0 + b)]
    else:
        nq, q_rows = t_lat // tq, tq
        q_map = lambda b, h, i: (b * nq + i, q_blk0 + h)
        out_rows, o_map = n_lat, (lambda b, h, i: (b * nq + i, h))
        segs = [(t_ctx, lambda b: ctx_row0 + b), (t_lat, lambda b: b)]
    k_specs = [pl.BlockSpec((rows, k_w), functools.partial(lambda rf, b, h, i: (rf(b), k_blk0 + h), rf))
               for rows, rf in segs]
    v_specs = [pl.BlockSpec((rows, v_w), functools.partial(lambda rf, b, h, i: (rf(b), v_blk0 + h), rf))
               for rows, rf in segs]
    nseg = len(segs)
    est = 8 * _nbytes((q_rows, t_lat + t_ctx), F32) + 4 * _nbytes((t_lat + t_ctx, v_w), F32)
    return pl.pallas_call(
        functools.partial(kernel, nseg=nseg),
        grid=(n_batch, heads, nq),
        in_specs=list(lead_specs) + [pl.BlockSpec((q_rows, q_w), q_map)] + k_specs + v_specs + list(extra_specs),
        out_specs=pl.BlockSpec((q_rows, out_w), o_map),
        out_shape=jax.ShapeDtypeStruct((out_rows, heads * out_w), BF16),
        compiler_params=pltpu.CompilerParams(dimension_semantics=("parallel", "parallel", "arbitrary"),
                                             vmem_limit_bytes=_vmem_limit(est)),
        name=name + ("_ctx" if ctx_queries else "_lat"),
    )(*lead_in, qk, *([qk] * nseg), *([p] * nseg), *extra_in)


def _ret_kernel(lg_ref, ql_ref, qc_ref, kl_ref, kc_ref, vl_ref, vc_ref, gl_ref, gc_ref, gain_ref,
                yl_ref, yc_ref, ol_scr, oc_scr, s_scr, *, t_lat, t_ctx):
    h = pl.program_id(1)
    lg_f, lg_b = lg_ref[0, h], lg_ref[1, h]
    c = RET_CHUNK
    pos_i = lax.broadcasted_iota(jnp.int32, (c, c), 0).astype(F32)
    pos_j = lax.broadcasted_iota(jnp.int32, (c, c), 1).astype(F32)
    rel = pos_i - pos_j
    d_both = (jnp.where(rel >= 0, jnp.exp(jnp.maximum(rel, 0.0) * lg_f), 0.0)
              + jnp.where(rel <= 0, jnp.exp(jnp.maximum(-rel, 0.0) * lg_b), 0.0))
    pos = lax.broadcasted_iota(jnp.int32, (c, 1), 0).astype(F32)
    qdec = (jnp.exp((pos + 1.0) * lg_f), jnp.exp((c - pos) * lg_b))
    kdec = (jnp.exp((c - 1.0 - pos) * lg_f), jnp.exp(pos * lg_b))
    one = jnp.ones((1, 1), F32)
    sdec = (jnp.exp(one * (c * lg_f)), jnp.exp(one * (c * lg_b)))

    chunks = [(qc_ref, kc_ref, vc_ref, oc_scr, i) for i in range(t_ctx // c)]
    chunks += [(ql_ref, kl_ref, vl_ref, ol_scr, i) for i in range(t_lat // c)]
    back = ([ch for ch in chunks if ch[0] is qc_ref][::-1] + [ch for ch in chunks if ch[0] is ql_ref][::-1])

    for d, order in ((0, chunks), (1, back)):
        s_scr[...] = jnp.zeros_like(s_scr)
        for q_ref, k_ref, v_ref, o_scr, i in order:
            rows = pl.ds(i * c, c)
            q, k = q_ref[rows, :], k_ref[rows, :]
            v = v_ref[rows, :].astype(BF16)
            state = s_scr[...]
            cross = jnp.dot((q.astype(F32) * qdec[d]).astype(BF16), state.astype(BF16),
                            preferred_element_type=F32)
            if d == 0:
                scores = lax.dot_general(q, k, _NT, preferred_element_type=F32) * d_both
                o_scr[rows, :] = cross + jnp.dot(scores.astype(BF16), v, preferred_element_type=F32)
            else:
                o_scr[rows, :] += cross
            kv = jnp.dot((k.astype(F32) * kdec[d]).T.astype(BF16), v, preferred_element_type=F32)
            s_scr[...] = sdec[d] * state + kv

    gain = gain_ref[...]
    for o_scr, g_ref, y_ref in ((ol_scr, gl_ref, yl_ref), (oc_scr, gc_ref, yc_ref)):
        g = g_ref[...]
        y_ref[...] = (_rms(o_scr[...]) * gain * (g * jax.nn.sigmoid(g))).astype(y_ref.dtype)


def _retention(qk, p, log_gamma, gain, *, n_batch, t_lat, t_ctx, heads, q_blk0, k_blk0, v_blk0, g_blk0, dv):
    n_lat = n_batch * t_lat
    ctx_row0 = n_lat // t_ctx
    lat = lambda w, blk0: pl.BlockSpec((t_lat, w), lambda b, h: (b, blk0 + h))
    ctx = lambda w, blk0: pl.BlockSpec((t_ctx, w), lambda b, h: (ctx_row0 + b, blk0 + h))
    est = 6 * _nbytes((t_lat + t_ctx, dv), F32) * 2
    return pl.pallas_call(
        functools.partial(_ret_kernel, t_lat=t_lat, t_ctx=t_ctx),
        grid=(n_batch, heads),
        in_specs=[pl.BlockSpec(memory_space=pltpu.SMEM),
                  lat(HEAD_DIM, q_blk0), ctx(HEAD_DIM, q_blk0), lat(HEAD_DIM, k_blk0), ctx(HEAD_DIM, k_blk0),
                  lat(dv, v_blk0), ctx(dv, v_blk0), lat(dv, g_blk0), ctx(dv, g_blk0),
                  pl.BlockSpec((1, dv), lambda b, h: (0, 0))],
        out_specs=(pl.BlockSpec((t_lat, dv), lambda b, h: (b, h)),
                   pl.BlockSpec((t_ctx, dv), lambda b, h: (b, h))),
        out_shape=(jax.ShapeDtypeStruct((n_lat, heads * dv), BF16),
                   jax.ShapeDtypeStruct((n_batch * t_ctx, heads * dv), BF16)),
        scratch_shapes=[pltpu.VMEM((t_lat, dv), F32), pltpu.VMEM((t_ctx, dv), F32),
                        pltpu.VMEM((HEAD_DIM, dv), F32)],
        compiler_params=pltpu.CompilerParams(dimension_semantics=("parallel", "arbitrary"),
                                             vmem_limit_bytes=_vmem_limit(est)),
        name="retention",
    )(log_gamma, qk, qk, qk, qk, p, p, p, p, gain.reshape(1, dv))


def _s5_tables(a_re, a_im, log_step, b_re, b_im, c_re, c_im):
    hp = lax.Precision.HIGHEST
    n_dir, n_grp, n_st = a_re.shape
    ln = S5_L
    tg = LANES // S5_P
    n_tile = n_grp // tg
    lam = lax.complex(a_re.astype(F32), a_im.astype(F32))
    step = jnp.exp(log_step.astype(F32))[..., None]
    a_bar = jnp.exp(lam * step)
    b_bar = ((a_bar - 1.0) / lam)[..., None] * lax.complex(b_re.astype(F32), b_im.astype(F32))
    c_mat = lax.complex(c_re.astype(F32), c_im.astype(F32))
    taus = jnp.arange(ln + 1, dtype=F32)
    apow = jnp.exp((lam * step)[..., None] * taus)

    kern = jnp.einsum("dgpn,dgnt,dgnq->dgtpq", c_mat, apow[..., :ln], b_bar, precision=hp).real
    ti = jnp.arange(ln)[:, None]
    tj = jnp.arange(ln)[None, :]
    t_f = kern[0][:, jnp.clip(ti - tj, 0)] * (ti >= tj)[None, :, :, None, None]
    t_b = kern[1][:, jnp.clip(tj - ti, 0)] * (tj >= ti)[None, :, :, None, None]
    def widen(compact, n_x, n_y, row_group):
        k_in, k_out = n_x * n_y, n_x * tg * n_y
        r, c = jnp.arange(k_in)[:, None], jnp.arange(k_out)[None, :]
        spread = ((r // n_y == c // (tg * n_y)) & (r % n_y == c % n_y)).astype(BF16)
        own = row_group[:, None] == (jnp.arange(k_out) % (tg * n_y) // n_y)[None, :]
        wide = jnp.einsum("...rk,kc->...rc", compact.astype(BF16), spread, preferred_element_type=F32)
        return (wide * own).astype(BF16)

    lane_group = jnp.arange(LANES) // S5_P
    state_group = jnp.arange(2 * tg * n_st) % (tg * n_st) // n_st

    toep = (t_f + t_b).reshape(n_tile, tg, ln, ln, S5_P, S5_P)
    m_intra = widen(toep.transpose(0, 3, 1, 5, 2, 4).reshape(n_tile, ln, LANES, ln * S5_P),
                    ln, S5_P, lane_group)

    def in_mat(d, tau_of_j):
        w = apow[d][:, :, tau_of_j][..., None] * b_bar[d][:, :, None, :]
        wri = jnp.stack([w.real, w.imag]).reshape(2, n_tile, tg, n_st, ln, S5_P)
        return widen(wri.transpose(1, 4, 2, 5, 0, 3).reshape(n_tile, ln, LANES, 2 * n_st), 2, n_st, lane_group)

    def out_mat(d, tau_of_i):
        w = c_mat[d][:, :, :, None] * apow[d][:, None, :, tau_of_i]
        wri = jnp.stack([w.real, -w.imag]).reshape(2, n_tile, tg, S5_P, n_st, ln)
        return widen(wri.transpose(1, 0, 2, 4, 5, 3).reshape(n_tile, 2 * tg * n_st, ln * S5_P),
                     ln, S5_P, state_group)

    idx = jnp.arange(ln)
    mats = [m_intra, in_mat(0, ln - 1 - idx), in_mat(1, idx), out_mat(0, idx + 1), out_mat(1, ln - idx)]
    dec = [apow[0][..., ln].real, apow[0][..., ln].imag, apow[1][..., ln].real, apow[1][..., ln].imag]
    return mats, [x.reshape(n_tile, 1, tg * n_st) for x in dec]


def _s5_kernel(u_ref, m_ref, bcf_ref, bcb_ref, ccf_ref, ccb_ref, dfr_ref, dfi_ref, dbr_ref, dbi_ref,
               y_ref, vf, vb, *, n_batch, t_lat, t_ctx):
    ln = S5_L
    n_lat = n_batch * t_lat
    n_cr, n_lr = t_ctx // ln, t_lat // ln
    n_st = vf.shape[0]
    n_re = n_st // 2

    def chunk_tokens(b, j):
        ctx_rows = u_ref[pl.ds(n_lat + b * t_ctx + j, n_cr, stride=ln), :]
        lat_rows = u_ref[pl.ds(b * t_lat + j, n_lr, stride=ln), :]
        return jnp.concatenate([ctx_rows, lat_rows], axis=0).astype(BF16)

    def mix(us, w_ref):
        return functools.reduce(jnp.add, [jnp.dot(us[j], w_ref[j], preferred_element_type=F32)
                                          for j in range(ln)])

    def batch_rows(b):
        return pl.ds(b, n_cr + n_lr, stride=n_batch)

    for b in range(n_batch):
        us = [chunk_tokens(b, j) for j in range(ln)]
        for v, w_ref in ((vf, bcf_ref), (vb, bcb_ref)):
            contrib = mix(us, w_ref)
            for c in range(n_st):
                v[c, batch_rows(b), :] = contrib[:, c * LANES:(c + 1) * LANES]

    tile_rows = 8
    tile_chunks = tile_rows // n_batch
    n_ct, n_tiles = n_cr // tile_chunks, (n_cr + n_lr) // tile_chunks

    def scan(v, d_re, d_im, tile_of_step, backward):
        order = range(tile_chunks - 1, -1, -1) if backward else range(tile_chunks)

        def step(s, carry):
            rows = pl.ds(pl.multiple_of(tile_of_step(s) * tile_rows, tile_rows), tile_rows)
            new_carry = []
            for c in range(n_re):
                s_re, s_im = carry[2 * c], carry[2 * c + 1]
                t_re, t_im = v[c, rows, :], v[n_re + c, rows, :]
                dr, di = d_re[:, c * LANES:(c + 1) * LANES], d_im[:, c * LANES:(c + 1) * LANES]
                o_re, o_im = [None] * tile_chunks, [None] * tile_chunks
                for k in order:
                    sub = slice(k * n_batch, (k + 1) * n_batch)
                    o_re[k], o_im[k] = s_re, s_im
                    s_re, s_im = dr * s_re - di * s_im + t_re[sub], dr * s_im + di * s_re + t_im[sub]
                v[c, rows, :] = jnp.concatenate(o_re, axis=0)
                v[n_re + c, rows, :] = jnp.concatenate(o_im, axis=0)
                new_carry += [s_re, s_im]
            return tuple(new_carry)

        zero = jnp.zeros((n_batch, LANES), F32)
        lax.fori_loop(0, n_tiles, step, (zero,) * n_st)

    scan(vf, dfr_ref[...], dfi_ref[...], lambda s: s, False)
    scan(vb, dbr_ref[...], dbi_ref[...],
         lambda s: jnp.where(s < n_ct, n_ct - 1 - s, n_tiles + n_ct - 1 - s), True)

    def entering_state(v, b):
        return jnp.concatenate([v[c, batch_rows(b), :] for c in range(n_st)], axis=1).astype(BF16)

    for b in range(n_batch):
        us = [chunk_tokens(b, j) for j in range(ln)]
        y = (mix(us, m_ref)
             + jnp.dot(entering_state(vf, b), ccf_ref[...], preferred_element_type=F32)
             + jnp.dot(entering_state(vb, b), ccb_ref[...], preferred_element_type=F32))
        for i in range(ln):
            lanes = slice(i * LANES, (i + 1) * LANES)
            y_ref[pl.ds(n_lat + b * t_ctx + i, n_cr, stride=ln), :] = y[:n_cr, lanes]
            y_ref[pl.ds(b * t_lat + i, n_lr, stride=ln), :] = y[n_cr:, lanes]


def _s5_core(p, col_blk0, mats, decs, *, n_batch, t_lat, t_ctx):
    rows = p.shape[0]
    n_tile = mats[0].shape[0]
    n_chunk = (t_lat + t_ctx) // S5_L
    assert 8 % n_batch == 0 and (t_ctx // S5_L) % (8 // n_batch) == 0 and (t_lat // S5_L) % (8 // n_batch) == 0
    state = mats[1].shape[-1]
    mat_specs = [pl.BlockSpec((None,) + m.shape[1:], lambda t, nd=m.ndim: (t,) + (0,) * (nd - 1)) for m in mats]
    dec_specs = [pl.BlockSpec((None, 1, state // 2), lambda t: (t, 0, 0)) for _ in decs]
    est = (4 * _nbytes((rows, LANES), F32) + 2 * sum(_nbytes(m.shape[1:], BF16) for m in mats)
           + 2 * _nbytes((n_chunk * n_batch, state), F32) + 6 * _nbytes((n_chunk, state), F32))
    return pl.pallas_call(
        functools.partial(_s5_kernel, n_batch=n_batch, t_lat=t_lat, t_ctx=t_ctx),
        grid=(n_tile,),
        in_specs=[pl.BlockSpec((rows, LANES), lambda t: (0, col_blk0 + t))] + mat_specs + dec_specs,
        out_specs=pl.BlockSpec((rows, LANES), lambda t: (0, t)),
        out_shape=jax.ShapeDtypeStruct((rows, n_tile * LANES), F32),
        scratch_shapes=[pltpu.VMEM((state // LANES, n_chunk * n_batch, LANES), F32) for _ in range(2)],
        compiler_params=pltpu.CompilerParams(dimension_semantics=("arbitrary",),
                                             vmem_limit_bytes=_vmem_limit(est)),
        name="s5_core",
    )(p, *mats, *decs)


def _s5_finish_kernel(y_ref, ua_ref, ub_ref, d_ref, w_ref, o_ref):
    u = jnp.concatenate([ua_ref[...], ub_ref[...]], axis=1)
    z = jax.nn.gelu(y_ref[...] + d_ref[...] * u)
    gl = jnp.dot(z.astype(BF16), w_ref[...].astype(BF16), preferred_element_type=F32)
    o_ref[...] = (z * jax.nn.sigmoid(gl)).astype(o_ref.dtype)


def _s5_finish(y, p, col_half0, d_skip, w_glu, wlead, *, tm=512):
    m_total, w = y.shape
    tm = _tile(m_total, tm)
    row = pl.BlockSpec((tm, w), lambda m: (m, 0))
    nlead = len(wlead)
    return pl.pallas_call(
        _s5_finish_kernel, grid=(m_total // tm,),
        in_specs=[row, pl.BlockSpec((tm, w // 2), lambda m: (m, col_half0)),
                  pl.BlockSpec((tm, w // 2), lambda m: (m, col_half0 + 1)),
                  pl.BlockSpec((1, w), lambda m: (0, 0)),
                  pl.BlockSpec((None,) * nlead + (w, w), lambda m: tuple(wlead) + (0, 0))],
        out_specs=row, out_shape=jax.ShapeDtypeStruct((m_total, w), BF16),
        compiler_params=pltpu.CompilerParams(
            dimension_semantics=("parallel",),
            vmem_limit_bytes=_vmem_limit(8 * _nbytes((tm, w), F32) + 3 * _nbytes((w, w), F32))),
        name="s5_finish",
    )(y, p, p, d_skip.reshape(1, w), w_glu)


def _s5_branch(p, col0, bw, params, layer, *, n_batch, t_lat, t_ctx):
    mats, decs = _s5_tables(*(params[k][layer] for k in
                              ("s5_a_re", "s5_a_im", "s5_log_step", "s5_b_re", "s5_b_im", "s5_c_re", "s5_c_im")))
    y = _s5_core(p, col0 // LANES, mats, decs, n_batch=n_batch, t_lat=t_lat, t_ctx=t_ctx)
    return _s5_finish(y, p, col0 // (bw // 2), params["s5_d"][layer], params["s5_w_glu"], (layer,))


def _merge_kernel(h_ref, *rest, n_branch):
    y_refs = rest[:n_branch]
    wg_ref, wb_ref, o_ref, acc_ref, val_ref = rest[n_branch:]
    j = pl.program_id(2)
    wb = wb_ref[...].astype(BF16)
    for idx, y_ref in enumerate(y_refs):
        @pl.when(j == idx)
        def _(y_ref=y_ref):
            val_ref[...] = jnp.dot(y_ref[...], wb, preferred_element_type=F32)

    gate = jnp.dot(h_ref[...], wg_ref[...].astype(BF16), preferred_element_type=F32)
    term = jax.nn.sigmoid(gate) * val_ref[...]

    @pl.when(j == 0)
    def _():
        acc_ref[...] = term

    @pl.when(j > 0)
    def _():
        acc_ref[...] += term

    @pl.when(j == n_branch - 1)
    def _():
        o_ref[...] = acc_ref[...].astype(o_ref.dtype)


def _merge(h, ys, w_gate, w_branch, layer, *, m_rows, tm=1024, tn=256):
    d = h.shape[1]
    tm = _tile(m_rows, tm)
    n_branch, bw = w_branch.shape[1], w_branch.shape[2]
    assert len(ys) == n_branch
    est = (2 * _nbytes((tm, d), BF16) + 2 * n_branch * _nbytes((tm, bw), BF16) + 3 * _nbytes((d, tn), F32)
           + 3 * _nbytes((bw, tn), F32) + 6 * _nbytes((tm, tn), F32))
    return pl.pallas_call(
        functools.partial(_merge_kernel, n_branch=n_branch),
        grid=(m_rows // tm, d // tn, n_branch),
        in_specs=[pl.BlockSpec((tm, d), lambda m, n, j: (m, 0))]
                 + [pl.BlockSpec((tm, bw), lambda m, n, j: (m, 0))] * n_branch
                 + [pl.BlockSpec((None, None, d, tn), lambda m, n, j: (layer, j, 0, n)),
                    pl.BlockSpec((None, None, bw, tn), lambda m, n, j: (layer, j, 0, n))],
        out_specs=pl.BlockSpec((tm, tn), lambda m, n, j: (m, n)),
        out_shape=jax.ShapeDtypeStruct((m_rows, d), BF16),
        scratch_shapes=[pltpu.VMEM((tm, tn), F32), pltpu.VMEM((tm, tn), F32)],
        compiler_params=pltpu.CompilerParams(dimension_semantics=("parallel", "parallel", "arbitrary"),
                                             vmem_limit_bytes=_vmem_limit(est)),
        name="merge",
    )(h, *ys, w_gate, w_branch)


def _swiglu_tile(x, w1_ref, w3_ref):
    a = jnp.dot(x, w1_ref[...].astype(BF16), preferred_element_type=F32)
    b = jnp.dot(x, w3_ref[...].astype(BF16), preferred_element_type=F32)
    return a * jax.nn.sigmoid(a) * b


def _swiglu_up_kernel(x_ref, w1_ref, w3_ref, o_ref):
    o_ref[...] = _swiglu_tile(x_ref[...], w1_ref, w3_ref).astype(o_ref.dtype)


def _swiglu_up(x, w1, w3, wlead, *, tm=1024, tn=256):
    m_total, d = x.shape
    tm = _tile(m_total, tm)
    ff = w1.shape[-1]
    nlead = len(wlead)
    w_spec = pl.BlockSpec((None,) * nlead + (d, tn), lambda n, m: tuple(wlead) + (0, n))
    est = 2 * _nbytes((tm, d), BF16) + 6 * _nbytes((d, tn), F32) + 6 * _nbytes((tm, tn), F32)
    return pl.pallas_call(
        _swiglu_up_kernel,
        grid=(ff // tn, m_total // tm),
        in_specs=[pl.BlockSpec((tm, d), lambda n, m: (m, 0)), w_spec, w_spec],
        out_specs=pl.BlockSpec((tm, tn), lambda n, m: (m, n)),
        out_shape=jax.ShapeDtypeStruct((m_total, ff), BF16),
        compiler_params=pltpu.CompilerParams(dimension_semantics=("parallel", "arbitrary"),
                                             vmem_limit_bytes=_vmem_limit(est)),
        name="ffn_up",
    )(x, w1, w3)


MOE_TOP_K = 2
MOE_ROW_TILE = 512
MOE_GATHER_ROWS = 256
MOE_COMBINE_ROWS = 128
MOE_DMA_UNROLL = 8


def _moe_routing(gates, n_tok):
    tile = MOE_ROW_TILE
    n_rows = MOE_TOP_K * n_tok + N_EXPERTS * tile
    experts = jnp.concatenate([gates[:, N_EXPERTS + k] for k in range(MOE_TOP_K)]).astype(jnp.int32)
    onehot = (experts[:, None] == jnp.arange(N_EXPERTS, dtype=jnp.int32)[None, :]).astype(jnp.int32)
    before = jnp.cumsum(onehot, axis=0) - onehot
    counts = jnp.sum(onehot, axis=0)
    padded = ((counts + tile - 1) // tile) * tile
    ends = jnp.cumsum(padded)
    dest = jnp.sum(onehot * (before + (ends - padded)[None, :]), axis=1)
    tokens = jnp.tile(jnp.arange(n_tok, dtype=jnp.int32), MOE_TOP_K)
    src = jnp.zeros((n_rows,), jnp.int32).at[dest].set(tokens)
    tile_start = jnp.arange(n_rows // tile, dtype=jnp.int32) * tile
    tile_expert = jnp.minimum(jnp.sum((tile_start[:, None] >= ends[None, :]).astype(jnp.int32), axis=1),
                              N_EXPERTS - 1)
    return src, dest, tile_expert, (ends[-1:] // tile).astype(jnp.int32)


def _row_copy(src_hbm, row, dst_vmem, slot, sem):
    return pltpu.make_async_copy(src_hbm.at[pl.ds(row, 1)], dst_vmem.at[pl.ds(slot, 1)], sem)


def _moe_gather_kernel(src_ref, nused_ref, x_hbm, o_ref, buf, sem, *, rows, steps_per_tile):
    step = pl.program_id(0)
    used = step < nused_ref[0] * steps_per_tile

    @pl.when(used)
    def _():
        base = step * rows

        def start(r, carry):
            _row_copy(x_hbm, src_ref[base + r], buf, r, sem).start()
            return carry

        def wait(r, carry):
            _row_copy(x_hbm, 0, buf, r, sem).wait()
            return carry

        lax.fori_loop(0, rows, start, 0, unroll=MOE_DMA_UNROLL)
        lax.fori_loop(0, rows, wait, 0, unroll=MOE_DMA_UNROLL)
        o_ref[...] = buf[...].astype(o_ref.dtype)

    @pl.when(jnp.logical_not(used))
    def _():
        o_ref[...] = jnp.zeros_like(o_ref)


def _moe_gather(x, src, n_used):
    n_rows, d = src.shape[0], x.shape[1]
    rows = MOE_GATHER_ROWS
    return pl.pallas_call(
        functools.partial(_moe_gather_kernel, rows=rows, steps_per_tile=MOE_ROW_TILE // rows),
        grid_spec=pltpu.PrefetchScalarGridSpec(
            num_scalar_prefetch=2, grid=(n_rows // rows,),
            in_specs=[pl.BlockSpec(memory_space=pl.ANY)],
            out_specs=pl.BlockSpec((rows, d), lambda i, s, nu: (i, 0)),
            scratch_shapes=[pltpu.VMEM((rows, d), F32), pltpu.SemaphoreType.DMA]),
        out_shape=jax.ShapeDtypeStruct((n_rows, d), BF16),
        compiler_params=pltpu.CompilerParams(
            dimension_semantics=("arbitrary",),
            vmem_limit_bytes=_vmem_limit(_nbytes((rows, d), F32) + 2 * _nbytes((rows, d), BF16))),
        name="moe_gather",
    )(src, n_used, x)


def _moe_up_kernel(te_ref, nused_ref, x_ref, w1_ref, w3_ref, o_ref):
    used = pl.program_id(1) < nused_ref[0]

    @pl.when(used)
    def _():
        o_ref[...] = _swiglu_tile(x_ref[...], w1_ref, w3_ref).astype(o_ref.dtype)

    @pl.when(jnp.logical_not(used))
    def _():
        o_ref[...] = jnp.zeros_like(o_ref)


def _moe_down_kernel(te_ref, nused_ref, u_ref, w2_ref, o_ref):
    used = pl.program_id(1) < nused_ref[0]

    @pl.when(used)
    def _():
        o_ref[...] = jnp.dot(u_ref[...], w2_ref[...].astype(BF16), preferred_element_type=F32)

    @pl.when(jnp.logical_not(used))
    def _():
        o_ref[...] = jnp.zeros_like(o_ref)


def _moe_grouped(kernel, x, ws, layer, tile_expert, n_used, *, tn, out_dtype, name):
    n_rows, kdim = x.shape
    n_out = ws[0].shape[-1]
    tm = MOE_ROW_TILE

    def row_blk(m, nu):
        return jnp.minimum(m, nu[0] - 1)

    w_spec = pl.BlockSpec((None, None, kdim, tn), lambda n, m, te, nu: (layer, te[row_blk(m, nu)], 0, n))
    est = (2 * _nbytes((tm, kdim), BF16) + len(ws) * 3 * _nbytes((kdim, tn), F32)
           + (2 + len(ws)) * _nbytes((tm, tn), F32))
    return pl.pallas_call(
        kernel,
        grid_spec=pltpu.PrefetchScalarGridSpec(
            num_scalar_prefetch=2, grid=(n_out // tn, n_rows // tm),
            in_specs=[pl.BlockSpec((tm, kdim), lambda n, m, te, nu: (row_blk(m, nu), 0))] + [w_spec] * len(ws),
            out_specs=pl.BlockSpec((tm, tn), lambda n, m, te, nu: (m, n))),
        out_shape=jax.ShapeDtypeStruct((n_rows, n_out), out_dtype),
        compiler_params=pltpu.CompilerParams(dimension_semantics=("parallel", "arbitrary"),
                                             vmem_limit_bytes=_vmem_limit(est)),
        name=name,
    )(tile_expert, n_used, x, *ws)


def _moe_combine_kernel(dest_ref, y_hbm, xs_ref, route_ref, gate_ref, o_ref, buf, sem, *, rows, n_tok):
    base = pl.program_id(0) * rows
    for k in range(MOE_TOP_K):
        def start(r, carry, k=k):
            _row_copy(y_hbm, dest_ref[k * n_tok + base + r], buf.at[k], r, sem).start()
            return carry

        lax.fori_loop(0, rows, start, 0, unroll=MOE_DMA_UNROLL)
    for k in range(MOE_TOP_K):
        def wait(r, carry, k=k):
            _row_copy(y_hbm, 0, buf.at[k], r, sem).wait()
            return carry

        lax.fori_loop(0, rows, wait, 0, unroll=MOE_DMA_UNROLL)
    route = route_ref[...]
    w_lane0 = N_EXPERTS + MOE_TOP_K
    mix = functools.reduce(jnp.add, [route[:, w_lane0 + k:w_lane0 + k + 1] * buf[k] for k in range(MOE_TOP_K)])
    o_ref[...] = xs_ref[...] + gate_ref[...] * mix


def _moe_combine(xs, y_rows, dest, route, gate, tab):
    n_tok, d = xs.shape
    rows = _tile(tab.group_rows, MOE_COMBINE_ROWS)
    row = lambda w: pl.BlockSpec((rows, w), lambda i, dst: (i, 0))
    est = (MOE_TOP_K + 6) * _nbytes((rows, d), F32)
    return pl.pallas_call(
        functools.partial(_moe_combine_kernel, rows=rows, n_tok=n_tok),
        grid_spec=pltpu.PrefetchScalarGridSpec(
            num_scalar_prefetch=1, grid=(n_tok // rows,),
            in_specs=[pl.BlockSpec(memory_space=pl.ANY), row(d), row(route.shape[1]),
                      pl.BlockSpec((None, 1, d), lambda i, dst: (tab.index_of_row(i * rows), 0, 0))],
            out_specs=row(d),
            scratch_shapes=[pltpu.VMEM((MOE_TOP_K, rows, d), F32), pltpu.SemaphoreType.DMA]),
        out_shape=jax.ShapeDtypeStruct((n_tok, d), F32),
        compiler_params=pltpu.CompilerParams(dimension_semantics=("arbitrary",),
                                             vmem_limit_bytes=_vmem_limit(est)),
        name="moe_combine",
    )(dest, y_rows, xs, route, gate)


def _moe_sparse(xs, h2, route, w1, w3, w2, layer, gate, tab):
    n_tok = xs.shape[0]
    src, dest, tile_expert, n_used = _moe_routing(route, n_tok)
    xg = _moe_gather(h2, src, n_used)
    ug = _moe_grouped(_moe_up_kernel, xg, (w1, w3), layer, tile_expert, n_used, tn=256, out_dtype=BF16,
                      name="moe_up")
    yg = _moe_grouped(_moe_down_kernel, ug, (w2,), layer, tile_expert, n_used, tn=1024, out_dtype=F32,
                      name="moe_down")
    return _moe_combine(xs, yg, dest, route, gate, tab)


def kernel(x, c, ctx, c_ctx, ada_w, ada_b, norm1_g, norm2_g, w_in, attn_q_norm, attn_k_norm, s5_a_re, s5_a_im, s5_log_step, s5_b_re, s5_b_im, s5_c_re, s5_c_im, s5_d, s5_w_glu, diff_lambda, diff_norm, ret_decay_logit, ret_norm, w_branch, w_merge_gate, w_out, ffn_w1, ffn_w3, ffn_w2, moe_router_w, moe_router_b, moe_w1, moe_w3, moe_w2, final_norm_g):
    n_batch, t_lat, d = x.shape
    t_ctx = ctx.shape[1]
    depth = w_in.shape[0]
    n_lat, n_ctx = n_batch * t_lat, n_batch * t_ctx
    bw = d // 4
    s5_params = dict(s5_a_re=s5_a_re, s5_a_im=s5_a_im, s5_log_step=s5_log_step, s5_b_re=s5_b_re,
                     s5_b_im=s5_b_im, s5_c_re=s5_c_re, s5_c_im=s5_c_im, s5_d=s5_d, s5_w_glu=s5_w_glu)

    tab = RowTable(lambda row: jnp.where(row < n_lat, row // t_lat, n_batch), t_lat)

    cos_t, sin_t = _rope_tables(t_lat, n_batch, t_ctx)
    ada_rows = 8
    c_all = jnp.zeros((ada_rows, d), F32).at[:n_batch].set(c).at[n_batch].set(c_ctx)
    xs = jnp.concatenate([x.reshape(n_lat, d), ctx.reshape(n_ctx, d)])

    for i in range(depth):
        need_ctx = i < depth - 1
        lam_init = 0.8 - 0.6 * math.exp(-0.3 * i)
        mod = _mm(c_all, ada_w, (i,), name="ada_mod", tm=ada_rows, tn=512, epi="bias", x_silu=True,
                  bias=ada_b[i].reshape(1, -1))
        tabs = [mod[:n_batch + 1, k * d:(k + 1) * d].reshape(n_batch + 1, 1, d) for k in range(6)]
        shift1, scale1, gate1, shift2, scale2, gate2 = tabs

        h = _modnorm(xs, norm1_g[i], shift1, scale1, tab)
        p = _mm(h, w_in, (i,), name="in_proj", tm=1024, tn=512)
        qk = _prep(p, cos_t, sin_t, attn_q_norm[i], attn_k_norm[i])

        att = functools.partial(_attention, qk=qk, p=p, n_batch=n_batch, t_lat=t_lat, t_ctx=t_ctx, tq=512)
        gqa = functools.partial(att, _gqa_kernel, name="gqa", heads=2,
                                q_w=4 * HEAD_DIM, q_blk0=0, k_w=HEAD_DIM, k_blk0=8, v_w=HEAD_DIM, v_blk0=10,
                                out_w=4 * HEAD_DIM)
        lp = diff_lambda[i].astype(F32)
        lam = (jnp.exp(jnp.sum(lp[0] * lp[1])) - jnp.exp(jnp.sum(lp[2] * lp[3])) + lam_init).reshape(1)
        dgain = diff_norm[i].reshape(1, 2 * HEAD_DIM)
        dif = functools.partial(att, functools.partial(_diff_kernel, out_scale=1.0 - lam_init),
                                name="diff",
                                heads=4, q_w=2 * HEAD_DIM, q_blk0=5, k_w=2 * HEAD_DIM, k_blk0=9,
                                v_w=2 * HEAD_DIM, v_blk0=18, out_w=2 * HEAD_DIM,
                                lead_in=(lam,), lead_specs=(pl.BlockSpec(memory_space=pltpu.SMEM),),
                                extra_in=(dgain,),
                                extra_specs=(pl.BlockSpec((1, 2 * HEAD_DIM), lambda b, hh, ii: (0, 0)),))
        ya_l, yc_l = gqa(ctx_queries=False), dif(ctx_queries=False)
        log_gamma = jax.nn.log_sigmoid(ret_decay_logit[i].astype(F32))
        yd_l, yd_c = _retention(qk, p, log_gamma, ret_norm[i], n_batch=n_batch, t_lat=t_lat, t_ctx=t_ctx,
                                heads=4, q_blk0=26, k_blk0=30, v_blk0=26, g_blk0=30, dv=2 * HEAD_DIM)
        s5_col0 = 12 * HEAD_DIM
        yb = _s5_branch(p, s5_col0, bw, s5_params, i, n_batch=n_batch, t_lat=t_lat, t_ctx=t_ctx)
        if need_ctx:
            ya = jnp.concatenate([ya_l, gqa(ctx_queries=True)])
            yc = jnp.concatenate([yc_l, dif(ctx_queries=True)])
            yd = jnp.concatenate([yd_l, yd_c])
            rows = n_lat + n_ctx
        else:
            ya, yc, yd = ya_l, yc_l, yd_l
            rows = n_lat
        acc = _merge(h, (ya, yb, yc, yd), w_merge_gate, w_branch, i, m_rows=rows)
        xs = _mm(acc, w_out, (i,), name="out_proj", tm=1024, tn=512, epi="resgate", res=xs, gate=gate1,
                 tab=tab, m_rows=rows)

        if i % 2 == 0:
            j = i // 2
            h2 = _modnorm(xs, norm2_g[i], shift2, scale2, tab)
            u = _swiglu_up(h2, ffn_w1, ffn_w3, (j,))
            half = u.shape[1] // 2
            for kc in range(2):
                xs = _mm(u, ffn_w2, (j,), name="ffn_down", tm=512, tn=512, tk=half, k0=kc, epi="resgate",
                         res=xs, gate=gate2, tab=tab)
        else:
            j = i // 2
            h2, route = _modnorm(xs, norm2_g[i], shift2, scale2, tab,
                                 router_w=moe_router_w[j], router_b=moe_router_b[j])
            xs = _moe_sparse(xs, h2, route, moe_w1, moe_w3, moe_w2, j, gate2, tab)

    return _final_norm(xs[:n_lat], final_norm_g).reshape(n_batch, t_lat, d)
```

```python
import functools
import math
from typing import Callable, NamedTuple

import jax
import jax.numpy as jnp
from jax import lax
from jax.experimental import pallas as pl
from jax.experimental.pallas import tpu as pltpu

F32 = jnp.float32
BF16 = jnp.bfloat16

HEAD_DIM = 128
GRID_W = 64
ROPE_BASE = 10000.0
ROPE_FREQS = HEAD_DIM // 4
EPS = 1e-6
A_GROUP = 4
S5_P = 16
S5_N = 64
S5_L = 8
LANES = 128
RET_CHUNK = 128
N_EXPERTS = 8
ROUTER_LANES = 128
V7X_VMEM_BYTES = 64 * 1024 * 1024
VMEM_HEADROOM_BYTES = 8 * 1024 * 1024


def _vmem_limit(nbytes):
    return int(min(nbytes + VMEM_HEADROOM_BYTES, V7X_VMEM_BYTES - VMEM_HEADROOM_BYTES))


def _nbytes(shape, dtype):
    return math.prod(shape) * jnp.dtype(dtype).itemsize


class RowTable(NamedTuple):
    index_of_row: Callable
    group_rows: int


def _tile(total, preferred):
    return math.gcd(total, preferred)


def _cache_bf16(fresh, w_ref, w_bf16_ref):
    @pl.when(fresh)
    def _():
        w_bf16_ref[...] = w_ref[...].astype(BF16)


def _mm_kernel(*refs, x_silu, epi):
    it = iter(refs)
    x_ref, w_ref = next(it), next(it)
    bias_ref = next(it) if epi == "bias" else None
    res_ref, gate_ref = (next(it), next(it)) if epi == "resgate" else (None, None)
    o_ref, w_bf16_ref = next(it), next(it)

    _cache_bf16(pl.program_id(1) == 0, w_ref, w_bf16_ref)
    x = x_ref[...]
    if x_silu:
        x = x * jax.nn.sigmoid(x)
    acc = jnp.dot(x.astype(BF16), w_bf16_ref[...], preferred_element_type=F32)
    if epi == "bias":
        acc = acc + bias_ref[...]
    elif epi == "resgate":
        acc = res_ref[...] + gate_ref[...] * acc
    o_ref[...] = acc.astype(o_ref.dtype)


def _mm(x, w, wlead=(), *, name, m_rows=None, tm, tn, tk=None, k0=0, epi="none",
        out_dtype=F32, x_silu=False, bias=None, res=None, gate=None, tab=None):
    m_total = x.shape[0] if m_rows is None else m_rows
    n_total = w.shape[-1]
    tk = x.shape[1] if tk is None else tk
    tm, tn = _tile(m_total, tm), _tile(n_total, tn)
    if tab is not None:
        tm = _tile(tab.group_rows, tm)
    nlead = len(wlead)
    in_specs = [
        pl.BlockSpec((tm, tk), lambda n, m: (m, k0)),
        pl.BlockSpec((None,) * nlead + (tk, tn), lambda n, m: tuple(wlead) + (k0, n)),
    ]
    args = [x, w]
    est = 2 * _nbytes((tm, tk), x.dtype) + 2 * _nbytes((tk, tn), w.dtype) + _nbytes((tk, tn), BF16)
    if epi == "bias":
        in_specs.append(pl.BlockSpec((1, tn), lambda n, m: (0, n)))
        args.append(bias)
    elif epi == "resgate":
        in_specs.append(pl.BlockSpec((tm, tn), lambda n, m: (m, n)))
        in_specs.append(pl.BlockSpec((None, 1, tn), lambda n, m: (tab.index_of_row(m * tm), 0, n)))
        args += [res, gate]
        est += 2 * _nbytes((tm, tn), F32)
    est += 2 * _nbytes((tm, tn), out_dtype) + 2 * _nbytes((tm, tn), F32)
    return pl.pallas_call(
        functools.partial(_mm_kernel, x_silu=x_silu, epi=epi),
        grid=(n_total // tn, m_total // tm),
        in_specs=in_specs,
        out_specs=pl.BlockSpec((tm, tn), lambda n, m: (m, n)),
        out_shape=jax.ShapeDtypeStruct((m_total, n_total), out_dtype),
        scratch_shapes=[pltpu.VMEM((tk, tn), BF16)],
        compiler_params=pltpu.CompilerParams(
            dimension_semantics=("parallel", "arbitrary"),
            vmem_limit_bytes=_vmem_limit(est)),
        name=name,
    )(*args)


def _rms(x):
    return x * lax.rsqrt(jnp.mean(x * x, axis=-1, keepdims=True) + EPS)


def _modnorm_kernel(x_ref, g_ref, sh_ref, sc_ref, *rest, router):
    h = _rms(x_ref[...]) * g_ref[...] * (1.0 + sc_ref[...]) + sh_ref[...]
    if not router:
        (o_ref,) = rest
        o_ref[...] = h.astype(o_ref.dtype)
        return
    rw_ref, rb_ref, o_ref, gate_ref = rest
    o_ref[...] = h.astype(o_ref.dtype)
    logits = jnp.dot(h, rw_ref[...], preferred_element_type=F32,
                     precision=lax.Precision.HIGHEST) + rb_ref[...]
    lane = lax.broadcasted_iota(jnp.int32, logits.shape, 1).astype(F32)
    neg = jnp.float32(-jnp.inf)
    logits = jnp.where(lane < N_EXPERTS, logits, neg)
    v1 = jnp.max(logits, axis=-1, keepdims=True)
    i1 = jnp.min(jnp.where(logits == v1, lane, float(ROUTER_LANES)), axis=-1, keepdims=True)
    rest_l = jnp.where(lane == i1, neg, logits)
    v2 = jnp.max(rest_l, axis=-1, keepdims=True)
    i2 = jnp.min(jnp.where(rest_l == v2, lane, float(ROUTER_LANES)), axis=-1, keepdims=True)
    e2 = jnp.exp(v2 - v1)
    w1 = 1.0 / (1.0 + e2)
    w2 = e2 * w1
    routed = jnp.where(lane == i1, w1, jnp.where(lane == i2, w2, 0.0))
    for off, val in enumerate((i1, i2, w1, w2)):
        routed = jnp.where(lane == float(N_EXPERTS + off), val, routed)
    gate_ref[...] = routed


def _modnorm(x, g, shift_tab, scale_tab, tab, *, tm=256, router_w=None, router_b=None):
    m_total, d = x.shape
    tm = _tile(tab.group_rows, _tile(m_total, tm))
    row = pl.BlockSpec((tm, d), lambda m: (m, 0))
    vec = pl.BlockSpec((1, d), lambda m: (0, 0))
    per_group = pl.BlockSpec((None, 1, d), lambda m: (tab.index_of_row(m * tm), 0, 0))
    in_specs, args = [row, vec, per_group, per_group], [x, g.reshape(1, d), shift_tab, scale_tab]
    router = router_w is not None
    out_specs, out_shape = row, jax.ShapeDtypeStruct((m_total, d), F32 if router else BF16)
    est = 2 * _nbytes((tm, d), F32) * 3
    if router:
        rw = jnp.zeros((d, ROUTER_LANES), F32).at[:, :N_EXPERTS].set(router_w)
        rb = jnp.zeros((1, ROUTER_LANES), F32).at[0, :N_EXPERTS].set(router_b)
        in_specs += [pl.BlockSpec((d, ROUTER_LANES), lambda m: (0, 0)),
                     pl.BlockSpec((1, ROUTER_LANES), lambda m: (0, 0))]
        args += [rw, rb]
        out_specs = (row, pl.BlockSpec((tm, ROUTER_LANES), lambda m: (m, 0)))
        out_shape = (out_shape, jax.ShapeDtypeStruct((m_total, ROUTER_LANES), F32))
        est += 2 * _nbytes((d, ROUTER_LANES), F32) * 4
    return pl.pallas_call(
        functools.partial(_modnorm_kernel, router=router),
        grid=(m_total // tm,), in_specs=in_specs, out_specs=out_specs, out_shape=out_shape,
        compiler_params=pltpu.CompilerParams(dimension_semantics=("parallel",),
                                             vmem_limit_bytes=_vmem_limit(est)),
        name="modnorm_router" if router else "modnorm",
    )(*args)


def _final_norm_kernel(x_ref, g_ref, o_ref):
    o_ref[...] = _rms(x_ref[...]) * g_ref[...]


def _final_norm(x, g, *, tm=256):
    m_total, d = x.shape
    tm = _tile(m_total, tm)
    row = pl.BlockSpec((tm, d), lambda m: (m, 0))
    return pl.pallas_call(
        _final_norm_kernel, grid=(m_total // tm,),
        in_specs=[row, pl.BlockSpec((1, d), lambda m: (0, 0))], out_specs=row,
        out_shape=jax.ShapeDtypeStruct((m_total, d), F32),
        compiler_params=pltpu.CompilerParams(dimension_semantics=("parallel",),
                                             vmem_limit_bytes=_vmem_limit(6 * _nbytes((tm, d), F32))),
        name="final_norm",
    )(x, g.reshape(1, d))


class PrepHead(NamedTuple):
    src_col: int
    norm: str
    scale: float


PREP_SRC_W = 512
ATTN_SCALE = HEAD_DIM ** -0.5


def _prep_plan():
    h = HEAD_DIM
    plan = [PrepHead(j * h, "q", ATTN_SCALE) for j in range(8)]
    plan += [PrepHead((8 + j) * h, "k", 1.0) for j in range(2)]
    plan += [PrepHead((20 + j) * h, "", ATTN_SCALE) for j in range(8)]
    plan += [PrepHead((28 + j) * h, "", 1.0) for j in range(8)]
    plan += [PrepHead((44 + j) * h, "", 1.0) for j in range(4)]
    plan += [PrepHead((48 + j) * h, "", ATTN_SCALE) for j in range(4)]
    return plan


def _prep_kernel(*refs, plan, src_blocks):
    src_refs = dict(zip(src_blocks, refs[:len(src_blocks)]))
    cos_ref, sin_ref, qn_ref, kn_ref, o_ref = refs[len(src_blocks):]
    cos, sin = cos_ref[...], sin_ref[...]
    lane = lax.broadcasted_iota(jnp.int32, cos.shape, 1)
    first_half = (lane % (2 * ROPE_FREQS)) < ROPE_FREQS
    gains = {"q": qn_ref[...], "k": kn_ref[...]}
    for j, head in enumerate(plan):
        blk, off = divmod(head.src_col, PREP_SRC_W)
        y = src_refs[blk][:, off:off + HEAD_DIM]
        if head.norm:
            y = _rms(y) * gains[head.norm]
        if head.scale != 1.0:
            y = y * head.scale
        partner = jnp.where(first_half, pltpu.roll(y, HEAD_DIM - ROPE_FREQS, 1), pltpu.roll(y, ROPE_FREQS, 1))
        o_ref[:, j * HEAD_DIM:(j + 1) * HEAD_DIM] = (y * cos + partner * sin).astype(o_ref.dtype)


def _prep(p, cos_t, sin_t, q_gain, k_gain, *, tm=256):
    m_total = p.shape[0]
    tm = _tile(m_total, tm)
    plan = _prep_plan()
    src_blocks = sorted({head.src_col // PREP_SRC_W for head in plan})
    row = lambda w, blk: pl.BlockSpec((tm, w), lambda m: (m, blk))
    vec = pl.BlockSpec((1, HEAD_DIM), lambda m: (0, 0))
    out_w = len(plan) * HEAD_DIM
    est = 2 * len(src_blocks) * _nbytes((tm, PREP_SRC_W), F32) + 2 * _nbytes((tm, out_w), BF16)
    return pl.pallas_call(
        functools.partial(_prep_kernel, plan=plan, src_blocks=src_blocks),
        grid=(m_total // tm,),
        in_specs=[row(PREP_SRC_W, blk) for blk in src_blocks] + [row(HEAD_DIM, 0), row(HEAD_DIM, 0), vec, vec],
        out_specs=row(out_w, 0),
        out_shape=jax.ShapeDtypeStruct((m_total, out_w), BF16),
        compiler_params=pltpu.CompilerParams(dimension_semantics=("parallel",),
                                             vmem_limit_bytes=_vmem_limit(est)),
        name="qk_prep",
    )(*([p] * len(src_blocks)), cos_t, sin_t, q_gain.reshape(1, HEAD_DIM), k_gain.reshape(1, HEAD_DIM))


def _rope_tables(t_lat, n_batch, t_ctx):
    rows = t_lat // GRID_W
    row = jnp.broadcast_to(jnp.arange(rows)[:, None], (rows, GRID_W)).reshape(-1)
    col = jnp.broadcast_to(jnp.arange(GRID_W)[None, :], (rows, GRID_W)).reshape(-1)
    inv = ROPE_BASE ** (-jnp.arange(ROPE_FREQS, dtype=F32) / ROPE_FREQS)
    ang = jnp.stack([row, col], axis=-1).astype(F32)[:, :, None] * inv
    cos, sin = jnp.cos(ang), jnp.sin(ang)
    cos128 = jnp.concatenate([cos, cos], axis=-1).reshape(t_lat, HEAD_DIM)
    sin128 = jnp.concatenate([-sin, sin], axis=-1).reshape(t_lat, HEAD_DIM)
    cos_t = jnp.concatenate([jnp.tile(cos128, (n_batch, 1)), jnp.ones((n_batch * t_ctx, HEAD_DIM), F32)])
    sin_t = jnp.concatenate([jnp.tile(sin128, (n_batch, 1)), jnp.zeros((n_batch * t_ctx, HEAD_DIM), F32)])
    return cos_t, sin_t


_NT = (((1,), (1,)), ((), ()))


def _softmax_parts(q, ks):
    ss = [lax.dot_general(q, k, _NT, preferred_element_type=F32) for k in ks]
    m = functools.reduce(jnp.maximum, [jnp.max(s, axis=-1, keepdims=True) for s in ss])
    es = [jnp.exp(s - m) for s in ss]
    l = functools.reduce(jnp.add, [jnp.sum(e, axis=-1, keepdims=True) for e in es])
    return es, l


def _gqa_kernel(q_ref, *rest, nseg):
    k_refs, v_refs, o_ref = rest[:nseg], rest[nseg:2 * nseg], rest[2 * nseg]
    ks = [r[...] for r in k_refs]
    vs = [r[...].astype(BF16) for r in v_refs]
    for g in range(A_GROUP):
        cols = slice(g * HEAD_DIM, (g + 1) * HEAD_DIM)
        es, l = _softmax_parts(q_ref[:, cols], ks)
        o = functools.reduce(jnp.add, [jnp.dot(e.astype(BF16), v, preferred_element_type=F32)
                                       for e, v in zip(es, vs)])
        o_ref[:, cols] = (o / l).astype(o_ref.dtype)


def _diff_kernel(lam_ref, q_ref, *rest, nseg, out_scale):
    k_refs, v_refs = rest[:nseg], rest[nseg:2 * nseg]
    gain_ref, o_ref = rest[2 * nseg], rest[2 * nseg + 1]
    lam = lam_ref[0]
    vs = [r[...].astype(BF16) for r in v_refs]
    parts = []
    for m in range(2):
        cols = slice(m * HEAD_DIM, (m + 1) * HEAD_DIM)
        es, l = _softmax_parts(q_ref[:, cols], [r[:, cols] for r in k_refs])
        parts.append((es, 1.0 / l))
    (e0, r0), (e1, r1) = parts
    o = functools.reduce(jnp.add, [
        jnp.dot((a * r0 - lam * (b * r1)).astype(BF16), v, preferred_element_type=F32)
        for a, b, v in zip(e0, e1, vs)])
    o_ref[...] = (_rms(o) * gain_ref[...] * out_scale).astype(o_ref.dtype)


def _attention(kernel, qk, p, *, name, n_batch, t_lat, t_ctx, heads, q_w, q_blk0, k_w, k_blk0, v_w, v_blk0,
               out_w, ctx_queries, tq, extra_in=(), extra_specs=(), lead_in=(), lead_specs=()):
    n_lat = n_batch * t_lat
    ctx_row0 = n_lat // t_ctx
    tq = _tile(t_lat, tq)
    if ctx_queries:
        nq, q_rows = 1, t_ctx
        q_map = lambda b, h, i: (ctx_row0 + b, q_blk0 + h)
        out_rows, o_map = n_batch * t_ctx, (lambda b, h, i: (b, h))
        segs = [(t_ctx, lambda b: ctx_row0 + b)]
    else:
        nq, q_rows = t_lat // tq, tq
        q_map = lambda b, h, i: (b * nq + i, q_blk0 + h)
        out_rows, o_map = n_lat, (lambda b, h, i: (b * nq + i, h))
        segs = [(t_ctx, lambda b: ctx_row0 + b), (t_lat, lambda b: b)]
    k_specs = [pl.BlockSpec((rows, k_w), functools.partial(lambda rf, b, h, i: (rf(b), k_blk0 + h), rf))
               for rows, rf in segs]
    v_specs = [pl.BlockSpec((rows, v_w), functools.partial(lambda rf, b, h, i: (rf(b), v_blk0 + h), rf))
               for rows, rf in segs]
    nseg = len(segs)
    est = 8 * _nbytes((q_rows, t_lat + t_ctx), F32) + 4 * _nbytes((t_lat + t_ctx, v_w), F32)
    return pl.pallas_call(
        functools.partial(kernel, nseg=nseg),
        grid=(n_batch, heads, nq),
        in_specs=list(lead_specs) + [pl.BlockSpec((q_rows, q_w), q_map)] + k_specs + v_specs + list(extra_specs),
        out_specs=pl.BlockSpec((q_rows, out_w), o_map),
        out_shape=jax.ShapeDtypeStruct((out_rows, heads * out_w), BF16),
        compiler_params=pltpu.CompilerParams(dimension_semantics=("parallel", "parallel", "arbitrary"),
                                             vmem_limit_bytes=_vmem_limit(est)),
        name=name + ("_ctx" if ctx_queries else "_lat"),
    )(*lead_in, qk, *([qk] * nseg), *([p] * nseg), *extra_in)


def _ret_kernel(lg_ref, ql_ref, qc_ref, kl_ref, kc_ref, vl_ref, vc_ref, gl_ref, gc_ref, gain_ref,
                yl_ref, yc_ref, ol_scr, oc_scr, s_scr, *, t_lat, t_ctx):
    h = pl.program_id(1)
    lg_f, lg_b = lg_ref[0, h], lg_ref[1, h]
    c = RET_CHUNK
    pos_i = lax.broadcasted_iota(jnp.int32, (c, c), 0).astype(F32)
    pos_j = lax.broadcasted_iota(jnp.int32, (c, c), 1).astype(F32)
    rel = pos_i - pos_j
    d_both = (jnp.where(rel >= 0, jnp.exp(jnp.maximum(rel, 0.0) * lg_f), 0.0)
              + jnp.where(rel <= 0, jnp.exp(jnp.maximum(-rel, 0.0) * lg_b), 0.0))
    pos = lax.broadcasted_iota(jnp.int32, (c, 1), 0).astype(F32)
    qdec = (jnp.exp((pos + 1.0) * lg_f), jnp.exp((c - pos) * lg_b))
    kdec = (jnp.exp((c - 1.0 - pos) * lg_f), jnp.exp(pos * lg_b))
    one = jnp.ones((1, 1), F32)
    sdec = (jnp.exp(one * (c * lg_f)), jnp.exp(one * (c * lg_b)))

    chunks = [(qc_ref, kc_ref, vc_ref, oc_scr, i) for i in range(t_ctx // c)]
    chunks += [(ql_ref, kl_ref, vl_ref, ol_scr, i) for i in range(t_lat // c)]
    back = ([ch for ch in chunks if ch[0] is qc_ref][::-1] + [ch for ch in chunks if ch[0] is ql_ref][::-1])

    for d, order in ((0, chunks), (1, back)):
        s_scr[...] = jnp.zeros_like(s_scr)
        for q_ref, k_ref, v_ref, o_scr, i in order:
            rows = pl.ds(i * c, c)
            q, k = q_ref[rows, :], k_ref[rows, :]
            v = v_ref[rows, :].astype(BF16)
            state = s_scr[...]
            cross = jnp.dot((q.astype(F32) * qdec[d]).astype(BF16), state.astype(BF16),
                            preferred_element_type=F32)
            if d == 0:
                scores = lax.dot_general(q, k, _NT, preferred_element_type=F32) * d_both
                o_scr[rows, :] = cross + jnp.dot(scores.astype(BF16), v, preferred_element_type=F32)
            else:
                o_scr[rows, :] += cross
            kv = jnp.dot((k.astype(F32) * kdec[d]).T.astype(BF16), v, preferred_element_type=F32)
            s_scr[...] = sdec[d] * state + kv

    gain = gain_ref[...]
    for o_scr, g_ref, y_ref in ((ol_scr, gl_ref, yl_ref), (oc_scr, gc_ref, yc_ref)):
        g = g_ref[...]
        y_ref[...] = (_rms(o_scr[...]) * gain * (g * jax.nn.sigmoid(g))).astype(y_ref.dtype)


def _retention(qk, p, log_gamma, gain, *, n_batch, t_lat, t_ctx, heads, q_blk0, k_blk0, v_blk0, g_blk0, dv):
    n_lat = n_batch * t_lat
    ctx_row0 = n_lat // t_ctx
    lat = lambda w, blk0: pl.BlockSpec((t_lat, w), lambda b, h: (b, blk0 + h))
    ctx = lambda w, blk0: pl.BlockSpec((t_ctx, w), lambda b, h: (ctx_row0 + b, blk0 + h))
    est = 6 * _nbytes((t_lat + t_ctx, dv), F32) * 2
    return pl.pallas_call(
        functools.partial(_ret_kernel, t_lat=t_lat, t_ctx=t_ctx),
        grid=(n_batch, heads),
        in_specs=[pl.BlockSpec(memory_space=pltpu.SMEM),
                  lat(HEAD_DIM, q_blk0), ctx(HEAD_DIM, q_blk0), lat(HEAD_DIM, k_blk0), ctx(HEAD_DIM, k_blk0),
                  lat(dv, v_blk0), ctx(dv, v_blk0), lat(dv, g_blk0), ctx(dv, g_blk0),
                  pl.BlockSpec((1, dv), lambda b, h: (0, 0))],
        out_specs=(pl.BlockSpec((t_lat, dv), lambda b, h: (b, h)),
                   pl.BlockSpec((t_ctx, dv), lambda b, h: (b, h))),
        out_shape=(jax.ShapeDtypeStruct((n_lat, heads * dv), BF16),
                   jax.ShapeDtypeStruct((n_batch * t_ctx, heads * dv), BF16)),
        scratch_shapes=[pltpu.VMEM((t_lat, dv), F32), pltpu.VMEM((t_ctx, dv), F32),
                        pltpu.VMEM((HEAD_DIM, dv), F32)],
        compiler_params=pltpu.CompilerParams(dimension_semantics=("parallel", "arbitrary"),
                                             vmem_limit_bytes=_vmem_limit(est)),
        name="retention",
    )(log_gamma, qk, qk, qk, qk, p, p, p, p, gain.reshape(1, dv))


def _s5_tables(a_re, a_im, log_step, b_re, b_im, c_re, c_im):
    hp = lax.Precision.HIGHEST
    n_dir, n_grp, n_st = a_re.shape
    ln = S5_L
    tg = LANES // S5_P
    n_tile = n_grp // tg
    lam = lax.complex(a_re.astype(F32), a_im.astype(F32))
    step = jnp.exp(log_step.astype(F32))[..., None]
    a_bar = jnp.exp(lam * step)
    b_bar = ((a_bar - 1.0) / lam)[..., None] * lax.complex(b_re.astype(F32), b_im.astype(F32))
    c_mat = lax.complex(c_re.astype(F32), c_im.astype(F32))
    taus = jnp.arange(ln + 1, dtype=F32)
    apow = jnp.exp((lam * step)[..., None] * taus)

    kern = jnp.einsum("dgpn,dgnt,dgnq->dgtpq", c_mat, apow[..., :ln], b_bar, precision=hp).real
    ti = jnp.arange(ln)[:, None]
    tj = jnp.arange(ln)[None, :]
    t_f = kern[0][:, jnp.clip(ti - tj, 0)] * (ti >= tj)[None, :, :, None, None]
    t_b = kern[1][:, jnp.clip(tj - ti, 0)] * (tj >= ti)[None, :, :, None, None]
    def widen(compact, n_x, n_y, row_group):
        k_in, k_out = n_x * n_y, n_x * tg * n_y
        r, c = jnp.arange(k_in)[:, None], jnp.arange(k_out)[None, :]
        spread = ((r // n_y == c // (tg * n_y)) & (r % n_y == c % n_y)).astype(BF16)
        own = row_group[:, None] == (jnp.arange(k_out) % (tg * n_y) // n_y)[None, :]
        wide = jnp.einsum("...rk,kc->...rc", compact.astype(BF16), spread, preferred_element_type=F32)
        return (wide * own).astype(BF16)

    lane_group = jnp.arange(LANES) // S5_P
    state_group = jnp.arange(2 * tg * n_st) % (tg * n_st) // n_st

    toep = (t_f + t_b).reshape(n_tile, tg, ln, ln, S5_P, S5_P)
    m_intra = widen(toep.transpose(0, 3, 1, 5, 2, 4).reshape(n_tile, ln, LANES, ln * S5_P),
                    ln, S5_P, lane_group)

    def in_mat(d, tau_of_j):
        w = apow[d][:, :, tau_of_j][..., None] * b_bar[d][:, :, None, :]
        wri = jnp.stack([w.real, w.imag]).reshape(2, n_tile, tg, n_st, ln, S5_P)
        return widen(wri.transpose(1, 4, 2, 5, 0, 3).reshape(n_tile, ln, LANES, 2 * n_st), 2, n_st, lane_group)

    def out_mat(d, tau_of_i):
        w = c_mat[d][:, :, :, None] * apow[d][:, None, :, tau_of_i]
        wri = jnp.stack([w.real, -w.imag]).reshape(2, n_tile, tg, S5_P, n_st, ln)
        return widen(wri.transpose(1, 0, 2, 4, 5, 3).reshape(n_tile, 2 * tg * n_st, ln * S5_P),
                     ln, S5_P, state_group)

    idx = jnp.arange(ln)
    fold = lambda m: m.reshape(n_tile, ln * LANES, m.shape[-1])
    mats = [fold(m_intra), fold(in_mat(0, ln - 1 - idx)), fold(in_mat(1, idx)),
            out_mat(0, idx + 1), out_mat(1, ln - idx)]
    dec = [apow[0][..., ln].real, apow[0][..., ln].imag, apow[1][..., ln].real, apow[1][..., ln].imag]
    return mats, [x.reshape(n_tile, 1, tg * n_st) for x in dec]


def _s5_kernel(u_ref, m_ref, bcf_ref, bcb_ref, ccf_ref, ccb_ref, dfr_ref, dfi_ref, dbr_ref, dbi_ref,
               y_ref, vf, vb, *, n_batch, t_lat, t_ctx):
    ln = S5_L
    n_lat = n_batch * t_lat
    n_cr, n_lr = t_ctx // ln, t_lat // ln
    n_st = vf.shape[0]
    n_re = n_st // 2

    def chunk_tokens(b, j):
        ctx_rows = u_ref[pl.ds(n_lat + b * t_ctx + j, n_cr, stride=ln), :]
        lat_rows = u_ref[pl.ds(b * t_lat + j, n_lr, stride=ln), :]
        return jnp.concatenate([ctx_rows, lat_rows], axis=0).astype(BF16)

    def chunks(b):
        return jnp.concatenate([chunk_tokens(b, j) for j in range(ln)], axis=1)

    def mix(uc, w_ref):
        return jnp.dot(uc, w_ref[...], preferred_element_type=F32)

    def batch_rows(b):
        return pl.ds(b, n_cr + n_lr, stride=n_batch)

    for b in range(n_batch):
        uc = chunks(b)
        for v, w_ref in ((vf, bcf_ref), (vb, bcb_ref)):
            contrib = mix(uc, w_ref)
            for c in range(n_st):
                v[c, batch_rows(b), :] = contrib[:, c * LANES:(c + 1) * LANES]

    tile_rows = 8
    tile_chunks = tile_rows // n_batch
    n_ct, n_tiles = n_cr // tile_chunks, (n_cr + n_lr) // tile_chunks

    def scan(v, d_re, d_im, tile_of_step, backward):
        order = range(tile_chunks - 1, -1, -1) if backward else range(tile_chunks)

        def step(s, carry):
            rows = pl.ds(pl.multiple_of(tile_of_step(s) * tile_rows, tile_rows), tile_rows)
            new_carry = []
            for c in range(n_re):
                s_re, s_im = carry[2 * c], carry[2 * c + 1]
                t_re, t_im = v[c, rows, :], v[n_re + c, rows, :]
                dr, di = d_re[:, c * LANES:(c + 1) * LANES], d_im[:, c * LANES:(c + 1) * LANES]
                o_re, o_im = [None] * tile_chunks, [None] * tile_chunks
                for k in order:
                    sub = slice(k * n_batch, (k + 1) * n_batch)
                    o_re[k], o_im[k] = s_re, s_im
                    s_re, s_im = dr * s_re - di * s_im + t_re[sub], dr * s_im + di * s_re + t_im[sub]
                v[c, rows, :] = jnp.concatenate(o_re, axis=0)
                v[n_re + c, rows, :] = jnp.concatenate(o_im, axis=0)
                new_carry += [s_re, s_im]
            return tuple(new_carry)

        zero = jnp.zeros((n_batch, LANES), F32)
        lax.fori_loop(0, n_tiles, step, (zero,) * n_st)

    scan(vf, dfr_ref[...], dfi_ref[...], lambda s: s, False)
    scan(vb, dbr_ref[...], dbi_ref[...],
         lambda s: jnp.where(s < n_ct, n_ct - 1 - s, n_tiles + n_ct - 1 - s), True)

    def entering_state(v, b):
        return jnp.concatenate([v[c, batch_rows(b), :] for c in range(n_st)], axis=1).astype(BF16)

    for b in range(n_batch):
        y = (mix(chunks(b), m_ref)
             + jnp.dot(entering_state(vf, b), ccf_ref[...], preferred_element_type=F32)
             + jnp.dot(entering_state(vb, b), ccb_ref[...], preferred_element_type=F32))
        for i in range(ln):
            lanes = slice(i * LANES, (i + 1) * LANES)
            y_ref[pl.ds(n_lat + b * t_ctx + i, n_cr, stride=ln), :] = y[:n_cr, lanes]
            y_ref[pl.ds(b * t_lat + i, n_lr, stride=ln), :] = y[n_cr:, lanes]


def _s5_core(p, col_blk0, mats, decs, *, n_batch, t_lat, t_ctx):
    rows = p.shape[0]
    n_tile = mats[0].shape[0]
    n_chunk = (t_lat + t_ctx) // S5_L
    assert 8 % n_batch == 0 and (t_ctx // S5_L) % (8 // n_batch) == 0 and (t_lat // S5_L) % (8 // n_batch) == 0
    state = mats[1].shape[-1]
    mat_specs = [pl.BlockSpec((None,) + m.shape[1:], lambda t, nd=m.ndim: (t,) + (0,) * (nd - 1)) for m in mats]
    dec_specs = [pl.BlockSpec((None, 1, state // 2), lambda t: (t, 0, 0)) for _ in decs]
    est = (4 * _nbytes((rows, LANES), F32) + 2 * sum(_nbytes(m.shape[1:], BF16) for m in mats)
           + 2 * _nbytes((n_chunk * n_batch, state), F32) + 6 * _nbytes((n_chunk, state), F32))
    return pl.pallas_call(
        functools.partial(_s5_kernel, n_batch=n_batch, t_lat=t_lat, t_ctx=t_ctx),
        grid=(n_tile,),
        in_specs=[pl.BlockSpec((rows, LANES), lambda t: (0, col_blk0 + t))] + mat_specs + dec_specs,
        out_specs=pl.BlockSpec((rows, LANES), lambda t: (0, t)),
        out_shape=jax.ShapeDtypeStruct((rows, n_tile * LANES), F32),
        scratch_shapes=[pltpu.VMEM((state // LANES, n_chunk * n_batch, LANES), F32) for _ in range(2)],
        compiler_params=pltpu.CompilerParams(dimension_semantics=("arbitrary",),
                                             vmem_limit_bytes=_vmem_limit(est)),
        name="s5_core",
    )(p, *mats, *decs)


def _s5_finish_kernel(y_ref, ua_ref, ub_ref, d_ref, w_ref, o_ref):
    u = jnp.concatenate([ua_ref[...], ub_ref[...]], axis=1)
    z = jax.nn.gelu(y_ref[...] + d_ref[...] * u)
    gl = jnp.dot(z.astype(BF16), w_ref[...].astype(BF16), preferred_element_type=F32)
    o_ref[...] = (z * jax.nn.sigmoid(gl)).astype(o_ref.dtype)


def _s5_finish(y, p, col_half0, d_skip, w_glu, wlead, *, tm=512):
    m_total, w = y.shape
    tm = _tile(m_total, tm)
    row = pl.BlockSpec((tm, w), lambda m: (m, 0))
    nlead = len(wlead)
    return pl.pallas_call(
        _s5_finish_kernel, grid=(m_total // tm,),
        in_specs=[row, pl.BlockSpec((tm, w // 2), lambda m: (m, col_half0)),
                  pl.BlockSpec((tm, w // 2), lambda m: (m, col_half0 + 1)),
                  pl.BlockSpec((1, w), lambda m: (0, 0)),
                  pl.BlockSpec((None,) * nlead + (w, w), lambda m: tuple(wlead) + (0, 0))],
        out_specs=row, out_shape=jax.ShapeDtypeStruct((m_total, w), BF16),
        compiler_params=pltpu.CompilerParams(
            dimension_semantics=("parallel",),
            vmem_limit_bytes=_vmem_limit(8 * _nbytes((tm, w), F32) + 3 * _nbytes((w, w), F32))),
        name="s5_finish",
    )(y, p, p, d_skip.reshape(1, w), w_glu)


def _s5_branch(p, col0, bw, params, layer, *, n_batch, t_lat, t_ctx):
    mats, decs = _s5_tables(*(params[k][layer] for k in
                              ("s5_a_re", "s5_a_im", "s5_log_step", "s5_b_re", "s5_b_im", "s5_c_re", "s5_c_im")))
    y = _s5_core(p, col0 // LANES, mats, decs, n_batch=n_batch, t_lat=t_lat, t_ctx=t_ctx)
    return _s5_finish(y, p, col0 // (bw // 2), params["s5_d"][layer], params["s5_w_glu"], (layer,))


def _merge_kernel(h_ref, *rest, n_branch):
    y_refs = rest[:n_branch]
    wg_ref, wb_ref, o_ref, acc_ref, val_ref = rest[n_branch:]
    j = pl.program_id(2)
    wb = wb_ref[...].astype(BF16)
    for idx, y_ref in enumerate(y_refs):
        @pl.when(j == idx)
        def _(y_ref=y_ref):
            val_ref[...] = jnp.dot(y_ref[...], wb, preferred_element_type=F32)

    gate = jnp.dot(h_ref[...], wg_ref[...].astype(BF16), preferred_element_type=F32)
    term = jax.nn.sigmoid(gate) * val_ref[...]

    @pl.when(j == 0)
    def _():
        acc_ref[...] = term

    @pl.when(j > 0)
    def _():
        acc_ref[...] += term

    @pl.when(j == n_branch - 1)
    def _():
        o_ref[...] = acc_ref[...].astype(o_ref.dtype)


def _merge(h, ys, w_gate, w_branch, layer, *, m_rows, tm=1024, tn=256):
    d = h.shape[1]
    tm = _tile(m_rows, tm)
    n_branch, bw = w_branch.shape[1], w_branch.shape[2]
    assert len(ys) == n_branch
    est = (2 * _nbytes((tm, d), BF16) + 2 * n_branch * _nbytes((tm, bw), BF16) + 3 * _nbytes((d, tn), F32)
           + 3 * _nbytes((bw, tn), F32) + 6 * _nbytes((tm, tn), F32))
    return pl.pallas_call(
        functools.partial(_merge_kernel, n_branch=n_branch),
        grid=(m_rows // tm, d // tn, n_branch),
        in_specs=[pl.BlockSpec((tm, d), lambda m, n, j: (m, 0))]
                 + [pl.BlockSpec((tm, bw), lambda m, n, j: (m, 0))] * n_branch
                 + [pl.BlockSpec((None, None, d, tn), lambda m, n, j: (layer, j, 0, n)),
                    pl.BlockSpec((None, None, bw, tn), lambda m, n, j: (layer, j, 0, n))],
        out_specs=pl.BlockSpec((tm, tn), lambda m, n, j: (m, n)),
        out_shape=jax.ShapeDtypeStruct((m_rows, d), BF16),
        scratch_shapes=[pltpu.VMEM((tm, tn), F32), pltpu.VMEM((tm, tn), F32)],
        compiler_params=pltpu.CompilerParams(dimension_semantics=("parallel", "parallel", "arbitrary"),
                                             vmem_limit_bytes=_vmem_limit(est)),
        name="merge",
    )(h, *ys, w_gate, w_branch)


def _swiglu_tile(x, w1_bf16_ref, w3_bf16_ref):
    a = jnp.dot(x, w1_bf16_ref[...], preferred_element_type=F32)
    b = jnp.dot(x, w3_bf16_ref[...], preferred_element_type=F32)
    return a * jax.nn.sigmoid(a) * b


def _swiglu_up_kernel(x_ref, w1_ref, w3_ref, o_ref, w1_bf16_ref, w3_bf16_ref):
    fresh = pl.program_id(1) == 0
    _cache_bf16(fresh, w1_ref, w1_bf16_ref)
    _cache_bf16(fresh, w3_ref, w3_bf16_ref)
    o_ref[...] = _swiglu_tile(x_ref[...], w1_bf16_ref, w3_bf16_ref).astype(o_ref.dtype)


def _swiglu_up(x, w1, w3, wlead, *, tm=1024, tn=256):
    m_total, d = x.shape
    tm = _tile(m_total, tm)
    ff = w1.shape[-1]
    nlead = len(wlead)
    w_spec = pl.BlockSpec((None,) * nlead + (d, tn), lambda n, m: tuple(wlead) + (0, n))
    est = 2 * _nbytes((tm, d), BF16) + 6 * _nbytes((d, tn), F32) + 6 * _nbytes((tm, tn), F32)
    return pl.pallas_call(
        _swiglu_up_kernel,
        grid=(ff // tn, m_total // tm),
        in_specs=[pl.BlockSpec((tm, d), lambda n, m: (m, 0)), w_spec, w_spec],
        out_specs=pl.BlockSpec((tm, tn), lambda n, m: (m, n)),
        out_shape=jax.ShapeDtypeStruct((m_total, ff), BF16),
        scratch_shapes=[pltpu.VMEM((d, tn), BF16), pltpu.VMEM((d, tn), BF16)],
        compiler_params=pltpu.CompilerParams(dimension_semantics=("parallel", "arbitrary"),
                                             vmem_limit_bytes=_vmem_limit(est)),
        name="ffn_up",
    )(x, w1, w3)


MOE_TOP_K = 2
MOE_ROW_TILE = 512
MOE_GATHER_ROWS = 256
MOE_COMBINE_ROWS = 128
MOE_DMA_UNROLL = 8


def _moe_routing(gates, n_tok):
    tile = MOE_ROW_TILE
    n_rows = MOE_TOP_K * n_tok + N_EXPERTS * tile
    experts = jnp.concatenate([gates[:, N_EXPERTS + k] for k in range(MOE_TOP_K)]).astype(jnp.int32)
    onehot = (experts[:, None] == jnp.arange(N_EXPERTS, dtype=jnp.int32)[None, :]).astype(jnp.int32)
    before = jnp.cumsum(onehot, axis=0) - onehot
    counts = jnp.sum(onehot, axis=0)
    padded = ((counts + tile - 1) // tile) * tile
    ends = jnp.cumsum(padded)
    dest = jnp.sum(onehot * (before + (ends - padded)[None, :]), axis=1)
    tokens = jnp.tile(jnp.arange(n_tok, dtype=jnp.int32), MOE_TOP_K)
    src = jnp.zeros((n_rows,), jnp.int32).at[dest].set(tokens)
    tile_start = jnp.arange(n_rows // tile, dtype=jnp.int32) * tile
    tile_expert = jnp.minimum(jnp.sum((tile_start[:, None] >= ends[None, :]).astype(jnp.int32), axis=1),
                              N_EXPERTS - 1)
    return src, dest, tile_expert, (ends[-1:] // tile).astype(jnp.int32)


def _row_copy(src_hbm, row, dst_vmem, slot, sem):
    return pltpu.make_async_copy(src_hbm.at[pl.ds(row, 1)], dst_vmem.at[pl.ds(slot, 1)], sem)


def _moe_gather_kernel(src_ref, nused_ref, x_hbm, o_ref, buf, sem, *, rows, steps_per_tile):
    step = pl.program_id(0)
    used = step < nused_ref[0] * steps_per_tile

    @pl.when(used)
    def _():
        base = step * rows

        def start(r, carry):
            _row_copy(x_hbm, src_ref[base + r], buf, r, sem).start()
            return carry

        def wait(r, carry):
            _row_copy(x_hbm, 0, buf, r, sem).wait()
            return carry

        lax.fori_loop(0, rows, start, 0, unroll=MOE_DMA_UNROLL)
        lax.fori_loop(0, rows, wait, 0, unroll=MOE_DMA_UNROLL)
        o_ref[...] = buf[...].astype(o_ref.dtype)

    @pl.when(jnp.logical_not(used))
    def _():
        o_ref[...] = jnp.zeros_like(o_ref)


def _moe_gather(x, src, n_used):
    n_rows, d = src.shape[0], x.shape[1]
    rows = MOE_GATHER_ROWS
    return pl.pallas_call(
        functools.partial(_moe_gather_kernel, rows=rows, steps_per_tile=MOE_ROW_TILE // rows),
        grid_spec=pltpu.PrefetchScalarGridSpec(
            num_scalar_prefetch=2, grid=(n_rows // rows,),
            in_specs=[pl.BlockSpec(memory_space=pl.ANY)],
            out_specs=pl.BlockSpec((rows, d), lambda i, s, nu: (i, 0)),
            scratch_shapes=[pltpu.VMEM((rows, d), F32), pltpu.SemaphoreType.DMA]),
        out_shape=jax.ShapeDtypeStruct((n_rows, d), BF16),
        compiler_params=pltpu.CompilerParams(
            dimension_semantics=("arbitrary",),
            vmem_limit_bytes=_vmem_limit(_nbytes((rows, d), F32) + 2 * _nbytes((rows, d), BF16))),
        name="moe_gather",
    )(src, n_used, x)


def _moe_tile_state(te_ref, nused_ref):
    m = pl.program_id(1)
    fresh = jnp.logical_or(m == 0, te_ref[m] != te_ref[jnp.maximum(m - 1, 0)])
    return m < nused_ref[0], fresh


def _moe_up_kernel(te_ref, nused_ref, x_ref, w1_ref, w3_ref, o_ref, w1_bf16_ref, w3_bf16_ref):
    used, fresh = _moe_tile_state(te_ref, nused_ref)

    @pl.when(used)
    def _():
        _cache_bf16(fresh, w1_ref, w1_bf16_ref)
        _cache_bf16(fresh, w3_ref, w3_bf16_ref)
        o_ref[...] = _swiglu_tile(x_ref[...], w1_bf16_ref, w3_bf16_ref).astype(o_ref.dtype)

    @pl.when(jnp.logical_not(used))
    def _():
        o_ref[...] = jnp.zeros_like(o_ref)


def _moe_down_kernel(te_ref, nused_ref, u_ref, w2_ref, o_ref, w2_bf16_ref):
    used, fresh = _moe_tile_state(te_ref, nused_ref)

    @pl.when(used)
    def _():
        _cache_bf16(fresh, w2_ref, w2_bf16_ref)
        o_ref[...] = jnp.dot(u_ref[...], w2_bf16_ref[...], preferred_element_type=F32)

    @pl.when(jnp.logical_not(used))
    def _():
        o_ref[...] = jnp.zeros_like(o_ref)


def _moe_grouped(kernel, x, ws, layer, tile_expert, n_used, *, tn, out_dtype, name):
    n_rows, kdim = x.shape
    n_out = ws[0].shape[-1]
    tm = MOE_ROW_TILE

    def row_blk(m, nu):
        return jnp.minimum(m, nu[0] - 1)

    w_spec = pl.BlockSpec((None, None, kdim, tn), lambda n, m, te, nu: (layer, te[row_blk(m, nu)], 0, n))
    est = (2 * _nbytes((tm, kdim), BF16) + len(ws) * 3 * _nbytes((kdim, tn), F32)
           + (2 + len(ws)) * _nbytes((tm, tn), F32))
    return pl.pallas_call(
        kernel,
        grid_spec=pltpu.PrefetchScalarGridSpec(
            num_scalar_prefetch=2, grid=(n_out // tn, n_rows // tm),
            in_specs=[pl.BlockSpec((tm, kdim), lambda n, m, te, nu: (row_blk(m, nu), 0))] + [w_spec] * len(ws),
            out_specs=pl.BlockSpec((tm, tn), lambda n, m, te, nu: (m, n)),
            scratch_shapes=[pltpu.VMEM((kdim, tn), BF16) for _ in ws]),
        out_shape=jax.ShapeDtypeStruct((n_rows, n_out), out_dtype),
        compiler_params=pltpu.CompilerParams(dimension_semantics=("parallel", "arbitrary"),
                                             vmem_limit_bytes=_vmem_limit(est)),
        name=name,
    )(tile_expert, n_used, x, *ws)


def _moe_combine_kernel(dest_ref, y_hbm, xs_ref, route_ref, gate_ref, o_ref, buf, sem, *, rows, n_tok):
    base = pl.program_id(0) * rows
    for k in range(MOE_TOP_K):
        def start(r, carry, k=k):
            _row_copy(y_hbm, dest_ref[k * n_tok + base + r], buf.at[k], r, sem).start()
            return carry

        lax.fori_loop(0, rows, start, 0, unroll=MOE_DMA_UNROLL)
    for k in range(MOE_TOP_K):
        def wait(r, carry, k=k):
            _row_copy(y_hbm, 0, buf.at[k], r, sem).wait()
            return carry

        lax.fori_loop(0, rows, wait, 0, unroll=MOE_DMA_UNROLL)
    route = route_ref[...]
    w_lane0 = N_EXPERTS + MOE_TOP_K
    mix = functools.reduce(jnp.add, [route[:, w_lane0 + k:w_lane0 + k + 1] * buf[k] for k in range(MOE_TOP_K)])
    o_ref[...] = xs_ref[...] + gate_ref[...] * mix


def _moe_combine(xs, y_rows, dest, route, gate, tab):
    n_tok, d = xs.shape
    rows = _tile(tab.group_rows, MOE_COMBINE_ROWS)
    row = lambda w: pl.BlockSpec((rows, w), lambda i, dst: (i, 0))
    est = (MOE_TOP_K + 6) * _nbytes((rows, d), F32)
    return pl.pallas_call(
        functools.partial(_moe_combine_kernel, rows=rows, n_tok=n_tok),
        grid_spec=pltpu.PrefetchScalarGridSpec(
            num_scalar_prefetch=1, grid=(n_tok // rows,),
            in_specs=[pl.BlockSpec(memory_space=pl.ANY), row(d), row(route.shape[1]),
                      pl.BlockSpec((None, 1, d), lambda i, dst: (tab.index_of_row(i * rows), 0, 0))],
            out_specs=row(d),
            scratch_shapes=[pltpu.VMEM((MOE_TOP_K, rows, d), F32), pltpu.SemaphoreType.DMA]),
        out_shape=jax.ShapeDtypeStruct((n_tok, d), F32),
        compiler_params=pltpu.CompilerParams(dimension_semantics=("arbitrary",),
                                             vmem_limit_bytes=_vmem_limit(est)),
        name="moe_combine",
    )(dest, y_rows, xs, route, gate)


def _moe_sparse(xs, h2, route, w1, w3, w2, layer, gate, tab):
    n_tok = xs.shape[0]
    src, dest, tile_expert, n_used = _moe_routing(route, n_tok)
    xg = _moe_gather(h2, src, n_used)
    ug = _moe_grouped(_moe_up_kernel, xg, (w1, w3), layer, tile_expert, n_used, tn=256, out_dtype=BF16,
                      name="moe_up")
    yg = _moe_grouped(_moe_down_kernel, ug, (w2,), layer, tile_expert, n_used, tn=1024, out_dtype=F32,
                      name="moe_down")
    return _moe_combine(xs, yg, dest, route, gate, tab)


def kernel(x, c, ctx, c_ctx, ada_w, ada_b, norm1_g, norm2_g, w_in, attn_q_norm, attn_k_norm, s5_a_re, s5_a_im, s5_log_step, s5_b_re, s5_b_im, s5_c_re, s5_c_im, s5_d, s5_w_glu, diff_lambda, diff_norm, ret_decay_logit, ret_norm, w_branch, w_merge_gate, w_out, ffn_w1, ffn_w3, ffn_w2, moe_router_w, moe_router_b, moe_w1, moe_w3, moe_w2, final_norm_g):
    n_batch, t_lat, d = x.shape
    t_ctx = ctx.shape[1]
    depth = w_in.shape[0]
    n_lat, n_ctx = n_batch * t_lat, n_batch * t_ctx
    bw = d // 4
    s5_params = dict(s5_a_re=s5_a_re, s5_a_im=s5_a_im, s5_log_step=s5_log_step, s5_b_re=s5_b_re,
                     s5_b_im=s5_b_im, s5_c_re=s5_c_re, s5_c_im=s5_c_im, s5_d=s5_d, s5_w_glu=s5_w_glu)

    tab = RowTable(lambda row: jnp.where(row < n_lat, row // t_lat, n_batch), t_lat)

    cos_t, sin_t = _rope_tables(t_lat, n_batch, t_ctx)
    ada_rows = 8
    c_all = jnp.zeros((ada_rows, d), F32).at[:n_batch].set(c).at[n_batch].set(c_ctx)
    xs = jnp.concatenate([x.reshape(n_lat, d), ctx.reshape(n_ctx, d)])

    for i in range(depth):
        need_ctx = i < depth - 1
        lam_init = 0.8 - 0.6 * math.exp(-0.3 * i)
        mod = _mm(c_all, ada_w, (i,), name="ada_mod", tm=ada_rows, tn=512, epi="bias", x_silu=True,
                  bias=ada_b[i].reshape(1, -1))
        tabs = [mod[:n_batch + 1, k * d:(k + 1) * d].reshape(n_batch + 1, 1, d) for k in range(6)]
        shift1, scale1, gate1, shift2, scale2, gate2 = tabs

        h = _modnorm(xs, norm1_g[i], shift1, scale1, tab)
        p = _mm(h, w_in, (i,), name="in_proj", tm=1024, tn=512)
        qk = _prep(p, cos_t, sin_t, attn_q_norm[i], attn_k_norm[i])

        att = functools.partial(_attention, qk=qk, p=p, n_batch=n_batch, t_lat=t_lat, t_ctx=t_ctx, tq=512)
        gqa = functools.partial(att, _gqa_kernel, name="gqa", heads=2,
                                q_w=4 * HEAD_DIM, q_blk0=0, k_w=HEAD_DIM, k_blk0=8, v_w=HEAD_DIM, v_blk0=10,
                                out_w=4 * HEAD_DIM)
        lp = diff_lambda[i].astype(F32)
        lam = (jnp.exp(jnp.sum(lp[0] * lp[1])) - jnp.exp(jnp.sum(lp[2] * lp[3])) + lam_init).reshape(1)
        dgain = diff_norm[i].reshape(1, 2 * HEAD_DIM)
        dif = functools.partial(att, functools.partial(_diff_kernel, out_scale=1.0 - lam_init),
                                name="diff",
                                heads=4, q_w=2 * HEAD_DIM, q_blk0=5, k_w=2 * HEAD_DIM, k_blk0=9,
                                v_w=2 * HEAD_DIM, v_blk0=18, out_w=2 * HEAD_DIM,
                                lead_in=(lam,), lead_specs=(pl.BlockSpec(memory_space=pltpu.SMEM),),
                                extra_in=(dgain,),
                                extra_specs=(pl.BlockSpec((1, 2 * HEAD_DIM), lambda b, hh, ii: (0, 0)),))
        ya_l, yc_l = gqa(ctx_queries=False), dif(ctx_queries=False)
        log_gamma = jax.nn.log_sigmoid(ret_decay_logit[i].astype(F32))
        yd_l, yd_c = _retention(qk, p, log_gamma, ret_norm[i], n_batch=n_batch, t_lat=t_lat, t_ctx=t_ctx,
                                heads=4, q_blk0=26, k_blk0=30, v_blk0=26, g_blk0=30, dv=2 * HEAD_DIM)
        s5_col0 = 12 * HEAD_DIM
        yb = _s5_branch(p, s5_col0, bw, s5_params, i, n_batch=n_batch, t_lat=t_lat, t_ctx=t_ctx)
        if need_ctx:
            ya = jnp.concatenate([ya_l, gqa(ctx_queries=True)])
            yc = jnp.concatenate([yc_l, dif(ctx_queries=True)])
            yd = jnp.concatenate([yd_l, yd_c])
            rows = n_lat + n_ctx
        else:
            ya, yc, yd = ya_l, yc_l, yd_l
            rows = n_lat
        acc = _merge(h, (ya, yb, yc, yd), w_merge_gate, w_branch, i, m_rows=rows)
        xs = _mm(acc, w_out, (i,), name="out_proj", tm=1024, tn=512, epi="resgate", res=xs, gate=gate1,
                 tab=tab, m_rows=rows)

        if i % 2 == 0:
            j = i // 2
            h2 = _modnorm(xs, norm2_g[i], shift2, scale2, tab)
            u = _swiglu_up(h2, ffn_w1, ffn_w3, (j,))
            half = u.shape[1] // 2
            for kc in range(2):
                xs = _mm(u, ffn_w2, (j,), name="ffn_down", tm=512, tn=512, tk=half, k0=kc, epi="resgate",
                         res=xs, gate=gate2, tab=tab)
        else:
            j = i // 2
            h2, route = _modnorm(xs, norm2_g[i], shift2, scale2, tab,
                                 router_w=moe_router_w[j], router_b=moe_router_b[j])
            xs = _moe_sparse(xs, h2, route, moe_w1, moe_w3, moe_w2, j, gate2, tab)

    return _final_norm(xs[:n_lat], final_norm_g).reshape(n_batch, t_lat, d)
```

```python
import functools
import math
from typing import Callable, NamedTuple

import jax
import jax.numpy as jnp
from jax import lax
from jax.experimental import pallas as pl
from jax.experimental.pallas import tpu as pltpu

F32 = jnp.float32
BF16 = jnp.bfloat16

HEAD_DIM = 128
GRID_W = 64
ROPE_BASE = 10000.0
ROPE_FREQS = HEAD_DIM // 4
EPS = 1e-6
A_GROUP = 4
S5_P = 16
S5_N = 64
S5_L = 8
LANES = 128
RET_CHUNK = 128
N_EXPERTS = 8
ROUTER_LANES = 128
V7X_VMEM_BYTES = 64 * 1024 * 1024
VMEM_HEADROOM_BYTES = 8 * 1024 * 1024


def _vmem_limit(nbytes):
    return int(min(nbytes + VMEM_HEADROOM_BYTES, V7X_VMEM_BYTES - VMEM_HEADROOM_BYTES))


def _nbytes(shape, dtype):
    return math.prod(shape) * jnp.dtype(dtype).itemsize


class RowTable(NamedTuple):
    index_of_row: Callable
    group_rows: int


def _tile(total, preferred):
    return math.gcd(total, preferred)


def _mm_kernel(*refs, x_silu, epi):
    it = iter(refs)
    x_ref, w_ref = next(it), next(it)
    bias_ref = next(it) if epi == "bias" else None
    res_ref, gate_ref = (next(it), next(it)) if epi == "resgate" else (None, None)
    o_ref = next(it)

    x = x_ref[...]
    if x_silu:
        x = x * jax.nn.sigmoid(x)
    acc = jnp.dot(x.astype(BF16), w_ref[...].astype(BF16), preferred_element_type=F32)
    if epi == "bias":
        acc = acc + bias_ref[...]
    elif epi == "resgate":
        acc = res_ref[...] + gate_ref[...] * acc
    o_ref[...] = acc.astype(o_ref.dtype)


def _mm(x, w, wlead=(), *, name, m_rows=None, tm, tn, tk=None, k0=0, epi="none",
        out_dtype=F32, x_silu=False, bias=None, res=None, gate=None, tab=None):
    m_total = x.shape[0] if m_rows is None else m_rows
    n_total = w.shape[-1]
    tk = x.shape[1] if tk is None else tk
    tm, tn = _tile(m_total, tm), _tile(n_total, tn)
    if tab is not None:
        tm = _tile(tab.group_rows, tm)
    nlead = len(wlead)
    in_specs = [
        pl.BlockSpec((tm, tk), lambda n, m: (m, k0)),
        pl.BlockSpec((None,) * nlead + (tk, tn), lambda n, m: tuple(wlead) + (k0, n)),
    ]
    args = [x, w]
    est = 2 * _nbytes((tm, tk), x.dtype) + 2 * _nbytes((tk, tn), w.dtype) + _nbytes((tk, tn), BF16)
    if epi == "bias":
        in_specs.append(pl.BlockSpec((1, tn), lambda n, m: (0, n)))
        args.append(bias)
    elif epi == "resgate":
        in_specs.append(pl.BlockSpec((tm, tn), lambda n, m: (m, n)))
        in_specs.append(pl.BlockSpec((None, 1, tn), lambda n, m: (tab.index_of_row(m * tm), 0, n)))
        args += [res, gate]
        est += 2 * _nbytes((tm, tn), F32)
    est += 2 * _nbytes((tm, tn), out_dtype) + 2 * _nbytes((tm, tn), F32)
    return pl.pallas_call(
        functools.partial(_mm_kernel, x_silu=x_silu, epi=epi),
        grid=(n_total // tn, m_total // tm),
        in_specs=in_specs,
        out_specs=pl.BlockSpec((tm, tn), lambda n, m: (m, n)),
        out_shape=jax.ShapeDtypeStruct((m_total, n_total), out_dtype),
        compiler_params=pltpu.CompilerParams(
            dimension_semantics=("parallel", "arbitrary"),
            vmem_limit_bytes=_vmem_limit(est)),
        name=name,
    )(*args)


def _rms(x):
    return x * lax.rsqrt(jnp.mean(x * x, axis=-1, keepdims=True) + EPS)


def _pack_bf16_pairs(x):
    w = x.shape[1] // 2
    as_bits = lambda v: lax.bitcast_convert_type(v.astype(BF16).astype(F32), jnp.uint32)
    return (as_bits(x[:, w:]) & jnp.uint32(0xFFFF0000)) | (as_bits(x[:, :w]) >> 16)


def _unpack_bf16_pairs(p):
    low = lax.bitcast_convert_type(p << 16, F32)
    high = lax.bitcast_convert_type(p & jnp.uint32(0xFFFF0000), F32)
    return low, high


def _modnorm_kernel(x_ref, g_ref, sh_ref, sc_ref, *rest, router):
    h = _rms(x_ref[...]) * g_ref[...] * (1.0 + sc_ref[...]) + sh_ref[...]
    if not router:
        (o_ref,) = rest
        o_ref[...] = h.astype(o_ref.dtype)
        return
    rw_ref, rb_ref, o_ref, gate_ref = rest
    o_ref[...] = _pack_bf16_pairs(h)
    logits = jnp.dot(h, rw_ref[...], preferred_element_type=F32,
                     precision=lax.Precision.HIGHEST) + rb_ref[...]
    lane = lax.broadcasted_iota(jnp.int32, logits.shape, 1).astype(F32)
    neg = jnp.float32(-jnp.inf)
    logits = jnp.where(lane < N_EXPERTS, logits, neg)
    v1 = jnp.max(logits, axis=-1, keepdims=True)
    i1 = jnp.min(jnp.where(logits == v1, lane, float(ROUTER_LANES)), axis=-1, keepdims=True)
    rest_l = jnp.where(lane == i1, neg, logits)
    v2 = jnp.max(rest_l, axis=-1, keepdims=True)
    i2 = jnp.min(jnp.where(rest_l == v2, lane, float(ROUTER_LANES)), axis=-1, keepdims=True)
    e2 = jnp.exp(v2 - v1)
    w1 = 1.0 / (1.0 + e2)
    w2 = e2 * w1
    routed = jnp.where(lane == i1, w1, jnp.where(lane == i2, w2, 0.0))
    for off, val in enumerate((i1, i2, w1, w2)):
        routed = jnp.where(lane == float(N_EXPERTS + off), val, routed)
    gate_ref[...] = routed


def _modnorm(x, g, shift_tab, scale_tab, tab, *, tm=256, router_w=None, router_b=None):
    m_total, d = x.shape
    tm = _tile(tab.group_rows, _tile(m_total, tm))
    row = pl.BlockSpec((tm, d), lambda m: (m, 0))
    vec = pl.BlockSpec((1, d), lambda m: (0, 0))
    per_group = pl.BlockSpec((None, 1, d), lambda m: (tab.index_of_row(m * tm), 0, 0))
    in_specs, args = [row, vec, per_group, per_group], [x, g.reshape(1, d), shift_tab, scale_tab]
    router = router_w is not None
    out_specs, out_shape = row, jax.ShapeDtypeStruct((m_total, d), BF16)
    est = 2 * _nbytes((tm, d), F32) * 3
    if router:
        rw = jnp.zeros((d, ROUTER_LANES), F32).at[:, :N_EXPERTS].set(router_w)
        rb = jnp.zeros((1, ROUTER_LANES), F32).at[0, :N_EXPERTS].set(router_b)
        in_specs += [pl.BlockSpec((d, ROUTER_LANES), lambda m: (0, 0)),
                     pl.BlockSpec((1, ROUTER_LANES), lambda m: (0, 0))]
        args += [rw, rb]
        out_specs = (pl.BlockSpec((tm, d // 2), lambda m: (m, 0)),
                     pl.BlockSpec((tm, ROUTER_LANES), lambda m: (m, 0)))
        out_shape = (jax.ShapeDtypeStruct((m_total, d // 2), jnp.uint32),
                     jax.ShapeDtypeStruct((m_total, ROUTER_LANES), F32))
        est += 2 * _nbytes((d, ROUTER_LANES), F32) * 4
    return pl.pallas_call(
        functools.partial(_modnorm_kernel, router=router),
        grid=(m_total // tm,), in_specs=in_specs, out_specs=out_specs, out_shape=out_shape,
        compiler_params=pltpu.CompilerParams(dimension_semantics=("parallel",),
                                             vmem_limit_bytes=_vmem_limit(est)),
        name="modnorm_router" if router else "modnorm",
    )(*args)


def _final_norm_kernel(x_ref, g_ref, o_ref):
    o_ref[...] = _rms(x_ref[...]) * g_ref[...]


def _final_norm(x, g, *, tm=256):
    m_total, d = x.shape
    tm = _tile(m_total, tm)
    row = pl.BlockSpec((tm, d), lambda m: (m, 0))
    return pl.pallas_call(
        _final_norm_kernel, grid=(m_total // tm,),
        in_specs=[row, pl.BlockSpec((1, d), lambda m: (0, 0))], out_specs=row,
        out_shape=jax.ShapeDtypeStruct((m_total, d), F32),
        compiler_params=pltpu.CompilerParams(dimension_semantics=("parallel",),
                                             vmem_limit_bytes=_vmem_limit(6 * _nbytes((tm, d), F32))),
        name="final_norm",
    )(x, g.reshape(1, d))


class PrepHead(NamedTuple):
    src_col: int
    norm: str
    scale: float


PREP_SRC_W = 512
ATTN_SCALE = HEAD_DIM ** -0.5


def _prep_plan():
    h = HEAD_DIM
    plan = [PrepHead(j * h, "q", ATTN_SCALE) for j in range(8)]
    plan += [PrepHead((8 + j) * h, "k", 1.0) for j in range(2)]
    plan += [PrepHead((20 + j) * h, "", ATTN_SCALE) for j in range(8)]
    plan += [PrepHead((28 + j) * h, "", 1.0) for j in range(8)]
    plan += [PrepHead((44 + j) * h, "", 1.0) for j in range(4)]
    plan += [PrepHead((48 + j) * h, "", ATTN_SCALE) for j in range(4)]
    return plan


def _prep_kernel(*refs, plan, src_blocks):
    src_refs = dict(zip(src_blocks, refs[:len(src_blocks)]))
    cos_ref, sin_ref, qn_ref, kn_ref, o_ref = refs[len(src_blocks):]
    cos, sin = cos_ref[...], sin_ref[...]
    lane = lax.broadcasted_iota(jnp.int32, cos.shape, 1)
    first_half = (lane % (2 * ROPE_FREQS)) < ROPE_FREQS
    gains = {"q": qn_ref[...], "k": kn_ref[...]}
    for j, head in enumerate(plan):
        blk, off = divmod(head.src_col, PREP_SRC_W)
        y = src_refs[blk][:, off:off + HEAD_DIM]
        if head.norm:
            y = _rms(y) * gains[head.norm]
        if head.scale != 1.0:
            y = y * head.scale
        partner = jnp.where(first_half, pltpu.roll(y, HEAD_DIM - ROPE_FREQS, 1), pltpu.roll(y, ROPE_FREQS, 1))
        o_ref[:, j * HEAD_DIM:(j + 1) * HEAD_DIM] = (y * cos + partner * sin).astype(o_ref.dtype)


def _prep(p, cos_t, sin_t, q_gain, k_gain, *, tm=256):
    m_total = p.shape[0]
    tm = _tile(m_total, tm)
    plan = _prep_plan()
    src_blocks = sorted({head.src_col // PREP_SRC_W for head in plan})
    row = lambda w, blk: pl.BlockSpec((tm, w), lambda m: (m, blk))
    vec = pl.BlockSpec((1, HEAD_DIM), lambda m: (0, 0))
    out_w = len(plan) * HEAD_DIM
    est = 2 * len(src_blocks) * _nbytes((tm, PREP_SRC_W), F32) + 2 * _nbytes((tm, out_w), BF16)
    return pl.pallas_call(
        functools.partial(_prep_kernel, plan=plan, src_blocks=src_blocks),
        grid=(m_total // tm,),
        in_specs=[row(PREP_SRC_W, blk) for blk in src_blocks] + [row(HEAD_DIM, 0), row(HEAD_DIM, 0), vec, vec],
        out_specs=row(out_w, 0),
        out_shape=jax.ShapeDtypeStruct((m_total, out_w), BF16),
        compiler_params=pltpu.CompilerParams(dimension_semantics=("parallel",),
                                             vmem_limit_bytes=_vmem_limit(est)),
        name="qk_prep",
    )(*([p] * len(src_blocks)), cos_t, sin_t, q_gain.reshape(1, HEAD_DIM), k_gain.reshape(1, HEAD_DIM))


def _rope_tables(t_lat, n_batch, t_ctx):
    rows = t_lat // GRID_W
    row = jnp.broadcast_to(jnp.arange(rows)[:, None], (rows, GRID_W)).reshape(-1)
    col = jnp.broadcast_to(jnp.arange(GRID_W)[None, :], (rows, GRID_W)).reshape(-1)
    inv = ROPE_BASE ** (-jnp.arange(ROPE_FREQS, dtype=F32) / ROPE_FREQS)
    ang = jnp.stack([row, col], axis=-1).astype(F32)[:, :, None] * inv
    cos, sin = jnp.cos(ang), jnp.sin(ang)
    cos128 = jnp.concatenate([cos, cos], axis=-1).reshape(t_lat, HEAD_DIM)
    sin128 = jnp.concatenate([-sin, sin], axis=-1).reshape(t_lat, HEAD_DIM)
    cos_t = jnp.concatenate([jnp.tile(cos128, (n_batch, 1)), jnp.ones((n_batch * t_ctx, HEAD_DIM), F32)])
    sin_t = jnp.concatenate([jnp.tile(sin128, (n_batch, 1)), jnp.zeros((n_batch * t_ctx, HEAD_DIM), F32)])
    return cos_t, sin_t


_NT = (((1,), (1,)), ((), ()))


def _softmax_parts(q, ks):
    ss = [lax.dot_general(q, k, _NT, preferred_element_type=F32) for k in ks]
    m = functools.reduce(jnp.maximum, [jnp.max(s, axis=-1, keepdims=True) for s in ss])
    es = [jnp.exp(s - m) for s in ss]
    l = functools.reduce(jnp.add, [jnp.sum(e, axis=-1, keepdims=True) for e in es])
    return es, l


def _gqa_kernel(q_ref, *rest, nseg):
    k_refs, v_refs, o_ref = rest[:nseg], rest[nseg:2 * nseg], rest[2 * nseg]
    ks = [r[...] for r in k_refs]
    vs = [r[...].astype(BF16) for r in v_refs]
    for g in range(A_GROUP):
        cols = slice(g * HEAD_DIM, (g + 1) * HEAD_DIM)
        es, l = _softmax_parts(q_ref[:, cols], ks)
        o = functools.reduce(jnp.add, [jnp.dot(e.astype(BF16), v, preferred_element_type=F32)
                                       for e, v in zip(es, vs)])
        o_ref[:, cols] = (o / l).astype(o_ref.dtype)


def _diff_kernel(lam_ref, q_ref, *rest, nseg, out_scale):
    k_refs, v_refs = rest[:nseg], rest[nseg:2 * nseg]
    gain_ref, o_ref = rest[2 * nseg], rest[2 * nseg + 1]
    lam = lam_ref[0]
    vs = [r[...].astype(BF16) for r in v_refs]
    maps = []
    for m in range(2):
        cols = slice(m * HEAD_DIM, (m + 1) * HEAD_DIM)
        es, l = _softmax_parts(q_ref[:, cols], [r[:, cols] for r in k_refs])
        pv = functools.reduce(jnp.add, [jnp.dot(e.astype(BF16), v, preferred_element_type=F32)
                                        for e, v in zip(es, vs)])
        maps.append(pv / l)
    o = maps[0] - lam * maps[1]
    o_ref[...] = (_rms(o) * gain_ref[...] * out_scale).astype(o_ref.dtype)


def _attention(kernel, qk, p, *, name, n_batch, t_lat, t_ctx, heads, q_w, q_blk0, k_w, k_blk0, v_w, v_blk0,
               out_w, ctx_queries, tq, extra_in=(), extra_specs=(), lead_in=(), lead_specs=()):
    n_lat = n_batch * t_lat
    ctx_row0 = n_lat // t_ctx
    tq = _tile(t_lat, tq)
    if ctx_queries:
        nq, q_rows = 1, t_ctx
        q_map = lambda b, h, i: (ctx_row0 + b, q_blk0 + h)
        out_rows, o_map = n_batch * t_ctx, (lambda b, h, i: (b, h))
        segs = [(t_ctx, lambda b: ctx_row0 + b)]
    else:
        nq, q_rows = t_lat // tq, tq
        q_map = lambda b, h, i: (b * nq + i, q_blk0 + h)
        out_rows, o_map = n_lat, (lambda b, h, i: (b * nq + i, h))
        segs = [(t_ctx, lambda b: ctx_row0 + b), (t_lat, lambda b: b)]
    k_specs = [pl.BlockSpec((rows, k_w), functools.partial(lambda rf, b, h, i: (rf(b), k_blk0 + h), rf))
               for rows, rf in segs]
    v_specs = [pl.BlockSpec((rows, v_w), functools.partial(lambda rf, b, h, i: (rf(b), v_blk0 + h), rf))
               for rows, rf in segs]
    nseg = len(segs)
    est = 8 * _nbytes((q_rows, t_lat + t_ctx), F32) + 4 * _nbytes((t_lat + t_ctx, v_w), F32)
    return pl.pallas_call(
        functools.partial(kernel, nseg=nseg),
        grid=(n_batch, heads, nq),
        in_specs=list(lead_specs) + [pl.BlockSpec((q_rows, q_w), q_map)] + k_specs + v_specs + list(extra_specs),
        out_specs=pl.BlockSpec((q_rows, out_w), o_map),
        out_shape=jax.ShapeDtypeStruct((out_rows, heads * out_w), BF16),
        compiler_params=pltpu.CompilerParams(dimension_semantics=("parallel", "parallel", "arbitrary"),
                                             vmem_limit_bytes=_vmem_limit(est)),
        name=name + ("_ctx" if ctx_queries else "_lat"),
    )(*lead_in, qk, *([qk] * nseg), *([p] * nseg), *extra_in)


def _ret_kernel(lg_ref, ql_ref, qc_ref, kl_ref, kc_ref, vl_ref, vc_ref, gl_ref, gc_ref, gain_ref,
                yl_ref, yc_ref, ol_scr, oc_scr, s_scr, *, t_lat, t_ctx):
    h = pl.program_id(1)
    lg_f, lg_b = lg_ref[0, h], lg_ref[1, h]
    c = RET_CHUNK
    pos_i = lax.broadcasted_iota(jnp.int32, (c, c), 0).astype(F32)
    pos_j = lax.broadcasted_iota(jnp.int32, (c, c), 1).astype(F32)
    rel = pos_i - pos_j
    d_both = (jnp.where(rel >= 0, jnp.exp(jnp.maximum(rel, 0.0) * lg_f), 0.0)
              + jnp.where(rel <= 0, jnp.exp(jnp.maximum(-rel, 0.0) * lg_b), 0.0))
    pos = lax.broadcasted_iota(jnp.int32, (c, 1), 0).astype(F32)
    qdec = (jnp.exp((pos + 1.0) * lg_f), jnp.exp((c - pos) * lg_b))
    kdec = (jnp.exp((c - 1.0 - pos) * lg_f), jnp.exp(pos * lg_b))
    one = jnp.ones((1, 1), F32)
    sdec = (jnp.exp(one * (c * lg_f)), jnp.exp(one * (c * lg_b)))

    chunks = [(qc_ref, kc_ref, vc_ref, oc_scr, i) for i in range(t_ctx // c)]
    chunks += [(ql_ref, kl_ref, vl_ref, ol_scr, i) for i in range(t_lat // c)]
    back = ([ch for ch in chunks if ch[0] is qc_ref][::-1] + [ch for ch in chunks if ch[0] is ql_ref][::-1])

    for d, order in ((0, chunks), (1, back)):
        s_scr[...] = jnp.zeros_like(s_scr)
        for q_ref, k_ref, v_ref, o_scr, i in order:
            rows = pl.ds(i * c, c)
            q, k = q_ref[rows, :], k_ref[rows, :]
            v = v_ref[rows, :].astype(BF16)
            state = s_scr[...]
            cross = jnp.dot((q.astype(F32) * qdec[d]).astype(BF16), state.astype(BF16),
                            preferred_element_type=F32)
            if d == 0:
                scores = lax.dot_general(q, k, _NT, preferred_element_type=F32) * d_both
                o_scr[rows, :] = cross + jnp.dot(scores.astype(BF16), v, preferred_element_type=F32)
            else:
                o_scr[rows, :] += cross
            kv = jnp.dot((k.astype(F32) * kdec[d]).T.astype(BF16), v, preferred_element_type=F32)
            s_scr[...] = sdec[d] * state + kv

    gain = gain_ref[...]
    for o_scr, g_ref, y_ref in ((ol_scr, gl_ref, yl_ref), (oc_scr, gc_ref, yc_ref)):
        g = g_ref[...]
        y_ref[...] = (_rms(o_scr[...]) * gain * (g * jax.nn.sigmoid(g))).astype(y_ref.dtype)


def _retention(qk, p, log_gamma, gain, *, n_batch, t_lat, t_ctx, heads, q_blk0, k_blk0, v_blk0, g_blk0, dv):
    n_lat = n_batch * t_lat
    ctx_row0 = n_lat // t_ctx
    lat = lambda w, blk0: pl.BlockSpec((t_lat, w), lambda b, h: (b, blk0 + h))
    ctx = lambda w, blk0: pl.BlockSpec((t_ctx, w), lambda b, h: (ctx_row0 + b, blk0 + h))
    est = 6 * _nbytes((t_lat + t_ctx, dv), F32) * 2
    return pl.pallas_call(
        functools.partial(_ret_kernel, t_lat=t_lat, t_ctx=t_ctx),
        grid=(n_batch, heads),
        in_specs=[pl.BlockSpec(memory_space=pltpu.SMEM),
                  lat(HEAD_DIM, q_blk0), ctx(HEAD_DIM, q_blk0), lat(HEAD_DIM, k_blk0), ctx(HEAD_DIM, k_blk0),
                  lat(dv, v_blk0), ctx(dv, v_blk0), lat(dv, g_blk0), ctx(dv, g_blk0),
                  pl.BlockSpec((1, dv), lambda b, h: (0, 0))],
        out_specs=(pl.BlockSpec((t_lat, dv), lambda b, h: (b, h)),
                   pl.BlockSpec((t_ctx, dv), lambda b, h: (b, h))),
        out_shape=(jax.ShapeDtypeStruct((n_lat, heads * dv), BF16),
                   jax.ShapeDtypeStruct((n_batch * t_ctx, heads * dv), BF16)),
        scratch_shapes=[pltpu.VMEM((t_lat, dv), F32), pltpu.VMEM((t_ctx, dv), F32),
                        pltpu.VMEM((HEAD_DIM, dv), F32)],
        compiler_params=pltpu.CompilerParams(dimension_semantics=("parallel", "arbitrary"),
                                             vmem_limit_bytes=_vmem_limit(est)),
        name="retention",
    )(log_gamma, qk, qk, qk, qk, p, p, p, p, gain.reshape(1, dv))


def _s5_tables(a_re, a_im, log_step, b_re, b_im, c_re, c_im):
    hp = lax.Precision.HIGHEST
    n_dir, n_grp, n_st = a_re.shape
    ln = S5_L
    tg = LANES // S5_P
    n_tile = n_grp // tg
    lam = lax.complex(a_re.astype(F32), a_im.astype(F32))
    step = jnp.exp(log_step.astype(F32))[..., None]
    a_bar = jnp.exp(lam * step)
    b_bar = ((a_bar - 1.0) / lam)[..., None] * lax.complex(b_re.astype(F32), b_im.astype(F32))
    c_mat = lax.complex(c_re.astype(F32), c_im.astype(F32))
    taus = jnp.arange(ln + 1, dtype=F32)
    apow = jnp.exp((lam * step)[..., None] * taus)

    kern = jnp.einsum("dgpn,dgnt,dgnq->dgtpq", c_mat, apow[..., :ln], b_bar, precision=hp).real
    ti = jnp.arange(ln)[:, None]
    tj = jnp.arange(ln)[None, :]
    t_f = kern[0][:, jnp.clip(ti - tj, 0)] * (ti >= tj)[None, :, :, None, None]
    t_b = kern[1][:, jnp.clip(tj - ti, 0)] * (tj >= ti)[None, :, :, None, None]
    toep = (t_f + t_b).reshape(n_tile, tg, ln, ln, S5_P, S5_P)
    m_intra = toep.transpose(0, 3, 1, 5, 2, 4).reshape(n_tile, ln, LANES, ln * S5_P)

    def in_mat(d, tau_of_j):
        w = apow[d][:, :, tau_of_j][..., None] * b_bar[d][:, :, None, :]
        wri = jnp.stack([w.real, w.imag]).reshape(2, n_tile, tg, n_st, ln, S5_P)
        return wri.transpose(1, 4, 2, 5, 0, 3).reshape(n_tile, ln, LANES, 2 * n_st)

    def out_mat(d, tau_of_i):
        w = c_mat[d][:, :, :, None] * apow[d][:, None, :, tau_of_i]
        wri = jnp.stack([w.real, -w.imag]).reshape(2, n_tile, tg, S5_P, n_st, ln)
        return wri.transpose(1, 0, 2, 4, 5, 3).reshape(n_tile, 2 * tg * n_st, ln * S5_P)

    def spread(n_x, n_y):
        r, c = jnp.arange(n_x * n_y)[:, None], jnp.arange(n_x * tg * n_y)[None, :]
        return ((r // n_y == c // (tg * n_y)) & (r % n_y == c % n_y)).astype(BF16)

    idx = jnp.arange(ln)
    fold = lambda m: m.reshape(n_tile, ln * LANES, m.shape[-1])
    mats = [fold(m_intra), fold(in_mat(0, ln - 1 - idx)), fold(in_mat(1, idx)),
            out_mat(0, idx + 1), out_mat(1, ln - idx)]
    dec = [apow[0][..., ln].real, apow[0][..., ln].imag, apow[1][..., ln].real, apow[1][..., ln].imag]
    return ([m.astype(BF16) for m in mats], [spread(ln, S5_P), spread(2, n_st)],
            [x.reshape(n_tile, 1, tg * n_st) for x in dec])


def _group_of(index, period, width):
    assert period & (period - 1) == 0 and width & (width - 1) == 0
    return lax.shift_right_logical(index & (period - 1), width.bit_length() - 1)


def _s5_kernel(u_ref, m_ref, bcf_ref, bcb_ref, ccf_ref, ccb_ref, sp_out_ref, sp_state_ref,
               dfr_ref, dfi_ref, dbr_ref, dbi_ref, y_ref, vf, vb, *, n_batch, t_lat, t_ctx):
    ln = S5_L
    n_lat = n_batch * t_lat
    n_cr, n_lr = t_ctx // ln, t_lat // ln
    n_st = vf.shape[0]
    n_re = n_st // 2
    state_w = n_re * LANES

    def widen(c_ref, sp_ref, row_period, row_width, col_period, col_width):
        wide = jnp.dot(c_ref[...], sp_ref[...], preferred_element_type=F32)
        r = lax.broadcasted_iota(jnp.int32, wide.shape, 0)
        c = lax.broadcasted_iota(jnp.int32, wide.shape, 1)
        own = _group_of(r, row_period, row_width) == _group_of(c, col_period, col_width)
        return jnp.where(own, wide, 0.0).astype(BF16)

    lane_rows, state_rows = (LANES, S5_P), (state_w, state_w // (LANES // S5_P))
    m_w = widen(m_ref, sp_out_ref, *lane_rows, *lane_rows)
    bc_w = [widen(r, sp_state_ref, *lane_rows, *state_rows) for r in (bcf_ref, bcb_ref)]
    cc_w = [widen(r, sp_out_ref, *state_rows, *lane_rows) for r in (ccf_ref, ccb_ref)]

    def chunk_tokens(b, j):
        ctx_rows = u_ref[pl.ds(n_lat + b * t_ctx + j, n_cr, stride=ln), :]
        lat_rows = u_ref[pl.ds(b * t_lat + j, n_lr, stride=ln), :]
        return jnp.concatenate([ctx_rows, lat_rows], axis=0).astype(BF16)

    def chunks(b):
        return jnp.concatenate([chunk_tokens(b, j) for j in range(ln)], axis=1)

    def mix(uc, w):
        return jnp.dot(uc, w, preferred_element_type=F32)

    def batch_rows(b):
        return pl.ds(b, n_cr + n_lr, stride=n_batch)

    for b in range(n_batch):
        uc = chunks(b)
        for v, w in zip((vf, vb), bc_w):
            contrib = mix(uc, w)
            for c in range(n_st):
                v[c, batch_rows(b), :] = contrib[:, c * LANES:(c + 1) * LANES]

    tile_rows = 8
    tile_chunks = tile_rows // n_batch
    n_ct, n_tiles = n_cr // tile_chunks, (n_cr + n_lr) // tile_chunks

    def scan(v, d_re, d_im, tile_of_step, backward):
        order = range(tile_chunks - 1, -1, -1) if backward else range(tile_chunks)

        def step(s, carry):
            rows = pl.ds(pl.multiple_of(tile_of_step(s) * tile_rows, tile_rows), tile_rows)
            new_carry = []
            for c in range(n_re):
                s_re, s_im = carry[2 * c], carry[2 * c + 1]
                t_re, t_im = v[c, rows, :], v[n_re + c, rows, :]
                dr, di = d_re[:, c * LANES:(c + 1) * LANES], d_im[:, c * LANES:(c + 1) * LANES]
                o_re, o_im = [None] * tile_chunks, [None] * tile_chunks
                for k in order:
                    sub = slice(k * n_batch, (k + 1) * n_batch)
                    o_re[k], o_im[k] = s_re, s_im
                    s_re, s_im = dr * s_re - di * s_im + t_re[sub], dr * s_im + di * s_re + t_im[sub]
                v[c, rows, :] = jnp.concatenate(o_re, axis=0)
                v[n_re + c, rows, :] = jnp.concatenate(o_im, axis=0)
                new_carry += [s_re, s_im]
            return tuple(new_carry)

        zero = jnp.zeros((n_batch, LANES), F32)
        lax.fori_loop(0, n_tiles, step, (zero,) * n_st)

    scan(vf, dfr_ref[...], dfi_ref[...], lambda s: s, False)
    scan(vb, dbr_ref[...], dbi_ref[...],
         lambda s: jnp.where(s < n_ct, n_ct - 1 - s, n_tiles + n_ct - 1 - s), True)

    def entering_state(v, b):
        return jnp.concatenate([v[c, batch_rows(b), :] for c in range(n_st)], axis=1).astype(BF16)

    for b in range(n_batch):
        y = (mix(chunks(b), m_w) + mix(entering_state(vf, b), cc_w[0]) + mix(entering_state(vb, b), cc_w[1]))
        for i in range(ln):
            lanes = slice(i * LANES, (i + 1) * LANES)
            y_ref[pl.ds(n_lat + b * t_ctx + i, n_cr, stride=ln), :] = y[:n_cr, lanes]
            y_ref[pl.ds(b * t_lat + i, n_lr, stride=ln), :] = y[n_cr:, lanes]


def _s5_core(p, col_blk0, mats, spreads, decs, *, n_batch, t_lat, t_ctx):
    rows = p.shape[0]
    n_tile = mats[0].shape[0]
    n_chunk = (t_lat + t_ctx) // S5_L
    assert 8 % n_batch == 0 and (t_ctx // S5_L) % (8 // n_batch) == 0 and (t_lat // S5_L) % (8 // n_batch) == 0
    state = spreads[1].shape[1]
    wide = S5_L * LANES
    mat_specs = [pl.BlockSpec((None,) + m.shape[1:], lambda t: (t, 0, 0)) for m in mats]
    spread_specs = [pl.BlockSpec(sp.shape, lambda t: (0, 0)) for sp in spreads]
    dec_specs = [pl.BlockSpec((None, 1, state // 2), lambda t: (t, 0, 0)) for _ in decs]
    est = (4 * _nbytes((rows, LANES), F32) + len(mats) * 2 * _nbytes((wide, wide), F32)
           + 2 * _nbytes((n_chunk * n_batch, state), F32) + 6 * _nbytes((n_chunk, state), F32))
    return pl.pallas_call(
        functools.partial(_s5_kernel, n_batch=n_batch, t_lat=t_lat, t_ctx=t_ctx),
        grid=(n_tile,),
        in_specs=([pl.BlockSpec((rows, LANES), lambda t: (0, col_blk0 + t))] + mat_specs + spread_specs
                  + dec_specs),
        out_specs=pl.BlockSpec((rows, LANES), lambda t: (0, t)),
        out_shape=jax.ShapeDtypeStruct((rows, n_tile * LANES), F32),
        scratch_shapes=[pltpu.VMEM((state // LANES, n_chunk * n_batch, LANES), F32) for _ in range(2)],
        compiler_params=pltpu.CompilerParams(dimension_semantics=("arbitrary",),
                                             vmem_limit_bytes=_vmem_limit(est)),
        name="s5_core",
    )(p, *mats, *spreads, *decs)


def _s5_finish_kernel(y_ref, ua_ref, ub_ref, d_ref, w_ref, o_ref):
    u = jnp.concatenate([ua_ref[...], ub_ref[...]], axis=1)
    z = jax.nn.gelu(y_ref[...] + d_ref[...] * u)
    gl = jnp.dot(z.astype(BF16), w_ref[...].astype(BF16), preferred_element_type=F32)
    o_ref[...] = (z * jax.nn.sigmoid(gl)).astype(o_ref.dtype)


def _s5_finish(y, p, col_half0, d_skip, w_glu, wlead, *, tm=512):
    m_total, w = y.shape
    tm = _tile(m_total, tm)
    row = pl.BlockSpec((tm, w), lambda m: (m, 0))
    nlead = len(wlead)
    return pl.pallas_call(
        _s5_finish_kernel, grid=(m_total // tm,),
        in_specs=[row, pl.BlockSpec((tm, w // 2), lambda m: (m, col_half0)),
                  pl.BlockSpec((tm, w // 2), lambda m: (m, col_half0 + 1)),
                  pl.BlockSpec((1, w), lambda m: (0, 0)),
                  pl.BlockSpec((None,) * nlead + (w, w), lambda m: tuple(wlead) + (0, 0))],
        out_specs=row, out_shape=jax.ShapeDtypeStruct((m_total, w), BF16),
        compiler_params=pltpu.CompilerParams(
            dimension_semantics=("parallel",),
            vmem_limit_bytes=_vmem_limit(8 * _nbytes((tm, w), F32) + 3 * _nbytes((w, w), F32))),
        name="s5_finish",
    )(y, p, p, d_skip.reshape(1, w), w_glu)


def _s5_branch(p, col0, bw, params, layer, *, n_batch, t_lat, t_ctx):
    mats, spreads, decs = _s5_tables(*(params[k][layer] for k in
                                       ("s5_a_re", "s5_a_im", "s5_log_step", "s5_b_re", "s5_b_im", "s5_c_re",
                                        "s5_c_im")))
    y = _s5_core(p, col0 // LANES, mats, spreads, decs, n_batch=n_batch, t_lat=t_lat, t_ctx=t_ctx)
    return _s5_finish(y, p, col0 // (bw // 2), params["s5_d"][layer], params["s5_w_glu"], (layer,))


def _merge_kernel(h_ref, *rest, n_branch):
    y_refs = rest[:n_branch]
    wg_ref, wb_ref, o_ref, acc_ref, val_ref = rest[n_branch:]
    j = pl.program_id(2)
    wb = wb_ref[...].astype(BF16)
    for idx, y_ref in enumerate(y_refs):
        @pl.when(j == idx)
        def _(y_ref=y_ref):
            val_ref[...] = jnp.dot(y_ref[...], wb, preferred_element_type=F32)

    gate = jnp.dot(h_ref[...], wg_ref[...].astype(BF16), preferred_element_type=F32)
    term = jax.nn.sigmoid(gate) * val_ref[...]

    @pl.when(j == 0)
    def _():
        acc_ref[...] = term

    @pl.when(j > 0)
    def _():
        acc_ref[...] += term

    @pl.when(j == n_branch - 1)
    def _():
        o_ref[...] = acc_ref[...].astype(o_ref.dtype)


def _merge(h, ys, w_gate, w_branch, layer, *, m_rows, tm=1024, tn=256):
    d = h.shape[1]
    tm = _tile(m_rows, tm)
    n_branch, bw = w_branch.shape[1], w_branch.shape[2]
    assert len(ys) == n_branch
    est = (2 * _nbytes((tm, d), BF16) + 2 * n_branch * _nbytes((tm, bw), BF16) + 3 * _nbytes((d, tn), F32)
           + 3 * _nbytes((bw, tn), F32) + 6 * _nbytes((tm, tn), F32))
    return pl.pallas_call(
        functools.partial(_merge_kernel, n_branch=n_branch),
        grid=(m_rows // tm, d // tn, n_branch),
        in_specs=[pl.BlockSpec((tm, d), lambda m, n, j: (m, 0))]
                 + [pl.BlockSpec((tm, bw), lambda m, n, j: (m, 0))] * n_branch
                 + [pl.BlockSpec((None, None, d, tn), lambda m, n, j: (layer, j, 0, n)),
                    pl.BlockSpec((None, None, bw, tn), lambda m, n, j: (layer, j, 0, n))],
        out_specs=pl.BlockSpec((tm, tn), lambda m, n, j: (m, n)),
        out_shape=jax.ShapeDtypeStruct((m_rows, d), BF16),
        scratch_shapes=[pltpu.VMEM((tm, tn), F32), pltpu.VMEM((tm, tn), F32)],
        compiler_params=pltpu.CompilerParams(dimension_semantics=("parallel", "parallel", "arbitrary"),
                                             vmem_limit_bytes=_vmem_limit(est)),
        name="merge",
    )(h, *ys, w_gate, w_branch)


def _swiglu_up_kernel(x_ref, w1_ref, w3_ref, o_ref):
    x = x_ref[...]
    a = jnp.dot(x, w1_ref[...].astype(BF16), preferred_element_type=F32)
    b = jnp.dot(x, w3_ref[...].astype(BF16), preferred_element_type=F32)
    o_ref[...] = (a * jax.nn.sigmoid(a) * b).astype(o_ref.dtype)


def _swiglu_up(x, w1, w3, wlead, *, tm=1024, tn=256):
    m_total, d = x.shape
    tm = _tile(m_total, tm)
    ff = w1.shape[-1]
    nlead = len(wlead)
    w_spec = pl.BlockSpec((None,) * nlead + (d, tn), lambda n, m: tuple(wlead) + (0, n))
    est = 2 * _nbytes((tm, d), BF16) + 6 * _nbytes((d, tn), F32) + 6 * _nbytes((tm, tn), F32)
    return pl.pallas_call(
        _swiglu_up_kernel,
        grid=(ff // tn, m_total // tm),
        in_specs=[pl.BlockSpec((tm, d), lambda n, m: (m, 0)), w_spec, w_spec],
        out_specs=pl.BlockSpec((tm, tn), lambda n, m: (m, n)),
        out_shape=jax.ShapeDtypeStruct((m_total, ff), BF16),
        compiler_params=pltpu.CompilerParams(dimension_semantics=("parallel", "arbitrary"),
                                             vmem_limit_bytes=_vmem_limit(est)),
        name="ffn_up",
    )(x, w1, w3)


MOE_TOP_K = 2
MOE_ROW_TILE = 512
MOE_GATHER_ROWS = 256
MOE_COMBINE_ROWS = 128
MOE_DMA_UNROLL = 8


def _moe_routing(gates, n_tok):
    tile = MOE_ROW_TILE
    n_rows = MOE_TOP_K * n_tok + N_EXPERTS * tile
    experts = jnp.concatenate([gates[:, N_EXPERTS + k] for k in range(MOE_TOP_K)]).astype(jnp.int32)
    onehot = (experts[:, None] == jnp.arange(N_EXPERTS, dtype=jnp.int32)[None, :]).astype(jnp.int32)
    before = jnp.cumsum(onehot, axis=0) - onehot
    counts = jnp.sum(onehot, axis=0)
    padded = ((counts + tile - 1) // tile) * tile
    ends = jnp.cumsum(padded)
    dest = jnp.sum(onehot * (before + (ends - padded)[None, :]), axis=1)
    tokens = jnp.tile(jnp.arange(n_tok, dtype=jnp.int32), MOE_TOP_K)
    src = jnp.zeros((n_rows,), jnp.int32).at[dest].set(tokens)
    tile_start = jnp.arange(n_rows // tile, dtype=jnp.int32) * tile
    tile_expert = jnp.minimum(jnp.sum((tile_start[:, None] >= ends[None, :]).astype(jnp.int32), axis=1),
                              N_EXPERTS - 1)
    return src, dest, tile_expert, (ends[-1:] // tile).astype(jnp.int32)


def _row_copy(src_hbm, row, dst_vmem, slot, sem):
    return pltpu.make_async_copy(src_hbm.at[pl.ds(row, 1)], dst_vmem.at[pl.ds(slot, 1)], sem)


def _moe_gather_kernel(src_ref, nused_ref, x_hbm, o_ref, sem, *, rows, steps_per_tile):
    step = pl.program_id(0)
    used = step < nused_ref[0] * steps_per_tile

    @pl.when(used)
    def _():
        base = step * rows

        def start(r, carry):
            _row_copy(x_hbm, src_ref[base + r], o_ref, r, sem).start()
            return carry

        def wait(r, carry):
            _row_copy(x_hbm, 0, o_ref, r, sem).wait()
            return carry

        lax.fori_loop(0, rows, start, 0, unroll=MOE_DMA_UNROLL)
        lax.fori_loop(0, rows, wait, 0, unroll=MOE_DMA_UNROLL)

    @pl.when(jnp.logical_not(used))
    def _():
        o_ref[...] = jnp.zeros_like(o_ref)


def _moe_gather(x, src, n_used):
    n_rows, w = src.shape[0], x.shape[1]
    rows = MOE_GATHER_ROWS
    return pl.pallas_call(
        functools.partial(_moe_gather_kernel, rows=rows, steps_per_tile=MOE_ROW_TILE // rows),
        grid_spec=pltpu.PrefetchScalarGridSpec(
            num_scalar_prefetch=2, grid=(n_rows // rows,),
            in_specs=[pl.BlockSpec(memory_space=pl.ANY)],
            out_specs=pl.BlockSpec((rows, w), lambda i, s, nu: (i, 0)),
            scratch_shapes=[pltpu.SemaphoreType.DMA]),
        out_shape=jax.ShapeDtypeStruct((n_rows, w), x.dtype),
        compiler_params=pltpu.CompilerParams(
            dimension_semantics=("arbitrary",),
            vmem_limit_bytes=_vmem_limit(2 * _nbytes((rows, w), x.dtype))),
        name="moe_gather",
    )(src, n_used, x)


def _moe_up_kernel(te_ref, nused_ref, x_ref, w1_ref, w3_ref, o_ref):
    used = pl.program_id(1) < nused_ref[0]

    @pl.when(used)
    def _():
        half = w1_ref.shape[0] // 2
        x_lo, x_hi = (v.astype(BF16) for v in _unpack_bf16_pairs(x_ref[...]))

        def proj(w_ref):
            return (jnp.dot(x_lo, w_ref[:half, :].astype(BF16), preferred_element_type=F32)
                    + jnp.dot(x_hi, w_ref[half:, :].astype(BF16), preferred_element_type=F32))

        a, b = proj(w1_ref), proj(w3_ref)
        o_ref[...] = (a * jax.nn.sigmoid(a) * b).astype(o_ref.dtype)

    @pl.when(jnp.logical_not(used))
    def _():
        o_ref[...] = jnp.zeros_like(o_ref)


def _moe_down_kernel(te_ref, nused_ref, u_ref, w2_ref, o_ref):
    used = pl.program_id(1) < nused_ref[0]

    @pl.when(used)
    def _():
        y = jnp.dot(u_ref[...], w2_ref[...].astype(BF16), preferred_element_type=F32)
        o_ref[...] = _pack_bf16_pairs(y)

    @pl.when(jnp.logical_not(used))
    def _():
        o_ref[...] = jnp.zeros_like(o_ref)


def _moe_grouped(kernel, x, ws, layer, tile_expert, n_used, *, tn, out_tn, out_dtype, name):
    n_rows, x_w = x.shape
    kdim, n_out = ws[0].shape[-2:]
    tm = MOE_ROW_TILE

    def row_blk(m, nu):
        return jnp.minimum(m, nu[0] - 1)

    w_spec = pl.BlockSpec((None, None, kdim, tn), lambda n, m, te, nu: (layer, te[row_blk(m, nu)], 0, n))
    est = (2 * _nbytes((tm, x_w), x.dtype) + len(ws) * 3 * _nbytes((kdim, tn), F32)
           + (3 + len(ws)) * _nbytes((tm, tn), F32) + 2 * _nbytes((tm, kdim), BF16))
    return pl.pallas_call(
        kernel,
        grid_spec=pltpu.PrefetchScalarGridSpec(
            num_scalar_prefetch=2, grid=(n_out // tn, n_rows // tm),
            in_specs=[pl.BlockSpec((tm, x_w), lambda n, m, te, nu: (row_blk(m, nu), 0))] + [w_spec] * len(ws),
            out_specs=pl.BlockSpec((tm, out_tn), lambda n, m, te, nu: (m, n))),
        out_shape=jax.ShapeDtypeStruct((n_rows, n_out // tn * out_tn), out_dtype),
        compiler_params=pltpu.CompilerParams(dimension_semantics=("parallel", "arbitrary"),
                                             vmem_limit_bytes=_vmem_limit(est)),
        name=name,
    )(tile_expert, n_used, x, *ws)


def _moe_combine_kernel(dest_ref, y_hbm, xs_ref, route_ref, gate_ref, o_ref, buf, sem, *, rows, n_tok, pack_w):
    base = pl.program_id(0) * rows
    for k in range(MOE_TOP_K):
        def start(r, carry, k=k):
            _row_copy(y_hbm, dest_ref[k * n_tok + base + r], buf.at[k], r, sem).start()
            return carry

        lax.fori_loop(0, rows, start, 0, unroll=MOE_DMA_UNROLL)
    for k in range(MOE_TOP_K):
        def wait(r, carry, k=k):
            _row_copy(y_hbm, 0, buf.at[k], r, sem).wait()
            return carry

        lax.fori_loop(0, rows, wait, 0, unroll=MOE_DMA_UNROLL)
    route = route_ref[...]
    w_lane0 = N_EXPERTS + MOE_TOP_K

    def expert_rows(k):
        halves = [_unpack_bf16_pairs(buf[k, :, c:c + pack_w]) for c in range(0, buf.shape[2], pack_w)]
        return jnp.concatenate([h for pair in halves for h in pair], axis=1)

    mix = functools.reduce(jnp.add, [route[:, w_lane0 + k:w_lane0 + k + 1] * expert_rows(k)
                                     for k in range(MOE_TOP_K)])
    o_ref[...] = xs_ref[...] + gate_ref[...] * mix


def _moe_combine(xs, y_rows, pack_w, dest, route, gate, tab):
    n_tok, d = xs.shape
    rows = _tile(tab.group_rows, MOE_COMBINE_ROWS)
    row = lambda w: pl.BlockSpec((rows, w), lambda i, dst: (i, 0))
    est = 8 * _nbytes((rows, d), F32)
    return pl.pallas_call(
        functools.partial(_moe_combine_kernel, rows=rows, n_tok=n_tok, pack_w=pack_w),
        grid_spec=pltpu.PrefetchScalarGridSpec(
            num_scalar_prefetch=1, grid=(n_tok // rows,),
            in_specs=[pl.BlockSpec(memory_space=pl.ANY), row(d), row(route.shape[1]),
                      pl.BlockSpec((None, 1, d), lambda i, dst: (tab.index_of_row(i * rows), 0, 0))],
            out_specs=row(d),
            scratch_shapes=[pltpu.VMEM((MOE_TOP_K, rows, y_rows.shape[1]), y_rows.dtype),
                            pltpu.SemaphoreType.DMA]),
        out_shape=jax.ShapeDtypeStruct((n_tok, d), F32),
        compiler_params=pltpu.CompilerParams(dimension_semantics=("arbitrary",),
                                             vmem_limit_bytes=_vmem_limit(est)),
        name="moe_combine",
    )(dest, y_rows, xs, route, gate)


def _moe_sparse(xs, h2p, route, w1, w3, w2, layer, gate, tab):
    n_tok = xs.shape[0]
    src, dest, tile_expert, n_used = _moe_routing(route, n_tok)
    xg = _moe_gather(h2p, src, n_used)
    up_tn, down_tn = 256, 1024
    ug = _moe_grouped(_moe_up_kernel, xg, (w1, w3), layer, tile_expert, n_used, tn=up_tn, out_tn=up_tn,
                      out_dtype=BF16, name="moe_up")
    yg = _moe_grouped(_moe_down_kernel, ug, (w2,), layer, tile_expert, n_used, tn=down_tn, out_tn=down_tn // 2,
                      out_dtype=jnp.uint32, name="moe_down")
    return _moe_combine(xs, yg, down_tn // 2, dest, route, gate, tab)


def kernel(x, c, ctx, c_ctx, ada_w, ada_b, norm1_g, norm2_g, w_in, attn_q_norm, attn_k_norm, s5_a_re, s5_a_im, s5_log_step, s5_b_re, s5_b_im, s5_c_re, s5_c_im, s5_d, s5_w_glu, diff_lambda, diff_norm, ret_decay_logit, ret_norm, w_branch, w_merge_gate, w_out, ffn_w1, ffn_w3, ffn_w2, moe_router_w, moe_router_b, moe_w1, moe_w3, moe_w2, final_norm_g):
    n_batch, t_lat, d = x.shape
    t_ctx = ctx.shape[1]
    depth = w_in.shape[0]
    n_lat, n_ctx = n_batch * t_lat, n_batch * t_ctx
    bw = d // 4
    s5_params = dict(s5_a_re=s5_a_re, s5_a_im=s5_a_im, s5_log_step=s5_log_step, s5_b_re=s5_b_re,
                     s5_b_im=s5_b_im, s5_c_re=s5_c_re, s5_c_im=s5_c_im, s5_d=s5_d, s5_w_glu=s5_w_glu)

    tab = RowTable(lambda row: jnp.where(row < n_lat, row // t_lat, n_batch), t_lat)

    cos_t, sin_t = _rope_tables(t_lat, n_batch, t_ctx)
    ada_rows = 8
    c_all = jnp.zeros((ada_rows, d), F32).at[:n_batch].set(c).at[n_batch].set(c_ctx)
    xs = jnp.concatenate([x.reshape(n_lat, d), ctx.reshape(n_ctx, d)])

    for i in range(depth):
        need_ctx = i < depth - 1
        lam_init = 0.8 - 0.6 * math.exp(-0.3 * i)
        mod = _mm(c_all, ada_w, (i,), name="ada_mod", tm=ada_rows, tn=512, epi="bias", x_silu=True,
                  bias=ada_b[i].reshape(1, -1))
        tabs = [mod[:n_batch + 1, k * d:(k + 1) * d].reshape(n_batch + 1, 1, d) for k in range(6)]
        shift1, scale1, gate1, shift2, scale2, gate2 = tabs

        h = _modnorm(xs, norm1_g[i], shift1, scale1, tab)
        p = _mm(h, w_in, (i,), name="in_proj", tm=1024, tn=512)
        qk = _prep(p, cos_t, sin_t, attn_q_norm[i], attn_k_norm[i])

        att = functools.partial(_attention, qk=qk, p=p, n_batch=n_batch, t_lat=t_lat, t_ctx=t_ctx, tq=512)
        gqa = functools.partial(att, _gqa_kernel, name="gqa", heads=2,
                                q_w=4 * HEAD_DIM, q_blk0=0, k_w=HEAD_DIM, k_blk0=8, v_w=HEAD_DIM, v_blk0=10,
                                out_w=4 * HEAD_DIM)
        lp = diff_lambda[i].astype(F32)
        lam = (jnp.exp(jnp.sum(lp[0] * lp[1])) - jnp.exp(jnp.sum(lp[2] * lp[3])) + lam_init).reshape(1)
        dgain = diff_norm[i].reshape(1, 2 * HEAD_DIM)
        dif = functools.partial(att, functools.partial(_diff_kernel, out_scale=1.0 - lam_init),
                                name="diff",
                                heads=4, q_w=2 * HEAD_DIM, q_blk0=5, k_w=2 * HEAD_DIM, k_blk0=9,
                                v_w=2 * HEAD_DIM, v_blk0=18, out_w=2 * HEAD_DIM,
                                lead_in=(lam,), lead_specs=(pl.BlockSpec(memory_space=pltpu.SMEM),),
                                extra_in=(dgain,),
                                extra_specs=(pl.BlockSpec((1, 2 * HEAD_DIM), lambda b, hh, ii: (0, 0)),))
        ya_l, yc_l = gqa(ctx_queries=False), dif(ctx_queries=False)
        log_gamma = jax.nn.log_sigmoid(ret_decay_logit[i].astype(F32))
        yd_l, yd_c = _retention(qk, p, log_gamma, ret_norm[i], n_batch=n_batch, t_lat=t_lat, t_ctx=t_ctx,
                                heads=4, q_blk0=26, k_blk0=30, v_blk0=26, g_blk0=30, dv=2 * HEAD_DIM)
        s5_col0 = 12 * HEAD_DIM
        yb = _s5_branch(p, s5_col0, bw, s5_params, i, n_batch=n_batch, t_lat=t_lat, t_ctx=t_ctx)
        if need_ctx:
            ya = jnp.concatenate([ya_l, gqa(ctx_queries=True)])
            yc = jnp.concatenate([yc_l, dif(ctx_queries=True)])
            yd = jnp.concatenate([yd_l, yd_c])
            rows = n_lat + n_ctx
        else:
            ya, yc, yd = ya_l, yc_l, yd_l
            rows = n_lat
        acc = _merge(h, (ya, yb, yc, yd), w_merge_gate, w_branch, i, m_rows=rows)
        xs = _mm(acc, w_out, (i,), name="out_proj", tm=1024, tn=512, epi="resgate", res=xs, gate=gate1,
                 tab=tab, m_rows=rows)

        if i % 2 == 0:
            j = i // 2
            h2 = _modnorm(xs, norm2_g[i], shift2, scale2, tab)
            u = _swiglu_up(h2, ffn_w1, ffn_w3, (j,))
            half = u.shape[1] // 2
            for kc in range(2):
                xs = _mm(u, ffn_w2, (j,), name="ffn_down", tm=512, tn=512, tk=half, k0=kc, epi="resgate",
                         res=xs, gate=gate2, tab=tab)
        else:
            j = i // 2
            h2, route = _modnorm(xs, norm2_g[i], shift2, scale2, tab,
                                 router_w=moe_router_w[j], router_b=moe_router_b[j])
            xs = _moe_sparse(xs, h2, route, moe_w1, moe_w3, moe_w2, j, gate2, tab)

    return _final_norm(xs[:n_lat], final_norm_g).reshape(n_batch, t_lat, d)
```

```python
import functools
import math
from typing import Callable, NamedTuple

import jax
import jax.numpy as jnp
from jax import lax
from jax.experimental import pallas as pl
from jax.experimental.pallas import tpu as pltpu

F32 = jnp.float32
BF16 = jnp.bfloat16

HEAD_DIM = 128
GRID_W = 64
ROPE_BASE = 10000.0
ROPE_FREQS = HEAD_DIM // 4
EPS = 1e-6
A_GROUP = 4
S5_P = 16
S5_N = 64
S5_L = 8
LANES = 128
RET_CHUNK = 128
N_EXPERTS = 8
ROUTER_LANES = 128
V7X_VMEM_BYTES = 64 * 1024 * 1024
VMEM_HEADROOM_BYTES = 8 * 1024 * 1024


def _vmem_limit(nbytes):
    return int(min(nbytes + VMEM_HEADROOM_BYTES, V7X_VMEM_BYTES - VMEM_HEADROOM_BYTES))


def _nbytes(shape, dtype):
    return math.prod(shape) * jnp.dtype(dtype).itemsize


class RowTable(NamedTuple):
    index_of_row: Callable
    group_rows: int


def _tile(total, preferred):
    return math.gcd(total, preferred)


def _mm_kernel(*refs, x_silu, epi):
    it = iter(refs)
    x_ref, w_ref = next(it), next(it)
    bias_ref = next(it) if epi == "bias" else None
    res_ref, gate_ref = (next(it), next(it)) if epi == "resgate" else (None, None)
    o_ref = next(it)

    x = x_ref[...]
    if x_silu:
        x = x * jax.nn.sigmoid(x)
    acc = jnp.dot(x.astype(BF16), w_ref[...].astype(BF16), preferred_element_type=F32)
    if epi == "bias":
        acc = acc + bias_ref[...]
    elif epi == "resgate":
        acc = res_ref[...] + gate_ref[...] * acc
    o_ref[...] = acc.astype(o_ref.dtype)


def _mm(x, w, wlead=(), *, name, m_rows=None, tm, tn, tk=None, k0=0, epi="none",
        out_dtype=F32, x_silu=False, bias=None, res=None, gate=None, tab=None):
    m_total = x.shape[0] if m_rows is None else m_rows
    n_total = w.shape[-1]
    tk = x.shape[1] if tk is None else tk
    tm, tn = _tile(m_total, tm), _tile(n_total, tn)
    if tab is not None:
        tm = _tile(tab.group_rows, tm)
    nlead = len(wlead)
    in_specs = [
        pl.BlockSpec((tm, tk), lambda n, m: (m, k0)),
        pl.BlockSpec((None,) * nlead + (tk, tn), lambda n, m: tuple(wlead) + (k0, n)),
    ]
    args = [x, w]
    est = 2 * _nbytes((tm, tk), x.dtype) + 2 * _nbytes((tk, tn), w.dtype) + _nbytes((tk, tn), BF16)
    if epi == "bias":
        in_specs.append(pl.BlockSpec((1, tn), lambda n, m: (0, n)))
        args.append(bias)
    elif epi == "resgate":
        in_specs.append(pl.BlockSpec((tm, tn), lambda n, m: (m, n)))
        in_specs.append(pl.BlockSpec((None, 1, tn), lambda n, m: (tab.index_of_row(m * tm), 0, n)))
        args += [res, gate]
        est += 2 * _nbytes((tm, tn), F32)
    est += 2 * _nbytes((tm, tn), out_dtype) + 2 * _nbytes((tm, tn), F32)
    return pl.pallas_call(
        functools.partial(_mm_kernel, x_silu=x_silu, epi=epi),
        grid=(n_total // tn, m_total // tm),
        in_specs=in_specs,
        out_specs=pl.BlockSpec((tm, tn), lambda n, m: (m, n)),
        out_shape=jax.ShapeDtypeStruct((m_total, n_total), out_dtype),
        compiler_params=pltpu.CompilerParams(
            dimension_semantics=("parallel", "arbitrary"),
            vmem_limit_bytes=_vmem_limit(est)),
        name=name,
    )(*args)


def _rms(x):
    return x * lax.rsqrt(jnp.mean(x * x, axis=-1, keepdims=True) + EPS)


def _pack_bf16_pairs(x):
    w = x.shape[1] // 2
    as_bits = lambda v: lax.bitcast_convert_type(v.astype(BF16).astype(F32), jnp.uint32)
    return (as_bits(x[:, w:]) & jnp.uint32(0xFFFF0000)) | (as_bits(x[:, :w]) >> 16)


def _unpack_bf16_pairs(p):
    low = lax.bitcast_convert_type(p << 16, F32)
    high = lax.bitcast_convert_type(p & jnp.uint32(0xFFFF0000), F32)
    return low, high


def _modnorm_kernel(x_ref, g_ref, sh_ref, sc_ref, *rest, router):
    h = _rms(x_ref[...]) * g_ref[...] * (1.0 + sc_ref[...]) + sh_ref[...]
    if not router:
        (o_ref,) = rest
        o_ref[...] = h.astype(o_ref.dtype)
        return
    rw_ref, rb_ref, o_ref, gate_ref = rest
    o_ref[...] = _pack_bf16_pairs(h)
    logits = jnp.dot(h, rw_ref[...], preferred_element_type=F32,
                     precision=lax.Precision.HIGHEST) + rb_ref[...]
    lane = lax.broadcasted_iota(jnp.int32, logits.shape, 1).astype(F32)
    neg = jnp.float32(-jnp.inf)
    logits = jnp.where(lane < N_EXPERTS, logits, neg)
    v1 = jnp.max(logits, axis=-1, keepdims=True)
    i1 = jnp.min(jnp.where(logits == v1, lane, float(ROUTER_LANES)), axis=-1, keepdims=True)
    rest_l = jnp.where(lane == i1, neg, logits)
    v2 = jnp.max(rest_l, axis=-1, keepdims=True)
    i2 = jnp.min(jnp.where(rest_l == v2, lane, float(ROUTER_LANES)), axis=-1, keepdims=True)
    e2 = jnp.exp(v2 - v1)
    w1 = 1.0 / (1.0 + e2)
    w2 = e2 * w1
    routed = jnp.where(lane == i1, w1, jnp.where(lane == i2, w2, 0.0))
    for off, val in enumerate((i1, i2, w1, w2)):
        routed = jnp.where(lane == float(N_EXPERTS + off), val, routed)
    gate_ref[...] = routed


def _modnorm(x, g, shift_tab, scale_tab, tab, *, tm=512, router_w=None, router_b=None):
    m_total, d = x.shape
    tm = _tile(tab.group_rows, _tile(m_total, tm))
    row = pl.BlockSpec((tm, d), lambda m: (m, 0))
    vec = pl.BlockSpec((1, d), lambda m: (0, 0))
    per_group = pl.BlockSpec((None, 1, d), lambda m: (tab.index_of_row(m * tm), 0, 0))
    in_specs, args = [row, vec, per_group, per_group], [x, g.reshape(1, d), shift_tab, scale_tab]
    router = router_w is not None
    out_specs, out_shape = row, jax.ShapeDtypeStruct((m_total, d), BF16)
    est = 2 * _nbytes((tm, d), F32) * 3
    if router:
        rw = jnp.zeros((d, ROUTER_LANES), F32).at[:, :N_EXPERTS].set(router_w)
        rb = jnp.zeros((1, ROUTER_LANES), F32).at[0, :N_EXPERTS].set(router_b)
        in_specs += [pl.BlockSpec((d, ROUTER_LANES), lambda m: (0, 0)),
                     pl.BlockSpec((1, ROUTER_LANES), lambda m: (0, 0))]
        args += [rw, rb]
        out_specs = (pl.BlockSpec((tm, d // 2), lambda m: (m, 0)),
                     pl.BlockSpec((tm, ROUTER_LANES), lambda m: (m, 0)))
        out_shape = (jax.ShapeDtypeStruct((m_total, d // 2), jnp.uint32),
                     jax.ShapeDtypeStruct((m_total, ROUTER_LANES), F32))
        est += 2 * _nbytes((d, ROUTER_LANES), F32) * 4
    return pl.pallas_call(
        functools.partial(_modnorm_kernel, router=router),
        grid=(m_total // tm,), in_specs=in_specs, out_specs=out_specs, out_shape=out_shape,
        compiler_params=pltpu.CompilerParams(dimension_semantics=("parallel",),
                                             vmem_limit_bytes=_vmem_limit(est)),
        name="modnorm_router" if router else "modnorm",
    )(*args)


def _final_norm_kernel(x_ref, g_ref, o_ref):
    o_ref[...] = _rms(x_ref[...]) * g_ref[...]


def _final_norm(x, g, *, tm=256):
    m_total, d = x.shape
    tm = _tile(m_total, tm)
    row = pl.BlockSpec((tm, d), lambda m: (m, 0))
    return pl.pallas_call(
        _final_norm_kernel, grid=(m_total // tm,),
        in_specs=[row, pl.BlockSpec((1, d), lambda m: (0, 0))], out_specs=row,
        out_shape=jax.ShapeDtypeStruct((m_total, d), F32),
        compiler_params=pltpu.CompilerParams(dimension_semantics=("parallel",),
                                             vmem_limit_bytes=_vmem_limit(6 * _nbytes((tm, d), F32))),
        name="final_norm",
    )(x, g.reshape(1, d))


class PrepHead(NamedTuple):
    src_col: int
    norm: str
    scale: float


PREP_SRC_W = 512
ATTN_SCALE = HEAD_DIM ** -0.5


def _prep_plan():
    h = HEAD_DIM
    plan = [PrepHead(j * h, "q", ATTN_SCALE) for j in range(8)]
    plan += [PrepHead((8 + j) * h, "k", 1.0) for j in range(2)]
    plan += [PrepHead((20 + j) * h, "", ATTN_SCALE) for j in range(8)]
    plan += [PrepHead((28 + j) * h, "", 1.0) for j in range(8)]
    plan += [PrepHead((44 + j) * h, "", 1.0) for j in range(4)]
    plan += [PrepHead((48 + j) * h, "", ATTN_SCALE) for j in range(4)]
    return plan


def _prep_kernel(*refs, plan, src_blocks):
    src_refs = dict(zip(src_blocks, refs[:len(src_blocks)]))
    cos_ref, sin_ref, qn_ref, kn_ref, o_ref = refs[len(src_blocks):]
    cos, sin = cos_ref[...], sin_ref[...]
    lane = lax.broadcasted_iota(jnp.int32, cos.shape, 1)
    first_half = (lane % (2 * ROPE_FREQS)) < ROPE_FREQS
    gains = {"q": qn_ref[...], "k": kn_ref[...]}
    for j, head in enumerate(plan):
        blk, off = divmod(head.src_col, PREP_SRC_W)
        y = src_refs[blk][:, off:off + HEAD_DIM]
        if head.norm:
            y = _rms(y) * gains[head.norm]
        if head.scale != 1.0:
            y = y * head.scale
        partner = jnp.where(first_half, pltpu.roll(y, HEAD_DIM - ROPE_FREQS, 1), pltpu.roll(y, ROPE_FREQS, 1))
        o_ref[:, j * HEAD_DIM:(j + 1) * HEAD_DIM] = (y * cos + partner * sin).astype(o_ref.dtype)


def _prep(p, cos_t, sin_t, q_gain, k_gain, *, tm=256):
    m_total = p.shape[0]
    tm = _tile(m_total, tm)
    plan = _prep_plan()
    src_blocks = sorted({head.src_col // PREP_SRC_W for head in plan})
    row = lambda w, blk: pl.BlockSpec((tm, w), lambda m: (m, blk))
    vec = pl.BlockSpec((1, HEAD_DIM), lambda m: (0, 0))
    out_w = len(plan) * HEAD_DIM
    est = 2 * len(src_blocks) * _nbytes((tm, PREP_SRC_W), F32) + 2 * _nbytes((tm, out_w), BF16)
    return pl.pallas_call(
        functools.partial(_prep_kernel, plan=plan, src_blocks=src_blocks),
        grid=(m_total // tm,),
        in_specs=[row(PREP_SRC_W, blk) for blk in src_blocks] + [row(HEAD_DIM, 0), row(HEAD_DIM, 0), vec, vec],
        out_specs=row(out_w, 0),
        out_shape=jax.ShapeDtypeStruct((m_total, out_w), BF16),
        compiler_params=pltpu.CompilerParams(dimension_semantics=("parallel",),
                                             vmem_limit_bytes=_vmem_limit(est)),
        name="qk_prep",
    )(*([p] * len(src_blocks)), cos_t, sin_t, q_gain.reshape(1, HEAD_DIM), k_gain.reshape(1, HEAD_DIM))


def _rope_tables(t_lat, n_batch, t_ctx):
    rows = t_lat // GRID_W
    row = jnp.broadcast_to(jnp.arange(rows)[:, None], (rows, GRID_W)).reshape(-1)
    col = jnp.broadcast_to(jnp.arange(GRID_W)[None, :], (rows, GRID_W)).reshape(-1)
    inv = ROPE_BASE ** (-jnp.arange(ROPE_FREQS, dtype=F32) / ROPE_FREQS)
    ang = jnp.stack([row, col], axis=-1).astype(F32)[:, :, None] * inv
    cos, sin = jnp.cos(ang), jnp.sin(ang)
    cos128 = jnp.concatenate([cos, cos], axis=-1).reshape(t_lat, HEAD_DIM)
    sin128 = jnp.concatenate([-sin, sin], axis=-1).reshape(t_lat, HEAD_DIM)
    cos_t = jnp.concatenate([jnp.tile(cos128, (n_batch, 1)), jnp.ones((n_batch * t_ctx, HEAD_DIM), F32)])
    sin_t = jnp.concatenate([jnp.tile(sin128, (n_batch, 1)), jnp.zeros((n_batch * t_ctx, HEAD_DIM), F32)])
    return cos_t, sin_t


_NT = (((1,), (1,)), ((), ()))


def _softmax_parts(q, ks):
    ss = [lax.dot_general(q, k, _NT, preferred_element_type=F32) for k in ks]
    m = functools.reduce(jnp.maximum, [jnp.max(s, axis=-1, keepdims=True) for s in ss])
    es = [jnp.exp(s - m) for s in ss]
    l = functools.reduce(jnp.add, [jnp.sum(e, axis=-1, keepdims=True) for e in es])
    return es, l


def _gqa_kernel(q_ref, *rest, nseg):
    k_refs, v_refs, o_ref = rest[:nseg], rest[nseg:2 * nseg], rest[2 * nseg]
    ks = [r[...] for r in k_refs]
    vs = [r[...].astype(BF16) for r in v_refs]
    for g in range(A_GROUP):
        cols = slice(g * HEAD_DIM, (g + 1) * HEAD_DIM)
        es, l = _softmax_parts(q_ref[:, cols], ks)
        o = functools.reduce(jnp.add, [jnp.dot(e.astype(BF16), v, preferred_element_type=F32)
                                       for e, v in zip(es, vs)])
        o_ref[:, cols] = (o / l).astype(o_ref.dtype)


def _diff_kernel(lam_ref, q_ref, *rest, nseg, out_scale):
    k_refs, v_refs = rest[:nseg], rest[nseg:2 * nseg]
    gain_ref, o_ref = rest[2 * nseg], rest[2 * nseg + 1]
    lam = lam_ref[0]
    vs = [r[...].astype(BF16) for r in v_refs]
    maps = []
    for m in range(2):
        cols = slice(m * HEAD_DIM, (m + 1) * HEAD_DIM)
        es, l = _softmax_parts(q_ref[:, cols], [r[:, cols] for r in k_refs])
        pv = functools.reduce(jnp.add, [jnp.dot(e.astype(BF16), v, preferred_element_type=F32)
                                        for e, v in zip(es, vs)])
        maps.append(pv / l)
    o = maps[0] - lam * maps[1]
    o_ref[...] = (_rms(o) * gain_ref[...] * out_scale).astype(o_ref.dtype)


def _attention(kernel, qk, p, *, name, n_batch, t_lat, t_ctx, heads, q_w, q_blk0, k_w, k_blk0, v_w, v_blk0,
               out_w, ctx_queries, tq, extra_in=(), extra_specs=(), lead_in=(), lead_specs=()):
    n_lat = n_batch * t_lat
    ctx_row0 = n_lat // t_ctx
    tq = _tile(t_lat, tq)
    if ctx_queries:
        nq, q_rows = 1, t_ctx
        q_map = lambda b, h, i: (ctx_row0 + b, q_blk0 + h)
        out_rows, o_map = n_batch * t_ctx, (lambda b, h, i: (b, h))
        segs = [(t_ctx, lambda b: ctx_row0 + b)]
    else:
        nq, q_rows = t_lat // tq, tq
        q_map = lambda b, h, i: (b * nq + i, q_blk0 + h)
        out_rows, o_map = n_lat, (lambda b, h, i: (b * nq + i, h))
        segs = [(t_ctx, lambda b: ctx_row0 + b), (t_lat, lambda b: b)]
    k_specs = [pl.BlockSpec((rows, k_w), functools.partial(lambda rf, b, h, i: (rf(b), k_blk0 + h), rf))
               for rows, rf in segs]
    v_specs = [pl.BlockSpec((rows, v_w), functools.partial(lambda rf, b, h, i: (rf(b), v_blk0 + h), rf))
               for rows, rf in segs]
    nseg = len(segs)
    est = 8 * _nbytes((q_rows, t_lat + t_ctx), F32) + 4 * _nbytes((t_lat + t_ctx, v_w), F32)
    return pl.pallas_call(
        functools.partial(kernel, nseg=nseg),
        grid=(n_batch, heads, nq),
        in_specs=list(lead_specs) + [pl.BlockSpec((q_rows, q_w), q_map)] + k_specs + v_specs + list(extra_specs),
        out_specs=pl.BlockSpec((q_rows, out_w), o_map),
        out_shape=jax.ShapeDtypeStruct((out_rows, heads * out_w), BF16),
        compiler_params=pltpu.CompilerParams(dimension_semantics=("parallel", "parallel", "arbitrary"),
                                             vmem_limit_bytes=_vmem_limit(est)),
        name=name + ("_ctx" if ctx_queries else "_lat"),
    )(*lead_in, qk, *([qk] * nseg), *([p] * nseg), *extra_in)


def _ret_kernel(lg_ref, ql_ref, qc_ref, kl_ref, kc_ref, vl_ref, vc_ref, gl_ref, gc_ref, gain_ref,
                yl_ref, yc_ref, ol_scr, oc_scr, s_scr, *, t_lat, t_ctx):
    h = pl.program_id(1)
    lg_f, lg_b = lg_ref[0, h], lg_ref[1, h]
    c = RET_CHUNK
    pos_i = lax.broadcasted_iota(jnp.int32, (c, c), 0).astype(F32)
    pos_j = lax.broadcasted_iota(jnp.int32, (c, c), 1).astype(F32)
    rel = pos_i - pos_j
    d_both = (jnp.where(rel >= 0, jnp.exp(jnp.maximum(rel, 0.0) * lg_f), 0.0)
              + jnp.where(rel <= 0, jnp.exp(jnp.maximum(-rel, 0.0) * lg_b), 0.0))
    pos = lax.broadcasted_iota(jnp.int32, (c, 1), 0).astype(F32)
    qdec = (jnp.exp((pos + 1.0) * lg_f), jnp.exp((c - pos) * lg_b))
    kdec = (jnp.exp((c - 1.0 - pos) * lg_f), jnp.exp(pos * lg_b))
    one = jnp.ones((1, 1), F32)
    sdec = (jnp.exp(one * (c * lg_f)), jnp.exp(one * (c * lg_b)))

    chunks = [(qc_ref, kc_ref, vc_ref, oc_scr, i) for i in range(t_ctx // c)]
    chunks += [(ql_ref, kl_ref, vl_ref, ol_scr, i) for i in range(t_lat // c)]
    back = ([ch for ch in chunks if ch[0] is qc_ref][::-1] + [ch for ch in chunks if ch[0] is ql_ref][::-1])

    for d, order in ((0, chunks), (1, back)):
        s_scr[...] = jnp.zeros_like(s_scr)
        for q_ref, k_ref, v_ref, o_scr, i in order:
            rows = pl.ds(i * c, c)
            q, k = q_ref[rows, :], k_ref[rows, :]
            v = v_ref[rows, :].astype(BF16)
            state = s_scr[...]
            cross = jnp.dot((q.astype(F32) * qdec[d]).astype(BF16), state.astype(BF16),
                            preferred_element_type=F32)
            if d == 0:
                scores = lax.dot_general(q, k, _NT, preferred_element_type=F32) * d_both
                o_scr[rows, :] = cross + jnp.dot(scores.astype(BF16), v, preferred_element_type=F32)
            else:
                o_scr[rows, :] += cross
            kv = jnp.dot((k.astype(F32) * kdec[d]).T.astype(BF16), v, preferred_element_type=F32)
            s_scr[...] = sdec[d] * state + kv

    gain = gain_ref[...]
    for o_scr, g_ref, y_ref in ((ol_scr, gl_ref, yl_ref), (oc_scr, gc_ref, yc_ref)):
        g = g_ref[...]
        y_ref[...] = (_rms(o_scr[...]) * gain * (g * jax.nn.sigmoid(g))).astype(y_ref.dtype)


def _retention(qk, p, log_gamma, gain, *, n_batch, t_lat, t_ctx, heads, q_blk0, k_blk0, v_blk0, g_blk0, dv):
    n_lat = n_batch * t_lat
    ctx_row0 = n_lat // t_ctx
    lat = lambda w, blk0: pl.BlockSpec((t_lat, w), lambda b, h: (b, blk0 + h))
    ctx = lambda w, blk0: pl.BlockSpec((t_ctx, w), lambda b, h: (ctx_row0 + b, blk0 + h))
    est = 6 * _nbytes((t_lat + t_ctx, dv), F32) * 2
    return pl.pallas_call(
        functools.partial(_ret_kernel, t_lat=t_lat, t_ctx=t_ctx),
        grid=(n_batch, heads),
        in_specs=[pl.BlockSpec(memory_space=pltpu.SMEM),
                  lat(HEAD_DIM, q_blk0), ctx(HEAD_DIM, q_blk0), lat(HEAD_DIM, k_blk0), ctx(HEAD_DIM, k_blk0),
                  lat(dv, v_blk0), ctx(dv, v_blk0), lat(dv, g_blk0), ctx(dv, g_blk0),
                  pl.BlockSpec((1, dv), lambda b, h: (0, 0))],
        out_specs=(pl.BlockSpec((t_lat, dv), lambda b, h: (b, h)),
                   pl.BlockSpec((t_ctx, dv), lambda b, h: (b, h))),
        out_shape=(jax.ShapeDtypeStruct((n_lat, heads * dv), BF16),
                   jax.ShapeDtypeStruct((n_batch * t_ctx, heads * dv), BF16)),
        scratch_shapes=[pltpu.VMEM((t_lat, dv), F32), pltpu.VMEM((t_ctx, dv), F32),
                        pltpu.VMEM((HEAD_DIM, dv), F32)],
        compiler_params=pltpu.CompilerParams(dimension_semantics=("parallel", "arbitrary"),
                                             vmem_limit_bytes=_vmem_limit(est)),
        name="retention",
    )(log_gamma, qk, qk, qk, qk, p, p, p, p, gain.reshape(1, dv))


def _s5_tables(a_re, a_im, log_step, b_re, b_im, c_re, c_im):
    hp = lax.Precision.HIGHEST
    n_dir, n_grp, n_st = a_re.shape
    ln = S5_L
    tg = LANES // S5_P
    n_tile = n_grp // tg
    lam = lax.complex(a_re.astype(F32), a_im.astype(F32))
    step = jnp.exp(log_step.astype(F32))[..., None]
    a_bar = jnp.exp(lam * step)
    b_bar = ((a_bar - 1.0) / lam)[..., None] * lax.complex(b_re.astype(F32), b_im.astype(F32))
    c_mat = lax.complex(c_re.astype(F32), c_im.astype(F32))
    taus = jnp.arange(ln + 1, dtype=F32)
    apow = jnp.exp((lam * step)[..., None] * taus)

    kern = jnp.einsum("dgpn,dgnt,dgnq->dgtpq", c_mat, apow[..., :ln], b_bar, precision=hp).real
    ti = jnp.arange(ln)[:, None]
    tj = jnp.arange(ln)[None, :]
    t_f = kern[0][:, jnp.clip(ti - tj, 0)] * (ti >= tj)[None, :, :, None, None]
    t_b = kern[1][:, jnp.clip(tj - ti, 0)] * (tj >= ti)[None, :, :, None, None]
    toep = (t_f + t_b).reshape(n_tile, tg, ln, ln, S5_P, S5_P)
    m_intra = toep.transpose(0, 3, 1, 5, 2, 4).reshape(n_tile, ln, LANES, ln * S5_P)

    def in_mat(d, tau_of_j):
        w = apow[d][:, :, tau_of_j][..., None] * b_bar[d][:, :, None, :]
        wri = jnp.stack([w.real, w.imag]).reshape(2, n_tile, tg, n_st, ln, S5_P)
        return wri.transpose(1, 4, 2, 5, 0, 3).reshape(n_tile, ln, LANES, 2 * n_st)

    def out_mat(d, tau_of_i):
        w = c_mat[d][:, :, :, None] * apow[d][:, None, :, tau_of_i]
        wri = jnp.stack([w.real, -w.imag]).reshape(2, n_tile, tg, S5_P, n_st, ln)
        return wri.transpose(1, 0, 2, 4, 5, 3).reshape(n_tile, 2 * tg * n_st, ln * S5_P)

    def spread(n_x, n_y):
        r, c = jnp.arange(n_x * n_y)[:, None], jnp.arange(n_x * tg * n_y)[None, :]
        return ((r // n_y == c // (tg * n_y)) & (r % n_y == c % n_y)).astype(BF16)

    idx = jnp.arange(ln)
    fold = lambda m: m.reshape(n_tile, ln * LANES, m.shape[-1])
    mats = [fold(m_intra), fold(in_mat(0, ln - 1 - idx)), fold(in_mat(1, idx)),
            out_mat(0, idx + 1), out_mat(1, ln - idx)]
    dec = [apow[0][..., ln].real, apow[0][..., ln].imag, apow[1][..., ln].real, apow[1][..., ln].imag]
    return ([m.astype(BF16) for m in mats], [spread(ln, S5_P), spread(2, n_st)],
            [x.reshape(n_tile, 1, tg * n_st) for x in dec])


def _group_of(index, period, width):
    assert period & (period - 1) == 0 and width & (width - 1) == 0
    return lax.shift_right_logical(index & (period - 1), width.bit_length() - 1)


def _s5_kernel(u_ref, m_ref, bcf_ref, bcb_ref, ccf_ref, ccb_ref, sp_out_ref, sp_state_ref,
               dfr_ref, dfi_ref, dbr_ref, dbi_ref, y_ref, vf, vb, *, n_batch, t_lat, t_ctx):
    ln = S5_L
    n_lat = n_batch * t_lat
    n_cr, n_lr = t_ctx // ln, t_lat // ln
    n_st = vf.shape[0]
    n_re = n_st // 2
    state_w = n_re * LANES

    def widen(c_ref, sp_ref, row_period, row_width, col_period, col_width):
        wide = jnp.dot(c_ref[...], sp_ref[...], preferred_element_type=F32)
        r = lax.broadcasted_iota(jnp.int32, wide.shape, 0)
        c = lax.broadcasted_iota(jnp.int32, wide.shape, 1)
        own = _group_of(r, row_period, row_width) == _group_of(c, col_period, col_width)
        return jnp.where(own, wide, 0.0).astype(BF16)

    lane_rows, state_rows = (LANES, S5_P), (state_w, state_w // (LANES // S5_P))
    m_w = widen(m_ref, sp_out_ref, *lane_rows, *lane_rows)
    bc_w = [widen(r, sp_state_ref, *lane_rows, *state_rows) for r in (bcf_ref, bcb_ref)]
    cc_w = [widen(r, sp_out_ref, *state_rows, *lane_rows) for r in (ccf_ref, ccb_ref)]

    def chunk_tokens(b, j):
        ctx_rows = u_ref[pl.ds(n_lat + b * t_ctx + j, n_cr, stride=ln), :]
        lat_rows = u_ref[pl.ds(b * t_lat + j, n_lr, stride=ln), :]
        return jnp.concatenate([ctx_rows, lat_rows], axis=0).astype(BF16)

    def chunks(b):
        return jnp.concatenate([chunk_tokens(b, j) for j in range(ln)], axis=1)

    def mix(uc, w):
        return jnp.dot(uc, w, preferred_element_type=F32)

    def batch_rows(b):
        return pl.ds(b, n_cr + n_lr, stride=n_batch)

    for b in range(n_batch):
        uc = chunks(b)
        for v, w in zip((vf, vb), bc_w):
            contrib = mix(uc, w)
            for c in range(n_st):
                v[c, batch_rows(b), :] = contrib[:, c * LANES:(c + 1) * LANES]

    tile_rows = 8
    tile_chunks = tile_rows // n_batch
    n_ct, n_tiles = n_cr // tile_chunks, (n_cr + n_lr) // tile_chunks

    def scan(v, d_re, d_im, tile_of_step, backward):
        order = range(tile_chunks - 1, -1, -1) if backward else range(tile_chunks)

        def step(s, carry):
            rows = pl.ds(pl.multiple_of(tile_of_step(s) * tile_rows, tile_rows), tile_rows)
            new_carry = []
            for c in range(n_re):
                s_re, s_im = carry[2 * c], carry[2 * c + 1]
                t_re, t_im = v[c, rows, :], v[n_re + c, rows, :]
                dr, di = d_re[:, c * LANES:(c + 1) * LANES], d_im[:, c * LANES:(c + 1) * LANES]
                o_re, o_im = [None] * tile_chunks, [None] * tile_chunks
                for k in order:
                    sub = slice(k * n_batch, (k + 1) * n_batch)
                    o_re[k], o_im[k] = s_re, s_im
                    s_re, s_im = dr * s_re - di * s_im + t_re[sub], dr * s_im + di * s_re + t_im[sub]
                v[c, rows, :] = jnp.concatenate(o_re, axis=0)
                v[n_re + c, rows, :] = jnp.concatenate(o_im, axis=0)
                new_carry += [s_re, s_im]
            return tuple(new_carry)

        zero = jnp.zeros((n_batch, LANES), F32)
        lax.fori_loop(0, n_tiles, step, (zero,) * n_st)

    scan(vf, dfr_ref[...], dfi_ref[...], lambda s: s, False)
    scan(vb, dbr_ref[...], dbi_ref[...],
         lambda s: jnp.where(s < n_ct, n_ct - 1 - s, n_tiles + n_ct - 1 - s), True)

    def entering_state(v, b):
        return jnp.concatenate([v[c, batch_rows(b), :] for c in range(n_st)], axis=1).astype(BF16)

    for b in range(n_batch):
        y = (mix(chunks(b), m_w) + mix(entering_state(vf, b), cc_w[0]) + mix(entering_state(vb, b), cc_w[1]))
        for i in range(ln):
            lanes = slice(i * LANES, (i + 1) * LANES)
            y_ref[pl.ds(n_lat + b * t_ctx + i, n_cr, stride=ln), :] = y[:n_cr, lanes]
            y_ref[pl.ds(b * t_lat + i, n_lr, stride=ln), :] = y[n_cr:, lanes]


def _s5_core(p, col_blk0, mats, spreads, decs, *, n_batch, t_lat, t_ctx):
    rows = p.shape[0]
    n_tile = mats[0].shape[0]
    n_chunk = (t_lat + t_ctx) // S5_L
    assert 8 % n_batch == 0 and (t_ctx // S5_L) % (8 // n_batch) == 0 and (t_lat // S5_L) % (8 // n_batch) == 0
    state = spreads[1].shape[1]
    wide = S5_L * LANES
    mat_specs = [pl.BlockSpec((None,) + m.shape[1:], lambda t: (t, 0, 0)) for m in mats]
    spread_specs = [pl.BlockSpec(sp.shape, lambda t: (0, 0)) for sp in spreads]
    dec_specs = [pl.BlockSpec((None, 1, state // 2), lambda t: (t, 0, 0)) for _ in decs]
    est = (4 * _nbytes((rows, LANES), F32) + len(mats) * 2 * _nbytes((wide, wide), F32)
           + 2 * _nbytes((n_chunk * n_batch, state), F32) + 6 * _nbytes((n_chunk, state), F32))
    return pl.pallas_call(
        functools.partial(_s5_kernel, n_batch=n_batch, t_lat=t_lat, t_ctx=t_ctx),
        grid=(n_tile,),
        in_specs=([pl.BlockSpec((rows, LANES), lambda t: (0, col_blk0 + t))] + mat_specs + spread_specs
                  + dec_specs),
        out_specs=pl.BlockSpec((rows, LANES), lambda t: (0, t)),
        out_shape=jax.ShapeDtypeStruct((rows, n_tile * LANES), F32),
        scratch_shapes=[pltpu.VMEM((state // LANES, n_chunk * n_batch, LANES), F32) for _ in range(2)],
        compiler_params=pltpu.CompilerParams(dimension_semantics=("arbitrary",),
                                             vmem_limit_bytes=_vmem_limit(est)),
        name="s5_core",
    )(p, *mats, *spreads, *decs)


def _s5_finish_kernel(y_ref, ua_ref, ub_ref, d_ref, w_ref, o_ref):
    u = jnp.concatenate([ua_ref[...], ub_ref[...]], axis=1)
    z = jax.nn.gelu(y_ref[...] + d_ref[...] * u)
    gl = jnp.dot(z.astype(BF16), w_ref[...].astype(BF16), preferred_element_type=F32)
    o_ref[...] = (z * jax.nn.sigmoid(gl)).astype(o_ref.dtype)


def _s5_finish(y, p, col_half0, d_skip, w_glu, wlead, *, tm=512):
    m_total, w = y.shape
    tm = _tile(m_total, tm)
    row = pl.BlockSpec((tm, w), lambda m: (m, 0))
    nlead = len(wlead)
    return pl.pallas_call(
        _s5_finish_kernel, grid=(m_total // tm,),
        in_specs=[row, pl.BlockSpec((tm, w // 2), lambda m: (m, col_half0)),
                  pl.BlockSpec((tm, w // 2), lambda m: (m, col_half0 + 1)),
                  pl.BlockSpec((1, w), lambda m: (0, 0)),
                  pl.BlockSpec((None,) * nlead + (w, w), lambda m: tuple(wlead) + (0, 0))],
        out_specs=row, out_shape=jax.ShapeDtypeStruct((m_total, w), BF16),
        compiler_params=pltpu.CompilerParams(
            dimension_semantics=("parallel",),
            vmem_limit_bytes=_vmem_limit(8 * _nbytes((tm, w), F32) + 3 * _nbytes((w, w), F32))),
        name="s5_finish",
    )(y, p, p, d_skip.reshape(1, w), w_glu)


def _s5_branch(p, col0, bw, params, layer, *, n_batch, t_lat, t_ctx):
    mats, spreads, decs = _s5_tables(*(params[k][layer] for k in
                                       ("s5_a_re", "s5_a_im", "s5_log_step", "s5_b_re", "s5_b_im", "s5_c_re",
                                        "s5_c_im")))
    y = _s5_core(p, col0 // LANES, mats, spreads, decs, n_batch=n_batch, t_lat=t_lat, t_ctx=t_ctx)
    return _s5_finish(y, p, col0 // (bw // 2), params["s5_d"][layer], params["s5_w_glu"], (layer,))


def _merge_kernel(h_ref, *rest, n_branch):
    y_refs = rest[:n_branch]
    wg_ref, wb_ref, o_ref, acc_ref, val_ref = rest[n_branch:]
    j = pl.program_id(2)
    wb = wb_ref[...].astype(BF16)
    for idx, y_ref in enumerate(y_refs):
        @pl.when(j == idx)
        def _(y_ref=y_ref):
            val_ref[...] = jnp.dot(y_ref[...], wb, preferred_element_type=F32)

    gate = jnp.dot(h_ref[...], wg_ref[...].astype(BF16), preferred_element_type=F32)
    term = jax.nn.sigmoid(gate) * val_ref[...]

    @pl.when(j == 0)
    def _():
        acc_ref[...] = term

    @pl.when(j > 0)
    def _():
        acc_ref[...] += term

    @pl.when(j == n_branch - 1)
    def _():
        o_ref[...] = acc_ref[...].astype(o_ref.dtype)


def _merge(h, ys, w_gate, w_branch, layer, *, m_rows, tm=1024, tn=256):
    d = h.shape[1]
    tm = _tile(m_rows, tm)
    n_branch, bw = w_branch.shape[1], w_branch.shape[2]
    assert len(ys) == n_branch
    est = (2 * _nbytes((tm, d), BF16) + 2 * n_branch * _nbytes((tm, bw), BF16) + 3 * _nbytes((d, tn), F32)
           + 3 * _nbytes((bw, tn), F32) + 6 * _nbytes((tm, tn), F32))
    return pl.pallas_call(
        functools.partial(_merge_kernel, n_branch=n_branch),
        grid=(m_rows // tm, d // tn, n_branch),
        in_specs=[pl.BlockSpec((tm, d), lambda m, n, j: (m, 0))]
                 + [pl.BlockSpec((tm, bw), lambda m, n, j: (m, 0))] * n_branch
                 + [pl.BlockSpec((None, None, d, tn), lambda m, n, j: (layer, j, 0, n)),
                    pl.BlockSpec((None, None, bw, tn), lambda m, n, j: (layer, j, 0, n))],
        out_specs=pl.BlockSpec((tm, tn), lambda m, n, j: (m, n)),
        out_shape=jax.ShapeDtypeStruct((m_rows, d), BF16),
        scratch_shapes=[pltpu.VMEM((tm, tn), F32), pltpu.VMEM((tm, tn), F32)],
        compiler_params=pltpu.CompilerParams(dimension_semantics=("parallel", "parallel", "arbitrary"),
                                             vmem_limit_bytes=_vmem_limit(est)),
        name="merge",
    )(h, *ys, w_gate, w_branch)


def _swiglu_up_kernel(x_ref, w1_ref, w3_ref, o_ref):
    x = x_ref[...]
    a = jnp.dot(x, w1_ref[...].astype(BF16), preferred_element_type=F32)
    b = jnp.dot(x, w3_ref[...].astype(BF16), preferred_element_type=F32)
    o_ref[...] = (a * jax.nn.sigmoid(a) * b).astype(o_ref.dtype)


def _swiglu_up(x, w1, w3, wlead, *, tm=1024, tn=256):
    m_total, d = x.shape
    tm = _tile(m_total, tm)
    ff = w1.shape[-1]
    nlead = len(wlead)
    w_spec = pl.BlockSpec((None,) * nlead + (d, tn), lambda n, m: tuple(wlead) + (0, n))
    est = 2 * _nbytes((tm, d), BF16) + 6 * _nbytes((d, tn), F32) + 6 * _nbytes((tm, tn), F32)
    return pl.pallas_call(
        _swiglu_up_kernel,
        grid=(ff // tn, m_total // tm),
        in_specs=[pl.BlockSpec((tm, d), lambda n, m: (m, 0)), w_spec, w_spec],
        out_specs=pl.BlockSpec((tm, tn), lambda n, m: (m, n)),
        out_shape=jax.ShapeDtypeStruct((m_total, ff), BF16),
        compiler_params=pltpu.CompilerParams(dimension_semantics=("parallel", "arbitrary"),
                                             vmem_limit_bytes=_vmem_limit(est)),
        name="ffn_up",
    )(x, w1, w3)


MOE_TOP_K = 2
MOE_ROW_TILE = 1024
MOE_GATHER_ROWS = 256
MOE_COMBINE_ROWS = 128
MOE_DMA_UNROLL = 8


def _moe_routing(gates, n_tok):
    tile = MOE_ROW_TILE
    n_rows = MOE_TOP_K * n_tok + N_EXPERTS * tile
    experts = jnp.concatenate([gates[:, N_EXPERTS + k] for k in range(MOE_TOP_K)]).astype(jnp.int32)
    onehot = (experts[:, None] == jnp.arange(N_EXPERTS, dtype=jnp.int32)[None, :]).astype(jnp.int32)
    before = jnp.cumsum(onehot, axis=0) - onehot
    counts = jnp.sum(onehot, axis=0)
    padded = ((counts + tile - 1) // tile) * tile
    ends = jnp.cumsum(padded)
    dest = jnp.sum(onehot * (before + (ends - padded)[None, :]), axis=1)
    tokens = jnp.tile(jnp.arange(n_tok, dtype=jnp.int32), MOE_TOP_K)
    src = jnp.zeros((n_rows,), jnp.int32).at[dest].set(tokens)
    tile_start = jnp.arange(n_rows // tile, dtype=jnp.int32) * tile
    tile_expert = jnp.minimum(jnp.sum((tile_start[:, None] >= ends[None, :]).astype(jnp.int32), axis=1),
                              N_EXPERTS - 1)
    return src, dest, tile_expert, (ends[-1:] // tile).astype(jnp.int32)


def _row_copy(src_hbm, row, dst_vmem, slot, sem):
    return pltpu.make_async_copy(src_hbm.at[pl.ds(row, 1)], dst_vmem.at[pl.ds(slot, 1)], sem)


def _moe_gather_kernel(src_ref, nused_ref, x_hbm, o_ref, sem, *, rows, steps_per_tile):
    step = pl.program_id(0)
    used = step < nused_ref[0] * steps_per_tile

    @pl.when(used)
    def _():
        base = step * rows

        def start(r, carry):
            _row_copy(x_hbm, src_ref[base + r], o_ref, r, sem).start()
            return carry

        def wait(r, carry):
            _row_copy(x_hbm, 0, o_ref, r, sem).wait()
            return carry

        lax.fori_loop(0, rows, start, 0, unroll=MOE_DMA_UNROLL)
        lax.fori_loop(0, rows, wait, 0, unroll=MOE_DMA_UNROLL)

    @pl.when(jnp.logical_not(used))
    def _():
        o_ref[...] = jnp.zeros_like(o_ref)


def _moe_gather(x, src, n_used):
    n_rows, w = src.shape[0], x.shape[1]
    rows = MOE_GATHER_ROWS
    return pl.pallas_call(
        functools.partial(_moe_gather_kernel, rows=rows, steps_per_tile=MOE_ROW_TILE // rows),
        grid_spec=pltpu.PrefetchScalarGridSpec(
            num_scalar_prefetch=2, grid=(n_rows // rows,),
            in_specs=[pl.BlockSpec(memory_space=pl.ANY)],
            out_specs=pl.BlockSpec((rows, w), lambda i, s, nu: (i, 0)),
            scratch_shapes=[pltpu.SemaphoreType.DMA]),
        out_shape=jax.ShapeDtypeStruct((n_rows, w), x.dtype),
        compiler_params=pltpu.CompilerParams(
            dimension_semantics=("arbitrary",),
            vmem_limit_bytes=_vmem_limit(2 * _nbytes((rows, w), x.dtype))),
        name="moe_gather",
    )(src, n_used, x)


def _moe_up_kernel(te_ref, nused_ref, x_ref, w1_ref, w3_ref, o_ref):
    used = pl.program_id(1) < nused_ref[0]

    @pl.when(used)
    def _():
        half = w1_ref.shape[0] // 2
        x_lo, x_hi = (v.astype(BF16) for v in _unpack_bf16_pairs(x_ref[...]))

        def proj(w_ref):
            return (jnp.dot(x_lo, w_ref[:half, :].astype(BF16), preferred_element_type=F32)
                    + jnp.dot(x_hi, w_ref[half:, :].astype(BF16), preferred_element_type=F32))

        a, b = proj(w1_ref), proj(w3_ref)
        o_ref[...] = (a * jax.nn.sigmoid(a) * b).astype(o_ref.dtype)

    @pl.when(jnp.logical_not(used))
    def _():
        o_ref[...] = jnp.zeros_like(o_ref)


def _moe_down_kernel(te_ref, nused_ref, u_ref, w2_ref, o_ref):
    used = pl.program_id(1) < nused_ref[0]

    @pl.when(used)
    def _():
        y = jnp.dot(u_ref[...], w2_ref[...].astype(BF16), preferred_element_type=F32)
        o_ref[...] = _pack_bf16_pairs(y)

    @pl.when(jnp.logical_not(used))
    def _():
        o_ref[...] = jnp.zeros_like(o_ref)


def _moe_grouped(kernel, x, ws, layer, tile_expert, n_used, *, tn, out_tn, out_dtype, name):
    n_rows, x_w = x.shape
    kdim, n_out = ws[0].shape[-2:]
    tm = MOE_ROW_TILE

    def row_blk(m, nu):
        return jnp.minimum(m, nu[0] - 1)

    w_spec = pl.BlockSpec((None, None, kdim, tn), lambda n, m, te, nu: (layer, te[row_blk(m, nu)], 0, n))
    est = (2 * _nbytes((tm, x_w), x.dtype) + len(ws) * 3 * _nbytes((kdim, tn), F32)
           + (3 + len(ws)) * _nbytes((tm, tn), F32) + 2 * _nbytes((tm, kdim), BF16))
    return pl.pallas_call(
        kernel,
        grid_spec=pltpu.PrefetchScalarGridSpec(
            num_scalar_prefetch=2, grid=(n_out // tn, n_rows // tm),
            in_specs=[pl.BlockSpec((tm, x_w), lambda n, m, te, nu: (row_blk(m, nu), 0))] + [w_spec] * len(ws),
            out_specs=pl.BlockSpec((tm, out_tn), lambda n, m, te, nu: (m, n))),
        out_shape=jax.ShapeDtypeStruct((n_rows, n_out // tn * out_tn), out_dtype),
        compiler_params=pltpu.CompilerParams(dimension_semantics=("parallel", "arbitrary"),
                                             vmem_limit_bytes=_vmem_limit(est)),
        name=name,
    )(tile_expert, n_used, x, *ws)


def _moe_combine_kernel(dest_ref, y_hbm, xs_ref, route_ref, gate_ref, *rest, rows, n_tok, pack_w, final_norm):
    final_gain_ref = rest[0] if final_norm else None
    o_ref, buf, sem = rest[-3:]
    base = pl.program_id(0) * rows
    for k in range(MOE_TOP_K):
        def start(r, carry, k=k):
            _row_copy(y_hbm, dest_ref[k * n_tok + base + r], buf.at[k], r, sem).start()
            return carry

        lax.fori_loop(0, rows, start, 0, unroll=MOE_DMA_UNROLL)
    for k in range(MOE_TOP_K):
        def wait(r, carry, k=k):
            _row_copy(y_hbm, 0, buf.at[k], r, sem).wait()
            return carry

        lax.fori_loop(0, rows, wait, 0, unroll=MOE_DMA_UNROLL)
    route = route_ref[...]
    w_lane0 = N_EXPERTS + MOE_TOP_K

    def expert_rows(k):
        halves = [_unpack_bf16_pairs(buf[k, :, c:c + pack_w]) for c in range(0, buf.shape[2], pack_w)]
        return jnp.concatenate([h for pair in halves for h in pair], axis=1)

    mix = functools.reduce(jnp.add, [route[:, w_lane0 + k:w_lane0 + k + 1] * expert_rows(k)
                                     for k in range(MOE_TOP_K)])
    out = xs_ref[...] + gate_ref[...] * mix
    o_ref[...] = _rms(out) * final_gain_ref[...] if final_norm else out


def _moe_combine(xs, y_rows, pack_w, dest, route, gate, tab, final_gain=None):
    n_tok, d = xs.shape
    rows = _tile(tab.group_rows, MOE_COMBINE_ROWS)
    row = lambda w: pl.BlockSpec((rows, w), lambda i, dst: (i, 0))
    est = 8 * _nbytes((rows, d), F32)
    final_norm = final_gain is not None
    extra_specs = [pl.BlockSpec((1, d), lambda i, dst: (0, 0))] if final_norm else []
    extra_args = [final_gain.reshape(1, d)] if final_norm else []
    return pl.pallas_call(
        functools.partial(_moe_combine_kernel, rows=rows, n_tok=n_tok, pack_w=pack_w, final_norm=final_norm),
        grid_spec=pltpu.PrefetchScalarGridSpec(
            num_scalar_prefetch=1, grid=(n_tok // rows,),
            in_specs=[pl.BlockSpec(memory_space=pl.ANY), row(d), row(route.shape[1]),
                      pl.BlockSpec((None, 1, d), lambda i, dst: (tab.index_of_row(i * rows), 0, 0))]
                     + extra_specs,
            out_specs=row(d),
            scratch_shapes=[pltpu.VMEM((MOE_TOP_K, rows, y_rows.shape[1]), y_rows.dtype),
                            pltpu.SemaphoreType.DMA]),
        out_shape=jax.ShapeDtypeStruct((n_tok, d), F32),
        compiler_params=pltpu.CompilerParams(dimension_semantics=("arbitrary",),
                                             vmem_limit_bytes=_vmem_limit(est)),
        name="moe_combine",
    )(dest, y_rows, xs, route, gate, *extra_args)


def _moe_sparse(xs, h2p, route, w1, w3, w2, layer, gate, tab, final_gain=None):
    n_tok = xs.shape[0]
    src, dest, tile_expert, n_used = _moe_routing(route, n_tok)
    xg = _moe_gather(h2p, src, n_used)
    up_tn, down_tn = 256, 1024
    ug = _moe_grouped(_moe_up_kernel, xg, (w1, w3), layer, tile_expert, n_used, tn=up_tn, out_tn=up_tn,
                      out_dtype=BF16, name="moe_up")
    yg = _moe_grouped(_moe_down_kernel, ug, (w2,), layer, tile_expert, n_used, tn=down_tn, out_tn=down_tn // 2,
                      out_dtype=jnp.uint32, name="moe_down")
    return _moe_combine(xs, yg, down_tn // 2, dest, route, gate, tab, final_gain)


def kernel(x, c, ctx, c_ctx, ada_w, ada_b, norm1_g, norm2_g, w_in, attn_q_norm, attn_k_norm, s5_a_re, s5_a_im, s5_log_step, s5_b_re, s5_b_im, s5_c_re, s5_c_im, s5_d, s5_w_glu, diff_lambda, diff_norm, ret_decay_logit, ret_norm, w_branch, w_merge_gate, w_out, ffn_w1, ffn_w3, ffn_w2, moe_router_w, moe_router_b, moe_w1, moe_w3, moe_w2, final_norm_g):
    n_batch, t_lat, d = x.shape
    t_ctx = ctx.shape[1]
    depth = w_in.shape[0]
    n_lat, n_ctx = n_batch * t_lat, n_batch * t_ctx
    bw = d // 4
    s5_params = dict(s5_a_re=s5_a_re, s5_a_im=s5_a_im, s5_log_step=s5_log_step, s5_b_re=s5_b_re,
                     s5_b_im=s5_b_im, s5_c_re=s5_c_re, s5_c_im=s5_c_im, s5_d=s5_d, s5_w_glu=s5_w_glu)

    tab = RowTable(lambda row: jnp.where(row < n_lat, row // t_lat, n_batch), t_lat)

    cos_t, sin_t = _rope_tables(t_lat, n_batch, t_ctx)
    ada_rows = 8
    c_all = jnp.zeros((ada_rows, d), F32).at[:n_batch].set(c).at[n_batch].set(c_ctx)
    xs = jnp.concatenate([x.reshape(n_lat, d), ctx.reshape(n_ctx, d)])

    for i in range(depth):
        need_ctx = i < depth - 1
        lam_init = 0.8 - 0.6 * math.exp(-0.3 * i)
        mod = _mm(c_all, ada_w, (i,), name="ada_mod", tm=ada_rows, tn=512, epi="bias", x_silu=True,
                  bias=ada_b[i].reshape(1, -1))
        tabs = [mod[:n_batch + 1, k * d:(k + 1) * d].reshape(n_batch + 1, 1, d) for k in range(6)]
        shift1, scale1, gate1, shift2, scale2, gate2 = tabs

        h = _modnorm(xs, norm1_g[i], shift1, scale1, tab)
        p = _mm(h, w_in, (i,), name="in_proj", tm=1024, tn=512)
        qk = _prep(p, cos_t, sin_t, attn_q_norm[i], attn_k_norm[i])

        att = functools.partial(_attention, qk=qk, p=p, n_batch=n_batch, t_lat=t_lat, t_ctx=t_ctx, tq=512)
        gqa = functools.partial(att, _gqa_kernel, name="gqa", heads=2,
                                q_w=4 * HEAD_DIM, q_blk0=0, k_w=HEAD_DIM, k_blk0=8, v_w=HEAD_DIM, v_blk0=10,
                                out_w=4 * HEAD_DIM)
        lp = diff_lambda[i].astype(F32)
        lam = (jnp.exp(jnp.sum(lp[0] * lp[1])) - jnp.exp(jnp.sum(lp[2] * lp[3])) + lam_init).reshape(1)
        dgain = diff_norm[i].reshape(1, 2 * HEAD_DIM)
        dif = functools.partial(att, functools.partial(_diff_kernel, out_scale=1.0 - lam_init),
                                name="diff",
                                heads=4, q_w=2 * HEAD_DIM, q_blk0=5, k_w=2 * HEAD_DIM, k_blk0=9,
                                v_w=2 * HEAD_DIM, v_blk0=18, out_w=2 * HEAD_DIM,
                                lead_in=(lam,), lead_specs=(pl.BlockSpec(memory_space=pltpu.SMEM),),
                                extra_in=(dgain,),
                                extra_specs=(pl.BlockSpec((1, 2 * HEAD_DIM), lambda b, hh, ii: (0, 0)),))
        ya_l, yc_l = gqa(ctx_queries=False), dif(ctx_queries=False)
        log_gamma = jax.nn.log_sigmoid(ret_decay_logit[i].astype(F32))
        yd_l, yd_c = _retention(qk, p, log_gamma, ret_norm[i], n_batch=n_batch, t_lat=t_lat, t_ctx=t_ctx,
                                heads=4, q_blk0=26, k_blk0=30, v_blk0=26, g_blk0=30, dv=2 * HEAD_DIM)
        s5_col0 = 12 * HEAD_DIM
        yb = _s5_branch(p, s5_col0, bw, s5_params, i, n_batch=n_batch, t_lat=t_lat, t_ctx=t_ctx)
        if need_ctx:
            ya = jnp.concatenate([ya_l, gqa(ctx_queries=True)])
            yc = jnp.concatenate([yc_l, dif(ctx_queries=True)])
            yd = jnp.concatenate([yd_l, yd_c])
            rows = n_lat + n_ctx
        else:
            ya, yc, yd = ya_l, yc_l, yd_l
            rows = n_lat
        acc = _merge(h, (ya, yb, yc, yd), w_merge_gate, w_branch, i, m_rows=rows)
        xs = _mm(acc, w_out, (i,), name="out_proj", tm=1024, tn=512, epi="resgate", res=xs, gate=gate1,
                 tab=tab, m_rows=rows)

        if i % 2 == 0:
            j = i // 2
            h2 = _modnorm(xs, norm2_g[i], shift2, scale2, tab)
            u = _swiglu_up(h2, ffn_w1, ffn_w3, (j,))
            half = u.shape[1] // 2
            for kc in range(2):
                xs = _mm(u, ffn_w2, (j,), name="ffn_down", tm=512, tn=512, tk=half, k0=kc, epi="resgate",
                         res=xs, gate=gate2, tab=tab)
        else:
            j = i // 2
            h2, route = _modnorm(xs, norm2_g[i], shift2, scale2, tab, tm=256,
                                 router_w=moe_router_w[j], router_b=moe_router_b[j])
            closing = final_norm_g if (i == depth - 1 and xs.shape[0] == n_lat) else None
            xs = _moe_sparse(xs, h2, route, moe_w1, moe_w3, moe_w2, j, gate2, tab, closing)
            if closing is not None:
                return xs.reshape(n_batch, t_lat, d)

    return _final_norm(xs[:n_lat], final_norm_g).reshape(n_batch, t_lat, d)
```

```python
import functools
import math
from typing import Callable, NamedTuple

import jax
import jax.numpy as jnp
from jax import lax
from jax.experimental import pallas as pl
from jax.experimental.pallas import tpu as pltpu

F32 = jnp.float32
BF16 = jnp.bfloat16

HEAD_DIM = 128
GRID_W = 64
ROPE_BASE = 10000.0
ROPE_FREQS = HEAD_DIM // 4
EPS = 1e-6
A_GROUP = 4
S5_P = 16
S5_N = 64
S5_L = 8
LANES = 128
RET_CHUNK = 128
N_EXPERTS = 8
ROUTER_LANES = 128
V7X_VMEM_BYTES = 64 * 1024 * 1024
VMEM_HEADROOM_BYTES = 8 * 1024 * 1024


def _vmem_limit(nbytes):
    return int(min(nbytes + VMEM_HEADROOM_BYTES, V7X_VMEM_BYTES - VMEM_HEADROOM_BYTES))


def _nbytes(shape, dtype):
    return math.prod(shape) * jnp.dtype(dtype).itemsize


class RowTable(NamedTuple):
    index_of_row: Callable
    group_rows: int


def _tile(total, preferred):
    return math.gcd(total, preferred)


def _mm_kernel(*refs, x_silu, epi):
    it = iter(refs)
    x_ref, w_ref = next(it), next(it)
    bias_ref = next(it) if epi == "bias" else None
    res_ref, gate_ref = (next(it), next(it)) if epi == "resgate" else (None, None)
    o_ref = next(it)

    x = x_ref[...]
    if x_silu:
        x = x * jax.nn.sigmoid(x)
    acc = jnp.dot(x.astype(BF16), w_ref[...].astype(BF16), preferred_element_type=F32)
    if epi == "bias":
        acc = acc + bias_ref[...]
    elif epi == "resgate":
        acc = res_ref[...] + gate_ref[...] * acc
    o_ref[...] = acc.astype(o_ref.dtype)


def _mm(x, w, wlead=(), *, name, m_rows=None, tm, tn, tk=None, k0=0, epi="none",
        out_dtype=F32, x_silu=False, bias=None, res=None, gate=None, tab=None):
    m_total = x.shape[0] if m_rows is None else m_rows
    n_total = w.shape[-1]
    tk = x.shape[1] if tk is None else tk
    tm, tn = _tile(m_total, tm), _tile(n_total, tn)
    if tab is not None:
        tm = _tile(tab.group_rows, tm)
    nlead = len(wlead)
    in_specs = [
        pl.BlockSpec((tm, tk), lambda n, m: (m, k0)),
        pl.BlockSpec((None,) * nlead + (tk, tn), lambda n, m: tuple(wlead) + (k0, n)),
    ]
    args = [x, w]
    est = 2 * _nbytes((tm, tk), x.dtype) + 2 * _nbytes((tk, tn), w.dtype) + _nbytes((tk, tn), BF16)
    if epi == "bias":
        in_specs.append(pl.BlockSpec((1, tn), lambda n, m: (0, n)))
        args.append(bias)
    elif epi == "resgate":
        in_specs.append(pl.BlockSpec((tm, tn), lambda n, m: (m, n)))
        in_specs.append(pl.BlockSpec((None, 1, tn), lambda n, m: (tab.index_of_row(m * tm), 0, n)))
        args += [res, gate]
        est += 2 * _nbytes((tm, tn), F32)
    est += 2 * _nbytes((tm, tn), out_dtype) + 2 * _nbytes((tm, tn), F32)
    return pl.pallas_call(
        functools.partial(_mm_kernel, x_silu=x_silu, epi=epi),
        grid=(n_total // tn, m_total // tm),
        in_specs=in_specs,
        out_specs=pl.BlockSpec((tm, tn), lambda n, m: (m, n)),
        out_shape=jax.ShapeDtypeStruct((m_total, n_total), out_dtype),
        compiler_params=pltpu.CompilerParams(
            dimension_semantics=("parallel", "arbitrary"),
            vmem_limit_bytes=_vmem_limit(est)),
        name=name,
    )(*args)


def _rms(x):
    return x * lax.rsqrt(jnp.mean(x * x, axis=-1, keepdims=True) + EPS)


def _pack_bf16_pairs(x):
    w = x.shape[1] // 2
    as_bits = lambda v: lax.bitcast_convert_type(v.astype(BF16).astype(F32), jnp.uint32)
    return (as_bits(x[:, w:]) & jnp.uint32(0xFFFF0000)) | (as_bits(x[:, :w]) >> 16)


def _unpack_bf16_pairs(p):
    low = lax.bitcast_convert_type(p << 16, F32)
    high = lax.bitcast_convert_type(p & jnp.uint32(0xFFFF0000), F32)
    return low, high


def _modnorm_kernel(x_ref, g_ref, sh_ref, sc_ref, *rest, router):
    h = _rms(x_ref[...]) * g_ref[...] * (1.0 + sc_ref[...]) + sh_ref[...]
    if not router:
        (o_ref,) = rest
        o_ref[...] = h.astype(o_ref.dtype)
        return
    rw_ref, rb_ref, o_ref, gate_ref = rest
    o_ref[...] = _pack_bf16_pairs(h)
    logits = jnp.dot(h, rw_ref[...], preferred_element_type=F32,
                     precision=lax.Precision.HIGHEST) + rb_ref[...]
    lane = lax.broadcasted_iota(jnp.int32, logits.shape, 1).astype(F32)
    neg = jnp.float32(-jnp.inf)
    logits = jnp.where(lane < N_EXPERTS, logits, neg)
    v1 = jnp.max(logits, axis=-1, keepdims=True)
    i1 = jnp.min(jnp.where(logits == v1, lane, float(ROUTER_LANES)), axis=-1, keepdims=True)
    rest_l = jnp.where(lane == i1, neg, logits)
    v2 = jnp.max(rest_l, axis=-1, keepdims=True)
    i2 = jnp.min(jnp.where(rest_l == v2, lane, float(ROUTER_LANES)), axis=-1, keepdims=True)
    e2 = jnp.exp(v2 - v1)
    w1 = 1.0 / (1.0 + e2)
    w2 = e2 * w1
    routed = jnp.where(lane == i1, w1, jnp.where(lane == i2, w2, 0.0))
    for off, val in enumerate((i1, i2, w1, w2)):
        routed = jnp.where(lane == float(N_EXPERTS + off), val, routed)
    gate_ref[...] = routed


def _modnorm(x, g, shift_tab, scale_tab, tab, *, tm=512, router_w=None, router_b=None):
    m_total, d = x.shape
    tm = _tile(tab.group_rows, _tile(m_total, tm))
    row = pl.BlockSpec((tm, d), lambda m: (m, 0))
    vec = pl.BlockSpec((1, d), lambda m: (0, 0))
    per_group = pl.BlockSpec((None, 1, d), lambda m: (tab.index_of_row(m * tm), 0, 0))
    in_specs, args = [row, vec, per_group, per_group], [x, g.reshape(1, d), shift_tab, scale_tab]
    router = router_w is not None
    out_specs, out_shape = row, jax.ShapeDtypeStruct((m_total, d), BF16)
    est = 2 * _nbytes((tm, d), F32) * 3
    if router:
        rw = jnp.zeros((d, ROUTER_LANES), F32).at[:, :N_EXPERTS].set(router_w)
        rb = jnp.zeros((1, ROUTER_LANES), F32).at[0, :N_EXPERTS].set(router_b)
        in_specs += [pl.BlockSpec((d, ROUTER_LANES), lambda m: (0, 0)),
                     pl.BlockSpec((1, ROUTER_LANES), lambda m: (0, 0))]
        args += [rw, rb]
        out_specs = (pl.BlockSpec((tm, d // 2), lambda m: (m, 0)),
                     pl.BlockSpec((tm, ROUTER_LANES), lambda m: (m, 0)))
        out_shape = (jax.ShapeDtypeStruct((m_total, d // 2), jnp.uint32),
                     jax.ShapeDtypeStruct((m_total, ROUTER_LANES), F32))
        est += 2 * _nbytes((d, ROUTER_LANES), F32) * 4
    return pl.pallas_call(
        functools.partial(_modnorm_kernel, router=router),
        grid=(m_total // tm,), in_specs=in_specs, out_specs=out_specs, out_shape=out_shape,
        compiler_params=pltpu.CompilerParams(dimension_semantics=("parallel",),
                                             vmem_limit_bytes=_vmem_limit(est)),
        name="modnorm_router" if router else "modnorm",
    )(*args)


def _final_norm_kernel(x_ref, g_ref, o_ref):
    o_ref[...] = _rms(x_ref[...]) * g_ref[...]


def _final_norm(x, g, *, tm=256):
    m_total, d = x.shape
    tm = _tile(m_total, tm)
    row = pl.BlockSpec((tm, d), lambda m: (m, 0))
    return pl.pallas_call(
        _final_norm_kernel, grid=(m_total // tm,),
        in_specs=[row, pl.BlockSpec((1, d), lambda m: (0, 0))], out_specs=row,
        out_shape=jax.ShapeDtypeStruct((m_total, d), F32),
        compiler_params=pltpu.CompilerParams(dimension_semantics=("parallel",),
                                             vmem_limit_bytes=_vmem_limit(6 * _nbytes((tm, d), F32))),
        name="final_norm",
    )(x, g.reshape(1, d))


class PrepHead(NamedTuple):
    src_col: int
    norm: str
    scale: float


PREP_SRC_W = 512
ATTN_SCALE = HEAD_DIM ** -0.5


def _prep_plan():
    h = HEAD_DIM
    plan = [PrepHead(j * h, "q", ATTN_SCALE) for j in range(8)]
    plan += [PrepHead((8 + j) * h, "k", 1.0) for j in range(2)]
    plan += [PrepHead((20 + j) * h, "", ATTN_SCALE) for j in range(8)]
    plan += [PrepHead((28 + j) * h, "", 1.0) for j in range(8)]
    plan += [PrepHead((44 + j) * h, "", 1.0) for j in range(4)]
    plan += [PrepHead((48 + j) * h, "", ATTN_SCALE) for j in range(4)]
    return plan


def _prep_kernel(*refs, plan, src_blocks):
    src_refs = dict(zip(src_blocks, refs[:len(src_blocks)]))
    cos_ref, sin_ref, qn_ref, kn_ref, o_ref = refs[len(src_blocks):]
    cos, sin = cos_ref[...], sin_ref[...]
    lane = lax.broadcasted_iota(jnp.int32, cos.shape, 1)
    first_half = (lane % (2 * ROPE_FREQS)) < ROPE_FREQS
    gains = {"q": qn_ref[...], "k": kn_ref[...]}
    for j, head in enumerate(plan):
        blk, off = divmod(head.src_col, PREP_SRC_W)
        y = src_refs[blk][:, off:off + HEAD_DIM]
        if head.norm:
            y = _rms(y) * gains[head.norm]
        if head.scale != 1.0:
            y = y * head.scale
        partner = jnp.where(first_half, pltpu.roll(y, HEAD_DIM - ROPE_FREQS, 1), pltpu.roll(y, ROPE_FREQS, 1))
        o_ref[:, j * HEAD_DIM:(j + 1) * HEAD_DIM] = (y * cos + partner * sin).astype(o_ref.dtype)


def _prep(p, cos_t, sin_t, q_gain, k_gain, *, tm=256):
    m_total = p.shape[0]
    tm = _tile(m_total, tm)
    plan = _prep_plan()
    src_blocks = sorted({head.src_col // PREP_SRC_W for head in plan})
    row = lambda w, blk: pl.BlockSpec((tm, w), lambda m: (m, blk))
    vec = pl.BlockSpec((1, HEAD_DIM), lambda m: (0, 0))
    out_w = len(plan) * HEAD_DIM
    est = 2 * len(src_blocks) * _nbytes((tm, PREP_SRC_W), F32) + 2 * _nbytes((tm, out_w), BF16)
    return pl.pallas_call(
        functools.partial(_prep_kernel, plan=plan, src_blocks=src_blocks),
        grid=(m_total // tm,),
        in_specs=[row(PREP_SRC_W, blk) for blk in src_blocks] + [row(HEAD_DIM, 0), row(HEAD_DIM, 0), vec, vec],
        out_specs=row(out_w, 0),
        out_shape=jax.ShapeDtypeStruct((m_total, out_w), BF16),
        compiler_params=pltpu.CompilerParams(dimension_semantics=("parallel",),
                                             vmem_limit_bytes=_vmem_limit(est)),
        name="qk_prep",
    )(*([p] * len(src_blocks)), cos_t, sin_t, q_gain.reshape(1, HEAD_DIM), k_gain.reshape(1, HEAD_DIM))


def _rope_tables(t_lat, n_batch, t_ctx):
    rows = t_lat // GRID_W
    row = jnp.broadcast_to(jnp.arange(rows)[:, None], (rows, GRID_W)).reshape(-1)
    col = jnp.broadcast_to(jnp.arange(GRID_W)[None, :], (rows, GRID_W)).reshape(-1)
    inv = ROPE_BASE ** (-jnp.arange(ROPE_FREQS, dtype=F32) / ROPE_FREQS)
    ang = jnp.stack([row, col], axis=-1).astype(F32)[:, :, None] * inv
    cos, sin = jnp.cos(ang), jnp.sin(ang)
    cos128 = jnp.concatenate([cos, cos], axis=-1).reshape(t_lat, HEAD_DIM)
    sin128 = jnp.concatenate([-sin, sin], axis=-1).reshape(t_lat, HEAD_DIM)
    cos_t = jnp.concatenate([jnp.tile(cos128, (n_batch, 1)), jnp.ones((n_batch * t_ctx, HEAD_DIM), F32)])
    sin_t = jnp.concatenate([jnp.tile(sin128, (n_batch, 1)), jnp.zeros((n_batch * t_ctx, HEAD_DIM), F32)])
    return cos_t, sin_t


_NT = (((1,), (1,)), ((), ()))


def _softmax_parts(q, ks):
    ss = [lax.dot_general(q, k, _NT, preferred_element_type=F32) for k in ks]
    m = functools.reduce(jnp.maximum, [jnp.max(s, axis=-1, keepdims=True) for s in ss])
    es = [jnp.exp(s - m) for s in ss]
    l = functools.reduce(jnp.add, [jnp.sum(e, axis=-1, keepdims=True) for e in es])
    return es, l


def _gqa_kernel(q_ref, *rest, nseg):
    k_refs, v_refs, o_ref = rest[:nseg], rest[nseg:2 * nseg], rest[-1]
    ks = [r[...] for r in k_refs]
    vs = [r[...].astype(BF16) for r in v_refs]
    for g in range(A_GROUP):
        cols = slice(g * HEAD_DIM, (g + 1) * HEAD_DIM)
        es, l = _softmax_parts(q_ref[:, cols], ks)
        o = functools.reduce(jnp.add, [jnp.dot(e.astype(BF16), v, preferred_element_type=F32)
                                       for e, v in zip(es, vs)])
        o_ref[:, cols] = (o / l).astype(o_ref.dtype)


def _diff_kernel(lam_ref, q_ref, *rest, nseg, out_scale):
    k_refs, v_refs = rest[:nseg], rest[nseg:2 * nseg]
    gain_ref, o_ref = rest[2 * nseg], rest[-1]
    lam = lam_ref[0]
    vs = [r[...].astype(BF16) for r in v_refs]
    maps = []
    for m in range(2):
        cols = slice(m * HEAD_DIM, (m + 1) * HEAD_DIM)
        es, l = _softmax_parts(q_ref[:, cols], [r[:, cols] for r in k_refs])
        pv = functools.reduce(jnp.add, [jnp.dot(e.astype(BF16), v, preferred_element_type=F32)
                                        for e, v in zip(es, vs)])
        maps.append(pv / l)
    o = maps[0] - lam * maps[1]
    o_ref[...] = (_rms(o) * gain_ref[...] * out_scale).astype(o_ref.dtype)


def _attention(kernel, qk, p, *, name, n_batch, t_lat, t_ctx, heads, q_w, q_blk0, k_w, k_blk0, v_w, v_blk0,
               out_w, out_blk0, y_buf, ctx_queries, tq, extra_in=(), extra_specs=(), lead_in=(), lead_specs=()):
    n_lat = n_batch * t_lat
    ctx_row0 = n_lat // t_ctx
    tq = _tile(t_lat, tq)
    if ctx_queries:
        nq, q_rows = 1, t_ctx
        q_map = lambda b, h, i: (ctx_row0 + b, q_blk0 + h)
        o_map = lambda b, h, i: (ctx_row0 + b, out_blk0 + h)
        segs = [(t_ctx, lambda b: ctx_row0 + b)]
    else:
        nq, q_rows = t_lat // tq, tq
        q_map = lambda b, h, i: (b * nq + i, q_blk0 + h)
        o_map = lambda b, h, i: (b * nq + i, out_blk0 + h)
        segs = [(t_ctx, lambda b: ctx_row0 + b), (t_lat, lambda b: b)]
    k_specs = [pl.BlockSpec((rows, k_w), functools.partial(lambda rf, b, h, i: (rf(b), k_blk0 + h), rf))
               for rows, rf in segs]
    v_specs = [pl.BlockSpec((rows, v_w), functools.partial(lambda rf, b, h, i: (rf(b), v_blk0 + h), rf))
               for rows, rf in segs]
    nseg = len(segs)
    est = 8 * _nbytes((q_rows, t_lat + t_ctx), F32) + 4 * _nbytes((t_lat + t_ctx, v_w), F32)
    return pl.pallas_call(
        functools.partial(kernel, nseg=nseg),
        grid=(n_batch, heads, nq),
        in_specs=(list(lead_specs) + [pl.BlockSpec((q_rows, q_w), q_map)] + k_specs + v_specs + list(extra_specs)
                  + [pl.BlockSpec(memory_space=pl.ANY)]),
        out_specs=pl.BlockSpec((q_rows, out_w), o_map),
        out_shape=jax.ShapeDtypeStruct(y_buf.shape, y_buf.dtype),
        input_output_aliases={len(lead_in) + 1 + 2 * nseg + len(extra_in): 0},
        compiler_params=pltpu.CompilerParams(dimension_semantics=("parallel", "parallel", "arbitrary"),
                                             vmem_limit_bytes=_vmem_limit(est)),
        name=name + ("_ctx" if ctx_queries else "_lat"),
    )(*lead_in, qk, *([qk] * nseg), *([p] * nseg), *extra_in, y_buf)


def _ret_kernel(lg_ref, ql_ref, qc_ref, kl_ref, kc_ref, vl_ref, vc_ref, gl_ref, gc_ref, gain_ref, y_buf_ref,
                y_ref, ol_scr, oc_scr, s_scr, *, t_lat, t_ctx, n_lat, write_ctx):
    h, b = pl.program_id(0), pl.program_id(1)
    lg_f, lg_b = lg_ref[0, h], lg_ref[1, h]
    c = RET_CHUNK
    pos_i = lax.broadcasted_iota(jnp.int32, (c, c), 0).astype(F32)
    pos_j = lax.broadcasted_iota(jnp.int32, (c, c), 1).astype(F32)
    rel = pos_i - pos_j
    d_both = (jnp.where(rel >= 0, jnp.exp(jnp.maximum(rel, 0.0) * lg_f), 0.0)
              + jnp.where(rel <= 0, jnp.exp(jnp.maximum(-rel, 0.0) * lg_b), 0.0))
    pos = lax.broadcasted_iota(jnp.int32, (c, 1), 0).astype(F32)
    qdec = (jnp.exp((pos + 1.0) * lg_f), jnp.exp((c - pos) * lg_b))
    kdec = (jnp.exp((c - 1.0 - pos) * lg_f), jnp.exp(pos * lg_b))
    one = jnp.ones((1, 1), F32)
    sdec = (jnp.exp(one * (c * lg_f)), jnp.exp(one * (c * lg_b)))

    chunks = [(qc_ref, kc_ref, vc_ref, oc_scr, i) for i in range(t_ctx // c)]
    chunks += [(ql_ref, kl_ref, vl_ref, ol_scr, i) for i in range(t_lat // c)]
    back = ([ch for ch in chunks if ch[0] is qc_ref][::-1] + [ch for ch in chunks if ch[0] is ql_ref][::-1])

    for d, order in ((0, chunks), (1, back)):
        s_scr[...] = jnp.zeros_like(s_scr)
        for q_ref, k_ref, v_ref, o_scr, i in order:
            rows = pl.ds(i * c, c)
            q, k = q_ref[rows, :], k_ref[rows, :]
            v = v_ref[rows, :].astype(BF16)
            state = s_scr[...]
            cross = jnp.dot((q.astype(F32) * qdec[d]).astype(BF16), state.astype(BF16),
                            preferred_element_type=F32)
            if d == 0:
                scores = lax.dot_general(q, k, _NT, preferred_element_type=F32) * d_both
                o_scr[rows, :] = cross + jnp.dot(scores.astype(BF16), v, preferred_element_type=F32)
            else:
                o_scr[rows, :] += cross
            kv = jnp.dot((k.astype(F32) * kdec[d]).T.astype(BF16), v, preferred_element_type=F32)
            s_scr[...] = sdec[d] * state + kv

    gain = gain_ref[...]
    outs = [(ol_scr, gl_ref, pl.ds(pl.multiple_of(b * t_lat, t_lat), t_lat))]
    if write_ctx:
        outs.append((oc_scr, gc_ref, pl.ds(pl.multiple_of(n_lat + b * t_ctx, t_ctx), t_ctx)))
    for o_scr, g_ref, rows in outs:
        g = g_ref[...]
        y_ref[rows, :] = (_rms(o_scr[...]) * gain * (g * jax.nn.sigmoid(g))).astype(y_ref.dtype)


def _retention(qk, p, log_gamma, gain, y_buf, out_blk0, *, n_batch, t_lat, t_ctx, heads, q_blk0, k_blk0, v_blk0,
               g_blk0, dv):
    n_lat = n_batch * t_lat
    ctx_row0 = n_lat // t_ctx
    y_rows = y_buf.shape[0]
    lat = lambda w, blk0: pl.BlockSpec((t_lat, w), lambda h, b: (b, blk0 + h))
    ctx = lambda w, blk0: pl.BlockSpec((t_ctx, w), lambda h, b: (ctx_row0 + b, blk0 + h))
    est = 6 * _nbytes((t_lat + t_ctx, dv), F32) * 2 + 2 * _nbytes((y_rows, dv), BF16)
    return pl.pallas_call(
        functools.partial(_ret_kernel, t_lat=t_lat, t_ctx=t_ctx, n_lat=n_lat, write_ctx=y_rows > n_lat),
        grid=(heads, n_batch),
        in_specs=[pl.BlockSpec(memory_space=pltpu.SMEM),
                  lat(HEAD_DIM, q_blk0), ctx(HEAD_DIM, q_blk0), lat(HEAD_DIM, k_blk0), ctx(HEAD_DIM, k_blk0),
                  lat(dv, v_blk0), ctx(dv, v_blk0), lat(dv, g_blk0), ctx(dv, g_blk0),
                  pl.BlockSpec((1, dv), lambda h, b: (0, 0)),
                  pl.BlockSpec(memory_space=pl.ANY)],
        out_specs=pl.BlockSpec((y_rows, dv), lambda h, b: (0, out_blk0 + h)),
        out_shape=jax.ShapeDtypeStruct(y_buf.shape, y_buf.dtype),
        input_output_aliases={10: 0},
        scratch_shapes=[pltpu.VMEM((t_lat, dv), F32), pltpu.VMEM((t_ctx, dv), F32),
                        pltpu.VMEM((HEAD_DIM, dv), F32)],
        compiler_params=pltpu.CompilerParams(dimension_semantics=("parallel", "arbitrary"),
                                             vmem_limit_bytes=_vmem_limit(est)),
        name="retention",
    )(log_gamma, qk, qk, qk, qk, p, p, p, p, gain.reshape(1, dv), y_buf)


def _s5_tables(a_re, a_im, log_step, b_re, b_im, c_re, c_im):
    hp = lax.Precision.HIGHEST
    n_dir, n_grp, n_st = a_re.shape
    ln = S5_L
    tg = LANES // S5_P
    n_tile = n_grp // tg
    lam = lax.complex(a_re.astype(F32), a_im.astype(F32))
    step = jnp.exp(log_step.astype(F32))[..., None]
    a_bar = jnp.exp(lam * step)
    b_bar = ((a_bar - 1.0) / lam)[..., None] * lax.complex(b_re.astype(F32), b_im.astype(F32))
    c_mat = lax.complex(c_re.astype(F32), c_im.astype(F32))
    taus = jnp.arange(ln + 1, dtype=F32)
    apow = jnp.exp((lam * step)[..., None] * taus)

    kern = jnp.einsum("dgpn,dgnt,dgnq->dgtpq", c_mat, apow[..., :ln], b_bar, precision=hp).real
    ti = jnp.arange(ln)[:, None]
    tj = jnp.arange(ln)[None, :]
    t_f = kern[0][:, jnp.clip(ti - tj, 0)] * (ti >= tj)[None, :, :, None, None]
    t_b = kern[1][:, jnp.clip(tj - ti, 0)] * (tj >= ti)[None, :, :, None, None]
    toep = (t_f + t_b).reshape(n_tile, tg, ln, ln, S5_P, S5_P)
    m_intra = toep.transpose(0, 3, 1, 5, 2, 4).reshape(n_tile, ln, LANES, ln * S5_P)

    def in_mat(d, tau_of_j):
        w = apow[d][:, :, tau_of_j][..., None] * b_bar[d][:, :, None, :]
        wri = jnp.stack([w.real, w.imag]).reshape(2, n_tile, tg, n_st, ln, S5_P)
        return wri.transpose(1, 4, 2, 5, 0, 3).reshape(n_tile, ln, LANES, 2 * n_st)

    def out_mat(d, tau_of_i):
        w = c_mat[d][:, :, :, None] * apow[d][:, None, :, tau_of_i]
        wri = jnp.stack([w.real, -w.imag]).reshape(2, n_tile, tg, S5_P, n_st, ln)
        return wri.transpose(1, 0, 2, 4, 5, 3).reshape(n_tile, 2 * tg * n_st, ln * S5_P)

    def spread(n_x, n_y):
        r, c = jnp.arange(n_x * n_y)[:, None], jnp.arange(n_x * tg * n_y)[None, :]
        return ((r // n_y == c // (tg * n_y)) & (r % n_y == c % n_y)).astype(BF16)

    idx = jnp.arange(ln)
    fold = lambda m: m.reshape(n_tile, ln * LANES, m.shape[-1])
    mats = [fold(m_intra), fold(in_mat(0, ln - 1 - idx)), fold(in_mat(1, idx)),
            out_mat(0, idx + 1), out_mat(1, ln - idx)]
    dec = [apow[0][..., ln].real, apow[0][..., ln].imag, apow[1][..., ln].real, apow[1][..., ln].imag]
    return ([m.astype(BF16) for m in mats], [spread(ln, S5_P), spread(2, n_st)],
            [x.reshape(n_tile, 1, tg * n_st) for x in dec])


def _group_of(index, period, width):
    assert period & (period - 1) == 0 and width & (width - 1) == 0
    return lax.shift_right_logical(index & (period - 1), width.bit_length() - 1)


def _s5_kernel(u_ref, m_ref, bcf_ref, bcb_ref, ccf_ref, ccb_ref, sp_out_ref, sp_state_ref,
               dfr_ref, dfi_ref, dbr_ref, dbi_ref, y_ref, vf, vb, *, n_batch, t_lat, t_ctx):
    ln = S5_L
    n_lat = n_batch * t_lat
    n_cr, n_lr = t_ctx // ln, t_lat // ln
    n_st = vf.shape[0]
    n_re = n_st // 2
    state_w = n_re * LANES

    def widen(c_ref, sp_ref, row_period, row_width, col_period, col_width):
        wide = jnp.dot(c_ref[...], sp_ref[...], preferred_element_type=F32)
        r = lax.broadcasted_iota(jnp.int32, wide.shape, 0)
        c = lax.broadcasted_iota(jnp.int32, wide.shape, 1)
        own = _group_of(r, row_period, row_width) == _group_of(c, col_period, col_width)
        return jnp.where(own, wide, 0.0).astype(BF16)

    lane_rows, state_rows = (LANES, S5_P), (state_w, state_w // (LANES // S5_P))
    m_w = widen(m_ref, sp_out_ref, *lane_rows, *lane_rows)
    bc_w = [widen(r, sp_state_ref, *lane_rows, *state_rows) for r in (bcf_ref, bcb_ref)]
    cc_w = [widen(r, sp_out_ref, *state_rows, *lane_rows) for r in (ccf_ref, ccb_ref)]

    def chunk_tokens(b, j):
        ctx_rows = u_ref[pl.ds(n_lat + b * t_ctx + j, n_cr, stride=ln), :]
        lat_rows = u_ref[pl.ds(b * t_lat + j, n_lr, stride=ln), :]
        return jnp.concatenate([ctx_rows, lat_rows], axis=0).astype(BF16)

    def chunks(b):
        return jnp.concatenate([chunk_tokens(b, j) for j in range(ln)], axis=1)

    def mix(uc, w):
        return jnp.dot(uc, w, preferred_element_type=F32)

    def batch_rows(b):
        return pl.ds(b, n_cr + n_lr, stride=n_batch)

    for b in range(n_batch):
        uc = chunks(b)
        for v, w in zip((vf, vb), bc_w):
            contrib = mix(uc, w)
            for c in range(n_st):
                v[c, batch_rows(b), :] = contrib[:, c * LANES:(c + 1) * LANES]

    tile_rows = 8
    tile_chunks = tile_rows // n_batch
    n_ct, n_tiles = n_cr // tile_chunks, (n_cr + n_lr) // tile_chunks

    def scan(v, d_re, d_im, tile_of_step, backward):
        order = range(tile_chunks - 1, -1, -1) if backward else range(tile_chunks)

        def step(s, carry):
            rows = pl.ds(pl.multiple_of(tile_of_step(s) * tile_rows, tile_rows), tile_rows)
            new_carry = []
            for c in range(n_re):
                s_re, s_im = carry[2 * c], carry[2 * c + 1]
                t_re, t_im = v[c, rows, :], v[n_re + c, rows, :]
                dr, di = d_re[:, c * LANES:(c + 1) * LANES], d_im[:, c * LANES:(c + 1) * LANES]
                o_re, o_im = [None] * tile_chunks, [None] * tile_chunks
                for k in order:
                    sub = slice(k * n_batch, (k + 1) * n_batch)
                    o_re[k], o_im[k] = s_re, s_im
                    s_re, s_im = dr * s_re - di * s_im + t_re[sub], dr * s_im + di * s_re + t_im[sub]
                v[c, rows, :] = jnp.concatenate(o_re, axis=0)
                v[n_re + c, rows, :] = jnp.concatenate(o_im, axis=0)
                new_carry += [s_re, s_im]
            return tuple(new_carry)

        zero = jnp.zeros((n_batch, LANES), F32)
        lax.fori_loop(0, n_tiles, step, (zero,) * n_st)

    scan(vf, dfr_ref[...], dfi_ref[...], lambda s: s, False)
    scan(vb, dbr_ref[...], dbi_ref[...],
         lambda s: jnp.where(s < n_ct, n_ct - 1 - s, n_tiles + n_ct - 1 - s), True)

    def entering_state(v, b):
        return jnp.concatenate([v[c, batch_rows(b), :] for c in range(n_st)], axis=1).astype(BF16)

    for b in range(n_batch):
        y = (mix(chunks(b), m_w) + mix(entering_state(vf, b), cc_w[0]) + mix(entering_state(vb, b), cc_w[1]))
        for i in range(ln):
            lanes = slice(i * LANES, (i + 1) * LANES)
            y_ref[pl.ds(n_lat + b * t_ctx + i, n_cr, stride=ln), :] = y[:n_cr, lanes]
            y_ref[pl.ds(b * t_lat + i, n_lr, stride=ln), :] = y[n_cr:, lanes]


def _s5_core(p, col_blk0, mats, spreads, decs, *, n_batch, t_lat, t_ctx):
    rows = p.shape[0]
    n_tile = mats[0].shape[0]
    n_chunk = (t_lat + t_ctx) // S5_L
    assert 8 % n_batch == 0 and (t_ctx // S5_L) % (8 // n_batch) == 0 and (t_lat // S5_L) % (8 // n_batch) == 0
    state = spreads[1].shape[1]
    wide = S5_L * LANES
    mat_specs = [pl.BlockSpec((None,) + m.shape[1:], lambda t: (t, 0, 0)) for m in mats]
    spread_specs = [pl.BlockSpec(sp.shape, lambda t: (0, 0)) for sp in spreads]
    dec_specs = [pl.BlockSpec((None, 1, state // 2), lambda t: (t, 0, 0)) for _ in decs]
    est = (4 * _nbytes((rows, LANES), F32) + len(mats) * 2 * _nbytes((wide, wide), F32)
           + 2 * _nbytes((n_chunk * n_batch, state), F32) + 6 * _nbytes((n_chunk, state), F32))
    return pl.pallas_call(
        functools.partial(_s5_kernel, n_batch=n_batch, t_lat=t_lat, t_ctx=t_ctx),
        grid=(n_tile,),
        in_specs=([pl.BlockSpec((rows, LANES), lambda t: (0, col_blk0 + t))] + mat_specs + spread_specs
                  + dec_specs),
        out_specs=pl.BlockSpec((rows, LANES), lambda t: (0, t)),
        out_shape=jax.ShapeDtypeStruct((rows, n_tile * LANES), F32),
        scratch_shapes=[pltpu.VMEM((state // LANES, n_chunk * n_batch, LANES), F32) for _ in range(2)],
        compiler_params=pltpu.CompilerParams(dimension_semantics=("arbitrary",),
                                             vmem_limit_bytes=_vmem_limit(est)),
        name="s5_core",
    )(p, *mats, *spreads, *decs)


def _s5_finish_kernel(y_ref, ua_ref, ub_ref, d_ref, w_ref, o_ref, *, out_col0):
    u = jnp.concatenate([ua_ref[...], ub_ref[...]], axis=1)
    z = jax.nn.gelu(y_ref[...] + d_ref[...] * u)
    gl = jnp.dot(z.astype(BF16), w_ref[...].astype(BF16), preferred_element_type=F32)
    o_ref[...] = jnp.zeros_like(o_ref)
    o_ref[:, out_col0:out_col0 + z.shape[1]] = (z * jax.nn.sigmoid(gl)).astype(o_ref.dtype)


def _s5_finish(y, p, col_half0, d_skip, w_glu, wlead, *, out_rows, out_cols, out_col0, tm=512):
    w = y.shape[1]
    m_total = out_rows
    tm = _tile(m_total, tm)
    row = pl.BlockSpec((tm, w), lambda m: (m, 0))
    nlead = len(wlead)
    return pl.pallas_call(
        functools.partial(_s5_finish_kernel, out_col0=out_col0), grid=(m_total // tm,),
        in_specs=[row, pl.BlockSpec((tm, w // 2), lambda m: (m, col_half0)),
                  pl.BlockSpec((tm, w // 2), lambda m: (m, col_half0 + 1)),
                  pl.BlockSpec((1, w), lambda m: (0, 0)),
                  pl.BlockSpec((None,) * nlead + (w, w), lambda m: tuple(wlead) + (0, 0))],
        out_specs=pl.BlockSpec((tm, out_cols), lambda m: (m, 0)),
        out_shape=jax.ShapeDtypeStruct((m_total, out_cols), BF16),
        compiler_params=pltpu.CompilerParams(
            dimension_semantics=("parallel",),
            vmem_limit_bytes=_vmem_limit(8 * _nbytes((tm, w), F32) + 3 * _nbytes((w, w), F32)
                                         + 2 * _nbytes((tm, out_cols), BF16))),
        name="s5_finish",
    )(y, p, p, d_skip.reshape(1, w), w_glu)


def _s5_branch(p, col0, bw, params, layer, *, n_batch, t_lat, t_ctx, out_rows, out_cols, out_col0):
    mats, spreads, decs = _s5_tables(*(params[k][layer] for k in
                                       ("s5_a_re", "s5_a_im", "s5_log_step", "s5_b_re", "s5_b_im", "s5_c_re",
                                        "s5_c_im")))
    y = _s5_core(p, col0 // LANES, mats, spreads, decs, n_batch=n_batch, t_lat=t_lat, t_ctx=t_ctx)
    return _s5_finish(y, p, col0 // (bw // 2), params["s5_d"][layer], params["s5_w_glu"], (layer,),
                      out_rows=out_rows, out_cols=out_cols, out_col0=out_col0)


def _merge_kernel(h_ref, y_ref, wg_ref, wb_ref, o_ref, acc_ref, *, n_branch):
    j = pl.program_id(2)
    gate = jnp.dot(h_ref[...], wg_ref[...].astype(BF16), preferred_element_type=F32)
    val = jnp.dot(y_ref[...], wb_ref[...].astype(BF16), preferred_element_type=F32)
    term = jax.nn.sigmoid(gate) * val

    @pl.when(j == 0)
    def _():
        acc_ref[...] = term

    @pl.when(j > 0)
    def _():
        acc_ref[...] += term

    @pl.when(j == n_branch - 1)
    def _():
        o_ref[...] = acc_ref[...].astype(o_ref.dtype)


def _merge(h, ycat, w_gate, w_branch, layer, *, m_rows, tm=1024, tn=256):
    d = h.shape[1]
    tm = _tile(m_rows, tm)
    n_branch, bw = w_branch.shape[1], w_branch.shape[2]
    est = (2 * _nbytes((tm, d), BF16) + 2 * _nbytes((tm, bw), BF16) + 3 * _nbytes((d, tn), F32)
           + 3 * _nbytes((bw, tn), F32) + 5 * _nbytes((tm, tn), F32))
    return pl.pallas_call(
        functools.partial(_merge_kernel, n_branch=n_branch),
        grid=(m_rows // tm, d // tn, n_branch),
        in_specs=[pl.BlockSpec((tm, d), lambda m, n, j: (m, 0)),
                  pl.BlockSpec((tm, bw), lambda m, n, j: (m, j)),
                  pl.BlockSpec((None, None, d, tn), lambda m, n, j: (layer, j, 0, n)),
                  pl.BlockSpec((None, None, bw, tn), lambda m, n, j: (layer, j, 0, n))],
        out_specs=pl.BlockSpec((tm, tn), lambda m, n, j: (m, n)),
        out_shape=jax.ShapeDtypeStruct((m_rows, d), BF16),
        scratch_shapes=[pltpu.VMEM((tm, tn), F32)],
        compiler_params=pltpu.CompilerParams(dimension_semantics=("parallel", "parallel", "arbitrary"),
                                             vmem_limit_bytes=_vmem_limit(est)),
        name="merge",
    )(h, ycat, w_gate, w_branch)


def _swiglu_up_kernel(x_ref, w1_ref, w3_ref, o_ref):
    x = x_ref[...]
    a = jnp.dot(x, w1_ref[...].astype(BF16), preferred_element_type=F32)
    b = jnp.dot(x, w3_ref[...].astype(BF16), preferred_element_type=F32)
    o_ref[...] = (a * jax.nn.sigmoid(a) * b).astype(o_ref.dtype)


def _swiglu_up(x, w1, w3, wlead, *, tm=1024, tn=256):
    m_total, d = x.shape
    tm = _tile(m_total, tm)
    ff = w1.shape[-1]
    nlead = len(wlead)
    w_spec = pl.BlockSpec((None,) * nlead + (d, tn), lambda n, m: tuple(wlead) + (0, n))
    est = 2 * _nbytes((tm, d), BF16) + 6 * _nbytes((d, tn), F32) + 6 * _nbytes((tm, tn), F32)
    return pl.pallas_call(
        _swiglu_up_kernel,
        grid=(ff // tn, m_total // tm),
        in_specs=[pl.BlockSpec((tm, d), lambda n, m: (m, 0)), w_spec, w_spec],
        out_specs=pl.BlockSpec((tm, tn), lambda n, m: (m, n)),
        out_shape=jax.ShapeDtypeStruct((m_total, ff), BF16),
        compiler_params=pltpu.CompilerParams(dimension_semantics=("parallel", "arbitrary"),
                                             vmem_limit_bytes=_vmem_limit(est)),
        name="ffn_up",
    )(x, w1, w3)


MOE_TOP_K = 2
MOE_ROW_TILE = 512
MOE_GATHER_ROWS = 256
MOE_COMBINE_ROWS = 128
MOE_DMA_UNROLL = 8


def _moe_routing(gates, n_tok):
    tile = MOE_ROW_TILE
    n_rows = MOE_TOP_K * n_tok + N_EXPERTS * tile
    experts = jnp.concatenate([gates[:, N_EXPERTS + k] for k in range(MOE_TOP_K)]).astype(jnp.int32)
    onehot = (experts[:, None] == jnp.arange(N_EXPERTS, dtype=jnp.int32)[None, :]).astype(jnp.int32)
    before = jnp.cumsum(onehot, axis=0) - onehot
    counts = jnp.sum(onehot, axis=0)
    padded = ((counts + tile - 1) // tile) * tile
    ends = jnp.cumsum(padded)
    dest = jnp.sum(onehot * (before + (ends - padded)[None, :]), axis=1)
    tokens = jnp.tile(jnp.arange(n_tok, dtype=jnp.int32), MOE_TOP_K)
    src = jnp.zeros((n_rows,), jnp.int32).at[dest].set(tokens)
    tile_start = jnp.arange(n_rows // tile, dtype=jnp.int32) * tile
    tile_expert = jnp.minimum(jnp.sum((tile_start[:, None] >= ends[None, :]).astype(jnp.int32), axis=1),
                              N_EXPERTS - 1)
    return src, dest, tile_expert, (ends[-1:] // tile).astype(jnp.int32)


def _row_copy(src_hbm, row, dst_vmem, slot, sem):
    return pltpu.make_async_copy(src_hbm.at[pl.ds(row, 1)], dst_vmem.at[pl.ds(slot, 1)], sem)


def _moe_gather_kernel(src_ref, nused_ref, x_hbm, o_ref, sem, *, rows, steps_per_tile):
    step = pl.program_id(0)
    used = step < nused_ref[0] * steps_per_tile

    @pl.when(used)
    def _():
        base = step * rows

        def start(r, carry):
            _row_copy(x_hbm, src_ref[base + r], o_ref, r, sem).start()
            return carry

        def wait(r, carry):
            _row_copy(x_hbm, 0, o_ref, r, sem).wait()
            return carry

        lax.fori_loop(0, rows, start, 0, unroll=MOE_DMA_UNROLL)
        lax.fori_loop(0, rows, wait, 0, unroll=MOE_DMA_UNROLL)

    @pl.when(jnp.logical_not(used))
    def _():
        o_ref[...] = jnp.zeros_like(o_ref)


def _moe_gather(x, src, n_used):
    n_rows, w = src.shape[0], x.shape[1]
    rows = MOE_GATHER_ROWS
    return pl.pallas_call(
        functools.partial(_moe_gather_kernel, rows=rows, steps_per_tile=MOE_ROW_TILE // rows),
        grid_spec=pltpu.PrefetchScalarGridSpec(
            num_scalar_prefetch=2, grid=(n_rows // rows,),
            in_specs=[pl.BlockSpec(memory_space=pl.ANY)],
            out_specs=pl.BlockSpec((rows, w), lambda i, s, nu: (i, 0)),
            scratch_shapes=[pltpu.SemaphoreType.DMA]),
        out_shape=jax.ShapeDtypeStruct((n_rows, w), x.dtype),
        compiler_params=pltpu.CompilerParams(
            dimension_semantics=("arbitrary",),
            vmem_limit_bytes=_vmem_limit(2 * _nbytes((rows, w), x.dtype))),
        name="moe_gather",
    )(src, n_used, x)


def _moe_up_kernel(te_ref, nused_ref, x_ref, w1_ref, w3_ref, o_ref):
    used = pl.program_id(1) < nused_ref[0]

    @pl.when(used)
    def _():
        half = w1_ref.shape[0] // 2
        x_lo, x_hi = (v.astype(BF16) for v in _unpack_bf16_pairs(x_ref[...]))

        def proj(w_ref):
            return (jnp.dot(x_lo, w_ref[:half, :].astype(BF16), preferred_element_type=F32)
                    + jnp.dot(x_hi, w_ref[half:, :].astype(BF16), preferred_element_type=F32))

        a, b = proj(w1_ref), proj(w3_ref)
        o_ref[...] = (a * jax.nn.sigmoid(a) * b).astype(o_ref.dtype)

    @pl.when(jnp.logical_not(used))
    def _():
        o_ref[...] = jnp.zeros_like(o_ref)


def _moe_down_kernel(te_ref, nused_ref, u_ref, w2_ref, o_ref):
    used = pl.program_id(1) < nused_ref[0]

    @pl.when(used)
    def _():
        y = jnp.dot(u_ref[...], w2_ref[...].astype(BF16), preferred_element_type=F32)
        o_ref[...] = _pack_bf16_pairs(y)

    @pl.when(jnp.logical_not(used))
    def _():
        o_ref[...] = jnp.zeros_like(o_ref)


def _moe_grouped(kernel, x, ws, layer, tile_expert, n_used, *, tn, out_tn, out_dtype, name):
    n_rows, x_w = x.shape
    kdim, n_out = ws[0].shape[-2:]
    tm = MOE_ROW_TILE

    def row_blk(m, nu):
        return jnp.minimum(m, nu[0] - 1)

    w_spec = pl.BlockSpec((None, None, kdim, tn), lambda n, m, te, nu: (layer, te[row_blk(m, nu)], 0, n))
    est = (2 * _nbytes((tm, x_w), x.dtype) + len(ws) * 3 * _nbytes((kdim, tn), F32)
           + (3 + len(ws)) * _nbytes((tm, tn), F32) + 2 * _nbytes((tm, kdim), BF16))
    return pl.pallas_call(
        kernel,
        grid_spec=pltpu.PrefetchScalarGridSpec(
            num_scalar_prefetch=2, grid=(n_out // tn, n_rows // tm),
            in_specs=[pl.BlockSpec((tm, x_w), lambda n, m, te, nu: (row_blk(m, nu), 0))] + [w_spec] * len(ws),
            out_specs=pl.BlockSpec((tm, out_tn), lambda n, m, te, nu: (m, n))),
        out_shape=jax.ShapeDtypeStruct((n_rows, n_out // tn * out_tn), out_dtype),
        compiler_params=pltpu.CompilerParams(dimension_semantics=("parallel", "arbitrary"),
                                             vmem_limit_bytes=_vmem_limit(est)),
        name=name,
    )(tile_expert, n_used, x, *ws)


def _moe_combine_kernel(dest_ref, y_hbm, xs_ref, route_ref, gate_ref, *rest, rows, n_tok, pack_w, final_norm):
    final_gain_ref = rest[0] if final_norm else None
    o_ref, buf, sem = rest[-3:]
    base = pl.program_id(0) * rows
    for k in range(MOE_TOP_K):
        def start(r, carry, k=k):
            _row_copy(y_hbm, dest_ref[k * n_tok + base + r], buf.at[k], r, sem).start()
            return carry

        lax.fori_loop(0, rows, start, 0, unroll=MOE_DMA_UNROLL)
    for k in range(MOE_TOP_K):
        def wait(r, carry, k=k):
            _row_copy(y_hbm, 0, buf.at[k], r, sem).wait()
            return carry

        lax.fori_loop(0, rows, wait, 0, unroll=MOE_DMA_UNROLL)
    route = route_ref[...]
    w_lane0 = N_EXPERTS + MOE_TOP_K

    def expert_rows(k):
        halves = [_unpack_bf16_pairs(buf[k, :, c:c + pack_w]) for c in range(0, buf.shape[2], pack_w)]
        return jnp.concatenate([h for pair in halves for h in pair], axis=1)

    mix = functools.reduce(jnp.add, [route[:, w_lane0 + k:w_lane0 + k + 1] * expert_rows(k)
                                     for k in range(MOE_TOP_K)])
    out = xs_ref[...] + gate_ref[...] * mix
    o_ref[...] = _rms(out) * final_gain_ref[...] if final_norm else out


def _moe_combine(xs, y_rows, pack_w, dest, route, gate, tab, final_gain=None):
    n_tok, d = xs.shape
    rows = _tile(tab.group_rows, MOE_COMBINE_ROWS)
    row = lambda w: pl.BlockSpec((rows, w), lambda i, dst: (i, 0))
    est = 8 * _nbytes((rows, d), F32)
    final_norm = final_gain is not None
    extra_specs = [pl.BlockSpec((1, d), lambda i, dst: (0, 0))] if final_norm else []
    extra_args = [final_gain.reshape(1, d)] if final_norm else []
    return pl.pallas_call(
        functools.partial(_moe_combine_kernel, rows=rows, n_tok=n_tok, pack_w=pack_w, final_norm=final_norm),
        grid_spec=pltpu.PrefetchScalarGridSpec(
            num_scalar_prefetch=1, grid=(n_tok // rows,),
            in_specs=[pl.BlockSpec(memory_space=pl.ANY), row(d), row(route.shape[1]),
                      pl.BlockSpec((None, 1, d), lambda i, dst: (tab.index_of_row(i * rows), 0, 0))]
                     + extra_specs,
            out_specs=row(d),
            scratch_shapes=[pltpu.VMEM((MOE_TOP_K, rows, y_rows.shape[1]), y_rows.dtype),
                            pltpu.SemaphoreType.DMA]),
        out_shape=jax.ShapeDtypeStruct((n_tok, d), F32),
        compiler_params=pltpu.CompilerParams(dimension_semantics=("arbitrary",),
                                             vmem_limit_bytes=_vmem_limit(est)),
        name="moe_combine",
    )(dest, y_rows, xs, route, gate, *extra_args)


def _moe_sparse(xs, h2p, route, w1, w3, w2, layer, gate, tab, final_gain=None):
    n_tok = xs.shape[0]
    src, dest, tile_expert, n_used = _moe_routing(route, n_tok)
    xg = _moe_gather(h2p, src, n_used)
    up_tn, down_tn = 256, 1024
    ug = _moe_grouped(_moe_up_kernel, xg, (w1, w3), layer, tile_expert, n_used, tn=up_tn, out_tn=up_tn,
                      out_dtype=BF16, name="moe_up")
    yg = _moe_grouped(_moe_down_kernel, ug, (w2,), layer, tile_expert, n_used, tn=down_tn, out_tn=down_tn // 2,
                      out_dtype=jnp.uint32, name="moe_down")
    return _moe_combine(xs, yg, down_tn // 2, dest, route, gate, tab, final_gain)


def kernel(x, c, ctx, c_ctx, ada_w, ada_b, norm1_g, norm2_g, w_in, attn_q_norm, attn_k_norm, s5_a_re, s5_a_im, s5_log_step, s5_b_re, s5_b_im, s5_c_re, s5_c_im, s5_d, s5_w_glu, diff_lambda, diff_norm, ret_decay_logit, ret_norm, w_branch, w_merge_gate, w_out, ffn_w1, ffn_w3, ffn_w2, moe_router_w, moe_router_b, moe_w1, moe_w3, moe_w2, final_norm_g):
    n_batch, t_lat, d = x.shape
    t_ctx = ctx.shape[1]
    depth = w_in.shape[0]
    n_lat, n_ctx = n_batch * t_lat, n_batch * t_ctx
    bw = d // 4
    s5_params = dict(s5_a_re=s5_a_re, s5_a_im=s5_a_im, s5_log_step=s5_log_step, s5_b_re=s5_b_re,
                     s5_b_im=s5_b_im, s5_c_re=s5_c_re, s5_c_im=s5_c_im, s5_d=s5_d, s5_w_glu=s5_w_glu)

    tab = RowTable(lambda row: jnp.where(row < n_lat, row // t_lat, n_batch), t_lat)

    cos_t, sin_t = _rope_tables(t_lat, n_batch, t_ctx)
    ada_rows = 8
    c_all = jnp.zeros((ada_rows, d), F32).at[:n_batch].set(c).at[n_batch].set(c_ctx)
    xs = jnp.concatenate([x.reshape(n_lat, d), ctx.reshape(n_ctx, d)])

    for i in range(depth):
        need_ctx = i < depth - 1
        lam_init = 0.8 - 0.6 * math.exp(-0.3 * i)
        mod = _mm(c_all, ada_w, (i,), name="ada_mod", tm=ada_rows, tn=512, epi="bias", x_silu=True,
                  bias=ada_b[i].reshape(1, -1))
        tabs = [mod[:n_batch + 1, k * d:(k + 1) * d].reshape(n_batch + 1, 1, d) for k in range(6)]
        shift1, scale1, gate1, shift2, scale2, gate2 = tabs

        h = _modnorm(xs, norm1_g[i], shift1, scale1, tab)
        p = _mm(h, w_in, (i,), name="in_proj", tm=1024, tn=512)
        qk = _prep(p, cos_t, sin_t, attn_q_norm[i], attn_k_norm[i])

        att = functools.partial(_attention, qk=qk, p=p, n_batch=n_batch, t_lat=t_lat, t_ctx=t_ctx, tq=512)
        gqa = functools.partial(att, _gqa_kernel, name="gqa", heads=2,
                                q_w=4 * HEAD_DIM, q_blk0=0, k_w=HEAD_DIM, k_blk0=8, v_w=HEAD_DIM, v_blk0=10,
                                out_w=4 * HEAD_DIM, out_blk0=0)
        lp = diff_lambda[i].astype(F32)
        lam = (jnp.exp(jnp.sum(lp[0] * lp[1])) - jnp.exp(jnp.sum(lp[2] * lp[3])) + lam_init).reshape(1)
        dgain = diff_norm[i].reshape(1, 2 * HEAD_DIM)
        dif = functools.partial(att, functools.partial(_diff_kernel, out_scale=1.0 - lam_init),
                                name="diff",
                                heads=4, q_w=2 * HEAD_DIM, q_blk0=5, k_w=2 * HEAD_DIM, k_blk0=9,
                                v_w=2 * HEAD_DIM, v_blk0=18, out_w=2 * HEAD_DIM, out_blk0=2 * bw // (2 * HEAD_DIM),
                                lead_in=(lam,), lead_specs=(pl.BlockSpec(memory_space=pltpu.SMEM),),
                                extra_in=(dgain,),
                                extra_specs=(pl.BlockSpec((1, 2 * HEAD_DIM), lambda b, hh, ii: (0, 0)),))
        rows = n_lat + n_ctx if need_ctx else n_lat
        s5_col0 = 12 * HEAD_DIM
        ycat = _s5_branch(p, s5_col0, bw, s5_params, i, n_batch=n_batch, t_lat=t_lat, t_ctx=t_ctx,
                          out_rows=rows, out_cols=4 * bw, out_col0=bw)
        ycat = gqa(ctx_queries=False, y_buf=ycat)
        ycat = dif(ctx_queries=False, y_buf=ycat)
        log_gamma = jax.nn.log_sigmoid(ret_decay_logit[i].astype(F32))
        ycat = _retention(qk, p, log_gamma, ret_norm[i], ycat, 3 * bw // (2 * HEAD_DIM), n_batch=n_batch,
                          t_lat=t_lat, t_ctx=t_ctx, heads=4, q_blk0=26, k_blk0=30, v_blk0=26, g_blk0=30,
                          dv=2 * HEAD_DIM)
        if need_ctx:
            ycat = gqa(ctx_queries=True, y_buf=ycat)
            ycat = dif(ctx_queries=True, y_buf=ycat)
        acc = _merge(h, ycat, w_merge_gate, w_branch, i, m_rows=rows)
        xs = _mm(acc, w_out, (i,), name="out_proj", tm=1024, tn=512, epi="resgate", res=xs, gate=gate1,
                 tab=tab, m_rows=rows)

        if i % 2 == 0:
            j = i // 2
            h2 = _modnorm(xs, norm2_g[i], shift2, scale2, tab)
            u = _swiglu_up(h2, ffn_w1, ffn_w3, (j,))
            half = u.shape[1] // 2
            for kc in range(2):
                xs = _mm(u, ffn_w2, (j,), name="ffn_down", tm=512, tn=512, tk=half, k0=kc, epi="resgate",
                         res=xs, gate=gate2, tab=tab)
        else:
            j = i // 2
            h2, route = _modnorm(xs, norm2_g[i], shift2, scale2, tab, tm=256,
                                 router_w=moe_router_w[j], router_b=moe_router_b[j])
            closing = final_norm_g if (i == depth - 1 and xs.shape[0] == n_lat) else None
            xs = _moe_sparse(xs, h2, route, moe_w1, moe_w3, moe_w2, j, gate2, tab, closing)
            if closing is not None:
                return xs.reshape(n_batch, t_lat, d)

    return _final_norm(xs[:n_lat], final_norm_g).reshape(n_batch, t_lat, d)
```

```python
import functools
import math
from typing import Callable, NamedTuple

import jax
import jax.numpy as jnp
from jax import lax
from jax.experimental import pallas as pl
from jax.experimental.pallas import tpu as pltpu

F32 = jnp.float32
BF16 = jnp.bfloat16

HEAD_DIM = 128
GRID_W = 64
ROPE_BASE = 10000.0
ROPE_FREQS = HEAD_DIM // 4
EPS = 1e-6
A_GROUP = 4
ATTN_KEY_CHUNK = 2048
S5_P = 16
S5_N = 64
S5_L = 8
LANES = 128
RET_CHUNK = 128
N_EXPERTS = 8
ROUTER_LANES = 128
V7X_VMEM_BYTES = 64 * 1024 * 1024
VMEM_HEADROOM_BYTES = 8 * 1024 * 1024


def _vmem_limit(nbytes):
    return int(min(nbytes + VMEM_HEADROOM_BYTES, V7X_VMEM_BYTES - VMEM_HEADROOM_BYTES))


def _nbytes(shape, dtype):
    return math.prod(shape) * jnp.dtype(dtype).itemsize


class RowTable(NamedTuple):
    index_of_row: Callable
    group_rows: int


def _tile(total, preferred):
    return math.gcd(total, preferred)


def _mm_kernel(*refs, x_silu, epi):
    it = iter(refs)
    x_ref, w_ref = next(it), next(it)
    bias_ref = next(it) if epi == "bias" else None
    res_ref, gate_ref = (next(it), next(it)) if epi == "resgate" else (None, None)
    o_ref = next(it)

    x = x_ref[...]
    if x_silu:
        x = x * jax.nn.sigmoid(x)
    acc = jnp.dot(x.astype(BF16), w_ref[...].astype(BF16), preferred_element_type=F32)
    if epi == "bias":
        acc = acc + bias_ref[...]
    elif epi == "resgate":
        acc = res_ref[...] + gate_ref[...] * acc
    o_ref[...] = acc.astype(o_ref.dtype)


def _mm(x, w, wlead=(), *, name, m_rows=None, tm, tn, tk=None, k0=0, epi="none",
        out_dtype=F32, x_silu=False, bias=None, res=None, gate=None, tab=None):
    m_total = x.shape[0] if m_rows is None else m_rows
    n_total = w.shape[-1]
    tk = x.shape[1] if tk is None else tk
    tm, tn = _tile(m_total, tm), _tile(n_total, tn)
    if tab is not None:
        tm = _tile(tab.group_rows, tm)
    nlead = len(wlead)
    in_specs = [
        pl.BlockSpec((tm, tk), lambda n, m: (m, k0)),
        pl.BlockSpec((None,) * nlead + (tk, tn), lambda n, m: tuple(wlead) + (k0, n)),
    ]
    args = [x, w]
    est = 2 * _nbytes((tm, tk), x.dtype) + 2 * _nbytes((tk, tn), w.dtype) + _nbytes((tk, tn), BF16)
    if epi == "bias":
        in_specs.append(pl.BlockSpec((1, tn), lambda n, m: (0, n)))
        args.append(bias)
    elif epi == "resgate":
        in_specs.append(pl.BlockSpec((tm, tn), lambda n, m: (m, n)))
        in_specs.append(pl.BlockSpec((None, 1, tn), lambda n, m: (tab.index_of_row(m * tm), 0, n)))
        args += [res, gate]
        est += 2 * _nbytes((tm, tn), F32)
    est += 2 * _nbytes((tm, tn), out_dtype) + 2 * _nbytes((tm, tn), F32)
    return pl.pallas_call(
        functools.partial(_mm_kernel, x_silu=x_silu, epi=epi),
        grid=(n_total // tn, m_total // tm),
        in_specs=in_specs,
        out_specs=pl.BlockSpec((tm, tn), lambda n, m: (m, n)),
        out_shape=jax.ShapeDtypeStruct((m_total, n_total), out_dtype),
        compiler_params=pltpu.CompilerParams(
            dimension_semantics=("parallel", "arbitrary"),
            vmem_limit_bytes=_vmem_limit(est)),
        name=name,
    )(*args)


def _rms(x):
    return x * lax.rsqrt(jnp.mean(x * x, axis=-1, keepdims=True) + EPS)


def _pack_bf16_pairs(x):
    w = x.shape[1] // 2
    as_bits = lambda v: lax.bitcast_convert_type(v.astype(BF16).astype(F32), jnp.uint32)
    return (as_bits(x[:, w:]) & jnp.uint32(0xFFFF0000)) | (as_bits(x[:, :w]) >> 16)


def _unpack_bf16_pairs(p):
    low = lax.bitcast_convert_type(p << 16, F32)
    high = lax.bitcast_convert_type(p & jnp.uint32(0xFFFF0000), F32)
    return low, high


def _modnorm_kernel(x_ref, g_ref, sh_ref, sc_ref, *rest, router):
    h = _rms(x_ref[...]) * g_ref[...] * (1.0 + sc_ref[...]) + sh_ref[...]
    if not router:
        (o_ref,) = rest
        o_ref[...] = h.astype(o_ref.dtype)
        return
    rw_ref, rb_ref, o_ref, gate_ref = rest
    o_ref[...] = _pack_bf16_pairs(h)
    logits = jnp.dot(h, rw_ref[...], preferred_element_type=F32,
                     precision=lax.Precision.HIGHEST) + rb_ref[...]
    lane = lax.broadcasted_iota(jnp.int32, logits.shape, 1).astype(F32)
    neg = jnp.float32(-jnp.inf)
    logits = jnp.where(lane < N_EXPERTS, logits, neg)
    v1 = jnp.max(logits, axis=-1, keepdims=True)
    i1 = jnp.min(jnp.where(logits == v1, lane, float(ROUTER_LANES)), axis=-1, keepdims=True)
    rest_l = jnp.where(lane == i1, neg, logits)
    v2 = jnp.max(rest_l, axis=-1, keepdims=True)
    i2 = jnp.min(jnp.where(rest_l == v2, lane, float(ROUTER_LANES)), axis=-1, keepdims=True)
    e2 = jnp.exp(v2 - v1)
    w1 = 1.0 / (1.0 + e2)
    w2 = e2 * w1
    routed = jnp.where(lane == i1, w1, jnp.where(lane == i2, w2, 0.0))
    for off, val in enumerate((i1, i2, w1, w2)):
        routed = jnp.where(lane == float(N_EXPERTS + off), val, routed)
    gate_ref[...] = routed


def _modnorm(x, g, shift_tab, scale_tab, tab, *, tm=512, router_w=None, router_b=None):
    m_total, d = x.shape
    tm = _tile(tab.group_rows, _tile(m_total, tm))
    row = pl.BlockSpec((tm, d), lambda m: (m, 0))
    vec = pl.BlockSpec((1, d), lambda m: (0, 0))
    per_group = pl.BlockSpec((None, 1, d), lambda m: (tab.index_of_row(m * tm), 0, 0))
    in_specs, args = [row, vec, per_group, per_group], [x, g.reshape(1, d), shift_tab, scale_tab]
    router = router_w is not None
    out_specs, out_shape = row, jax.ShapeDtypeStruct((m_total, d), BF16)
    est = 2 * _nbytes((tm, d), F32) * 3
    if router:
        rw = jnp.zeros((d, ROUTER_LANES), F32).at[:, :N_EXPERTS].set(router_w)
        rb = jnp.zeros((1, ROUTER_LANES), F32).at[0, :N_EXPERTS].set(router_b)
        in_specs += [pl.BlockSpec((d, ROUTER_LANES), lambda m: (0, 0)),
                     pl.BlockSpec((1, ROUTER_LANES), lambda m: (0, 0))]
        args += [rw, rb]
        out_specs = (pl.BlockSpec((tm, d // 2), lambda m: (m, 0)),
                     pl.BlockSpec((tm, ROUTER_LANES), lambda m: (m, 0)))
        out_shape = (jax.ShapeDtypeStruct((m_total, d // 2), jnp.uint32),
                     jax.ShapeDtypeStruct((m_total, ROUTER_LANES), F32))
        est += 2 * _nbytes((d, ROUTER_LANES), F32) * 4
    return pl.pallas_call(
        functools.partial(_modnorm_kernel, router=router),
        grid=(m_total // tm,), in_specs=in_specs, out_specs=out_specs, out_shape=out_shape,
        compiler_params=pltpu.CompilerParams(dimension_semantics=("parallel",),
                                             vmem_limit_bytes=_vmem_limit(est)),
        name="modnorm_router" if router else "modnorm",
    )(*args)


def _final_norm_kernel(x_ref, g_ref, o_ref):
    o_ref[...] = _rms(x_ref[...]) * g_ref[...]


def _final_norm(x, g, *, tm=256):
    m_total, d = x.shape
    tm = _tile(m_total, tm)
    row = pl.BlockSpec((tm, d), lambda m: (m, 0))
    return pl.pallas_call(
        _final_norm_kernel, grid=(m_total // tm,),
        in_specs=[row, pl.BlockSpec((1, d), lambda m: (0, 0))], out_specs=row,
        out_shape=jax.ShapeDtypeStruct((m_total, d), F32),
        compiler_params=pltpu.CompilerParams(dimension_semantics=("parallel",),
                                             vmem_limit_bytes=_vmem_limit(6 * _nbytes((tm, d), F32))),
        name="final_norm",
    )(x, g.reshape(1, d))


class PrepHead(NamedTuple):
    src_col: int
    norm: str
    scale: float


PREP_SRC_W = 512
ATTN_SCALE = HEAD_DIM ** -0.5


def _prep_plan():
    h = HEAD_DIM
    plan = [PrepHead(j * h, "q", ATTN_SCALE) for j in range(8)]
    plan += [PrepHead((20 + j) * h, "", ATTN_SCALE) for j in range(8)]
    plan += [PrepHead((28 + j) * h, "", 1.0) for j in range(8)]
    plan += [PrepHead((8 + j) * h, "k", 1.0) for j in range(2)]
    plan += [PrepHead((44 + j) * h, "", 1.0) for j in range(4)]
    plan += [PrepHead((48 + j) * h, "", ATTN_SCALE) for j in range(4)]
    return plan


def _prep_kernel(*refs, plan, src_blocks):
    src_refs = dict(zip(src_blocks, refs[:len(src_blocks)]))
    cos_ref, sin_ref, qn_ref, kn_ref, o_ref = refs[len(src_blocks):]
    cos, sin = cos_ref[...], sin_ref[...]
    lane = lax.broadcasted_iota(jnp.int32, cos.shape, 1)
    first_half = (lane % (2 * ROPE_FREQS)) < ROPE_FREQS
    gains = {"q": qn_ref[...], "k": kn_ref[...]}
    for j, head in enumerate(plan):
        blk, off = divmod(head.src_col, PREP_SRC_W)
        y = src_refs[blk][:, off:off + HEAD_DIM]
        if head.norm:
            y = _rms(y) * gains[head.norm]
        if head.scale != 1.0:
            y = y * head.scale
        partner = jnp.where(first_half, pltpu.roll(y, HEAD_DIM - ROPE_FREQS, 1), pltpu.roll(y, ROPE_FREQS, 1))
        o_ref[:, j * HEAD_DIM:(j + 1) * HEAD_DIM] = (y * cos + partner * sin).astype(o_ref.dtype)


def _prep(p, cos_t, sin_t, q_gain, k_gain, *, tm=256):
    m_total = p.shape[0]
    tm = _tile(m_total, tm)
    plan = _prep_plan()
    src_blocks = sorted({head.src_col // PREP_SRC_W for head in plan})
    row = lambda w, blk: pl.BlockSpec((tm, w), lambda m: (m, blk))
    vec = pl.BlockSpec((1, HEAD_DIM), lambda m: (0, 0))
    out_w = len(plan) * HEAD_DIM
    est = 2 * len(src_blocks) * _nbytes((tm, PREP_SRC_W), F32) + 2 * _nbytes((tm, out_w), BF16)
    return pl.pallas_call(
        functools.partial(_prep_kernel, plan=plan, src_blocks=src_blocks),
        grid=(m_total // tm,),
        in_specs=[row(PREP_SRC_W, blk) for blk in src_blocks] + [row(HEAD_DIM, 0), row(HEAD_DIM, 0), vec, vec],
        out_specs=row(out_w, 0),
        out_shape=jax.ShapeDtypeStruct((m_total, out_w), BF16),
        compiler_params=pltpu.CompilerParams(dimension_semantics=("parallel",),
                                             vmem_limit_bytes=_vmem_limit(est)),
        name="qk_prep",
    )(*([p] * len(src_blocks)), cos_t, sin_t, q_gain.reshape(1, HEAD_DIM), k_gain.reshape(1, HEAD_DIM))


def _rope_tables(t_lat, n_batch, t_ctx):
    rows = t_lat // GRID_W
    row = jnp.broadcast_to(jnp.arange(rows)[:, None], (rows, GRID_W)).reshape(-1)
    col = jnp.broadcast_to(jnp.arange(GRID_W)[None, :], (rows, GRID_W)).reshape(-1)
    inv = ROPE_BASE ** (-jnp.arange(ROPE_FREQS, dtype=F32) / ROPE_FREQS)
    ang = jnp.stack([row, col], axis=-1).astype(F32)[:, :, None] * inv
    cos, sin = jnp.cos(ang), jnp.sin(ang)
    cos128 = jnp.concatenate([cos, cos], axis=-1).reshape(t_lat, HEAD_DIM)
    sin128 = jnp.concatenate([-sin, sin], axis=-1).reshape(t_lat, HEAD_DIM)
    cos_t = jnp.concatenate([jnp.tile(cos128, (n_batch, 1)), jnp.ones((n_batch * t_ctx, HEAD_DIM), F32)])
    sin_t = jnp.concatenate([jnp.tile(sin128, (n_batch, 1)), jnp.zeros((n_batch * t_ctx, HEAD_DIM), F32)])
    return cos_t, sin_t


_NT = (((1,), (1,)), ((), ()))


def _softmax_parts(q, ks):
    ss = [lax.dot_general(q, k, _NT, preferred_element_type=F32) for k in ks]
    m = functools.reduce(jnp.maximum, [jnp.max(s, axis=-1, keepdims=True) for s in ss])
    es = [jnp.exp(s - m) for s in ss]
    l = functools.reduce(jnp.add, [jnp.sum(e, axis=-1, keepdims=True) for e in es])
    return es, l


def _gqa_kernel(q_ref, *rest, nseg):
    k_refs, v_refs, o_ref = rest[:nseg], rest[nseg:2 * nseg], rest[-1]
    ks = [r[...] for r in k_refs]
    vs = [r[...].astype(BF16) for r in v_refs]
    for g in range(A_GROUP):
        cols = slice(g * HEAD_DIM, (g + 1) * HEAD_DIM)
        es, l = _softmax_parts(q_ref[:, cols], ks)
        o = functools.reduce(jnp.add, [jnp.dot(e.astype(BF16), v, preferred_element_type=F32)
                                       for e, v in zip(es, vs)])
        o_ref[:, cols] = (o / l).astype(o_ref.dtype)


def _diff_kernel(lam_ref, q_ref, *rest, nseg, out_scale):
    k_refs, v_refs = rest[:nseg], rest[nseg:2 * nseg]
    gain_ref, o_ref = rest[2 * nseg], rest[-1]
    lam = lam_ref[0]
    hw = 2 * HEAD_DIM
    for hd in range(q_ref.shape[1] // hw):
        vs = [r[:, hd * hw:(hd + 1) * hw].astype(BF16) for r in v_refs]
        maps = []
        for m in range(2):
            cols = slice(hd * hw + m * HEAD_DIM, hd * hw + (m + 1) * HEAD_DIM)
            es, l = _softmax_parts(q_ref[:, cols], [r[:, cols] for r in k_refs])
            pv = functools.reduce(jnp.add, [jnp.dot(e.astype(BF16), v, preferred_element_type=F32)
                                            for e, v in zip(es, vs)])
            maps.append(pv / l)
        o = maps[0] - lam * maps[1]
        o_ref[:, hd * hw:(hd + 1) * hw] = (_rms(o) * gain_ref[...] * out_scale).astype(o_ref.dtype)


def _attention(kernel, qk, p, *, name, n_batch, t_lat, t_ctx, heads, q_w, q_blk0, k_w, k_blk0, v_w, v_blk0,
               out_w, out_blk0, y_buf, ctx_queries, tq, extra_in=(), extra_specs=(), lead_in=(), lead_specs=()):
    n_lat = n_batch * t_lat
    ctx_row0 = n_lat // t_ctx
    tq = _tile(t_lat, tq)
    if ctx_queries:
        nq, q_rows = 1, t_ctx
        q_map = lambda b, h, i: (ctx_row0 + b, q_blk0 + h)
        o_map = lambda b, h, i: (ctx_row0 + b, out_blk0 + h)
        segs = [(t_ctx, lambda b: ctx_row0 + b)]
    else:
        nq, q_rows = t_lat // tq, tq
        q_map = lambda b, h, i: (b * nq + i, q_blk0 + h)
        o_map = lambda b, h, i: (b * nq + i, out_blk0 + h)
        kc = _tile(t_lat, ATTN_KEY_CHUNK)
        n_kc = t_lat // kc
        segs = [(t_ctx, lambda b: ctx_row0 + b)]
        segs += [(kc, functools.partial(lambda c, b: b * n_kc + c, c)) for c in range(n_kc)]
    k_specs = [pl.BlockSpec((rows, k_w), functools.partial(lambda rf, b, h, i: (rf(b), k_blk0 + h), rf))
               for rows, rf in segs]
    v_specs = [pl.BlockSpec((rows, v_w), functools.partial(lambda rf, b, h, i: (rf(b), v_blk0 + h), rf))
               for rows, rf in segs]
    nseg = len(segs)
    est = 8 * _nbytes((q_rows, t_lat + t_ctx), F32) + 4 * _nbytes((t_lat + t_ctx, v_w), F32)
    return pl.pallas_call(
        functools.partial(kernel, nseg=nseg),
        grid=(n_batch, heads, nq),
        in_specs=(list(lead_specs) + [pl.BlockSpec((q_rows, q_w), q_map)] + k_specs + v_specs + list(extra_specs)
                  + [pl.BlockSpec(memory_space=pl.ANY)]),
        out_specs=pl.BlockSpec((q_rows, out_w), o_map),
        out_shape=jax.ShapeDtypeStruct(y_buf.shape, y_buf.dtype),
        input_output_aliases={len(lead_in) + 1 + 2 * nseg + len(extra_in): 0},
        compiler_params=pltpu.CompilerParams(dimension_semantics=("parallel", "parallel", "arbitrary"),
                                             vmem_limit_bytes=_vmem_limit(est)),
        name=name + ("_ctx" if ctx_queries else "_lat"),
    )(*lead_in, qk, *([qk] * nseg), *([p] * nseg), *extra_in, y_buf)


def _ret_kernel(lg_ref, ql_ref, qc_ref, kl_ref, kc_ref, vl_ref, vc_ref, gl_ref, gc_ref, gain_ref, y_buf_ref,
                y_ref, ol_scr, oc_scr, s_scr, *, t_lat, t_ctx, n_lat, write_ctx):
    h, b = pl.program_id(0), pl.program_id(1)
    lg_f, lg_b = lg_ref[0, h], lg_ref[1, h]
    c = RET_CHUNK
    pos_i = lax.broadcasted_iota(jnp.int32, (c, c), 0).astype(F32)
    pos_j = lax.broadcasted_iota(jnp.int32, (c, c), 1).astype(F32)
    rel = pos_i - pos_j
    d_both = (jnp.where(rel >= 0, jnp.exp(jnp.maximum(rel, 0.0) * lg_f), 0.0)
              + jnp.where(rel <= 0, jnp.exp(jnp.maximum(-rel, 0.0) * lg_b), 0.0))
    pos = lax.broadcasted_iota(jnp.int32, (c, 1), 0).astype(F32)
    qdec = (jnp.exp((pos + 1.0) * lg_f), jnp.exp((c - pos) * lg_b))
    kdec = (jnp.exp((c - 1.0 - pos) * lg_f), jnp.exp(pos * lg_b))
    one = jnp.ones((1, 1), F32)
    sdec = (jnp.exp(one * (c * lg_f)), jnp.exp(one * (c * lg_b)))

    chunks = [(qc_ref, kc_ref, vc_ref, oc_scr, i) for i in range(t_ctx // c)]
    chunks += [(ql_ref, kl_ref, vl_ref, ol_scr, i) for i in range(t_lat // c)]
    back = ([ch for ch in chunks if ch[0] is qc_ref][::-1] + [ch for ch in chunks if ch[0] is ql_ref][::-1])

    for d, order in ((0, chunks), (1, back)):
        s_scr[...] = jnp.zeros_like(s_scr)
        for q_ref, k_ref, v_ref, o_scr, i in order:
            rows = pl.ds(i * c, c)
            q, k = q_ref[rows, :], k_ref[rows, :]
            v = v_ref[rows, :].astype(BF16)
            state = s_scr[...]
            cross = jnp.dot((q.astype(F32) * qdec[d]).astype(BF16), state.astype(BF16),
                            preferred_element_type=F32)
            if d == 0:
                scores = lax.dot_general(q, k, _NT, preferred_element_type=F32) * d_both
                o_scr[rows, :] = cross + jnp.dot(scores.astype(BF16), v, preferred_element_type=F32)
            else:
                o_scr[rows, :] += cross
            kv = jnp.dot((k.astype(F32) * kdec[d]).T.astype(BF16), v, preferred_element_type=F32)
            s_scr[...] = sdec[d] * state + kv

    gain = gain_ref[...]
    outs = [(ol_scr, gl_ref, pl.ds(pl.multiple_of(b * t_lat, t_lat), t_lat))]
    if write_ctx:
        outs.append((oc_scr, gc_ref, pl.ds(pl.multiple_of(n_lat + b * t_ctx, t_ctx), t_ctx)))
    for o_scr, g_ref, rows in outs:
        g = g_ref[...]
        y_ref[rows, :] = (_rms(o_scr[...]) * gain * (g * jax.nn.sigmoid(g))).astype(y_ref.dtype)


def _retention(qk, p, log_gamma, gain, y_buf, out_blk0, *, n_batch, t_lat, t_ctx, heads, q_blk0, k_blk0, v_blk0,
               g_blk0, dv):
    n_lat = n_batch * t_lat
    ctx_row0 = n_lat // t_ctx
    y_rows = y_buf.shape[0]
    lat = lambda w, blk0: pl.BlockSpec((t_lat, w), lambda h, b: (b, blk0 + h))
    ctx = lambda w, blk0: pl.BlockSpec((t_ctx, w), lambda h, b: (ctx_row0 + b, blk0 + h))
    est = 6 * _nbytes((t_lat + t_ctx, dv), F32) * 2 + 2 * _nbytes((y_rows, dv), BF16)
    return pl.pallas_call(
        functools.partial(_ret_kernel, t_lat=t_lat, t_ctx=t_ctx, n_lat=n_lat, write_ctx=y_rows > n_lat),
        grid=(heads, n_batch),
        in_specs=[pl.BlockSpec(memory_space=pltpu.SMEM),
                  lat(HEAD_DIM, q_blk0), ctx(HEAD_DIM, q_blk0), lat(HEAD_DIM, k_blk0), ctx(HEAD_DIM, k_blk0),
                  lat(dv, v_blk0), ctx(dv, v_blk0), lat(dv, g_blk0), ctx(dv, g_blk0),
                  pl.BlockSpec((1, dv), lambda h, b: (0, 0)),
                  pl.BlockSpec(memory_space=pl.ANY)],
        out_specs=pl.BlockSpec((y_rows, dv), lambda h, b: (0, out_blk0 + h)),
        out_shape=jax.ShapeDtypeStruct(y_buf.shape, y_buf.dtype),
        input_output_aliases={10: 0},
        scratch_shapes=[pltpu.VMEM((t_lat, dv), F32), pltpu.VMEM((t_ctx, dv), F32),
                        pltpu.VMEM((HEAD_DIM, dv), F32)],
        compiler_params=pltpu.CompilerParams(dimension_semantics=("parallel", "arbitrary"),
                                             vmem_limit_bytes=_vmem_limit(est)),
        name="retention",
    )(log_gamma, qk, qk, qk, qk, p, p, p, p, gain.reshape(1, dv), y_buf)


def _s5_tables(a_re, a_im, log_step, b_re, b_im, c_re, c_im):
    hp = lax.Precision.HIGHEST
    n_dir, n_grp, n_st = a_re.shape
    ln = S5_L
    tg = LANES // S5_P
    n_tile = n_grp // tg
    lam = lax.complex(a_re.astype(F32), a_im.astype(F32))
    step = jnp.exp(log_step.astype(F32))[..., None]
    a_bar = jnp.exp(lam * step)
    b_bar = ((a_bar - 1.0) / lam)[..., None] * lax.complex(b_re.astype(F32), b_im.astype(F32))
    c_mat = lax.complex(c_re.astype(F32), c_im.astype(F32))
    taus = jnp.arange(ln + 1, dtype=F32)
    apow = jnp.exp((lam * step)[..., None] * taus)

    kern = jnp.einsum("dgpn,dgnt,dgnq->dgtpq", c_mat, apow[..., :ln], b_bar, precision=hp).real
    ti = jnp.arange(ln)[:, None]
    tj = jnp.arange(ln)[None, :]
    t_f = kern[0][:, jnp.clip(ti - tj, 0)] * (ti >= tj)[None, :, :, None, None]
    t_b = kern[1][:, jnp.clip(tj - ti, 0)] * (tj >= ti)[None, :, :, None, None]
    toep = (t_f + t_b).reshape(n_tile, tg, ln, ln, S5_P, S5_P)
    m_intra = toep.transpose(0, 3, 1, 5, 2, 4).reshape(n_tile, ln, LANES, ln * S5_P)

    def in_mat(d, tau_of_j):
        w = apow[d][:, :, tau_of_j][..., None] * b_bar[d][:, :, None, :]
        wri = jnp.stack([w.real, w.imag]).reshape(2, n_tile, tg, n_st, ln, S5_P)
        return wri.transpose(1, 4, 2, 5, 0, 3).reshape(n_tile, ln, LANES, 2 * n_st)

    def out_mat(d, tau_of_i):
        w = c_mat[d][:, :, :, None] * apow[d][:, None, :, tau_of_i]
        wri = jnp.stack([w.real, -w.imag]).reshape(2, n_tile, tg, S5_P, n_st, ln)
        return wri.transpose(1, 0, 2, 4, 5, 3).reshape(n_tile, 2 * tg * n_st, ln * S5_P)

    def spread(n_x, n_y):
        r, c = jnp.arange(n_x * n_y)[:, None], jnp.arange(n_x * tg * n_y)[None, :]
        return ((r // n_y == c // (tg * n_y)) & (r % n_y == c % n_y)).astype(BF16)

    idx = jnp.arange(ln)
    fold = lambda m: m.reshape(n_tile, ln * LANES, m.shape[-1])
    mats = [fold(m_intra), fold(in_mat(0, ln - 1 - idx)), fold(in_mat(1, idx)),
            out_mat(0, idx + 1), out_mat(1, ln - idx)]
    dec = [apow[0][..., ln].real, apow[0][..., ln].imag, apow[1][..., ln].real, apow[1][..., ln].imag]
    return ([m.astype(BF16) for m in mats], [spread(ln, S5_P), spread(2, n_st)],
            [x.reshape(n_tile, 1, tg * n_st) for x in dec])


def _group_of(index, period, width):
    assert period & (period - 1) == 0 and width & (width - 1) == 0
    return lax.shift_right_logical(index & (period - 1), width.bit_length() - 1)


def _s5_kernel(u_ref, m_ref, bcf_ref, bcb_ref, ccf_ref, ccb_ref, sp_out_ref, sp_state_ref,
               dfr_ref, dfi_ref, dbr_ref, dbi_ref, y_ref, vf, vb, *, n_batch, t_lat, t_ctx):
    ln = S5_L
    n_lat = n_batch * t_lat
    n_cr, n_lr = t_ctx // ln, t_lat // ln
    n_st = vf.shape[0]
    n_re = n_st // 2
    state_w = n_re * LANES

    def widen(c_ref, sp_ref, row_period, row_width, col_period, col_width):
        wide = jnp.dot(c_ref[...], sp_ref[...], preferred_element_type=F32)
        r = lax.broadcasted_iota(jnp.int32, wide.shape, 0)
        c = lax.broadcasted_iota(jnp.int32, wide.shape, 1)
        own = _group_of(r, row_period, row_width) == _group_of(c, col_period, col_width)
        return jnp.where(own, wide, 0.0).astype(BF16)

    lane_rows, state_rows = (LANES, S5_P), (state_w, state_w // (LANES // S5_P))
    m_w = widen(m_ref, sp_out_ref, *lane_rows, *lane_rows)
    bc_w = [widen(r, sp_state_ref, *lane_rows, *state_rows) for r in (bcf_ref, bcb_ref)]
    cc_w = [widen(r, sp_out_ref, *state_rows, *lane_rows) for r in (ccf_ref, ccb_ref)]

    def chunk_tokens(b, j):
        ctx_rows = u_ref[pl.ds(n_lat + b * t_ctx + j, n_cr, stride=ln), :]
        lat_rows = u_ref[pl.ds(b * t_lat + j, n_lr, stride=ln), :]
        return jnp.concatenate([ctx_rows, lat_rows], axis=0).astype(BF16)

    def chunks(b):
        return jnp.concatenate([chunk_tokens(b, j) for j in range(ln)], axis=1)

    def mix(uc, w):
        return jnp.dot(uc, w, preferred_element_type=F32)

    def batch_rows(b):
        return pl.ds(b, n_cr + n_lr, stride=n_batch)

    for b in range(n_batch):
        uc = chunks(b)
        for v, w in zip((vf, vb), bc_w):
            contrib = mix(uc, w)
            for c in range(n_st):
                v[c, batch_rows(b), :] = contrib[:, c * LANES:(c + 1) * LANES]

    tile_rows = 8
    tile_chunks = tile_rows // n_batch
    n_ct, n_tiles = n_cr // tile_chunks, (n_cr + n_lr) // tile_chunks

    def scan(v, d_re, d_im, tile_of_step, backward):
        order = range(tile_chunks - 1, -1, -1) if backward else range(tile_chunks)

        def step(s, carry):
            rows = pl.ds(pl.multiple_of(tile_of_step(s) * tile_rows, tile_rows), tile_rows)
            new_carry = []
            for c in range(n_re):
                s_re, s_im = carry[2 * c], carry[2 * c + 1]
                t_re, t_im = v[c, rows, :], v[n_re + c, rows, :]
                dr, di = d_re[:, c * LANES:(c + 1) * LANES], d_im[:, c * LANES:(c + 1) * LANES]
                o_re, o_im = [None] * tile_chunks, [None] * tile_chunks
                for k in order:
                    sub = slice(k * n_batch, (k + 1) * n_batch)
                    o_re[k], o_im[k] = s_re, s_im
                    s_re, s_im = dr * s_re - di * s_im + t_re[sub], dr * s_im + di * s_re + t_im[sub]
                v[c, rows, :] = jnp.concatenate(o_re, axis=0)
                v[n_re + c, rows, :] = jnp.concatenate(o_im, axis=0)
                new_carry += [s_re, s_im]
            return tuple(new_carry)

        zero = jnp.zeros((n_batch, LANES), F32)
        lax.fori_loop(0, n_tiles, step, (zero,) * n_st)

    scan(vf, dfr_ref[...], dfi_ref[...], lambda s: s, False)
    scan(vb, dbr_ref[...], dbi_ref[...],
         lambda s: jnp.where(s < n_ct, n_ct - 1 - s, n_tiles + n_ct - 1 - s), True)

    def entering_state(v, b):
        return jnp.concatenate([v[c, batch_rows(b), :] for c in range(n_st)], axis=1).astype(BF16)

    for b in range(n_batch):
        y = (mix(chunks(b), m_w) + mix(entering_state(vf, b), cc_w[0]) + mix(entering_state(vb, b), cc_w[1]))
        for i in range(ln):
            lanes = slice(i * LANES, (i + 1) * LANES)
            y_ref[pl.ds(n_lat + b * t_ctx + i, n_cr, stride=ln), :] = y[:n_cr, lanes]
            y_ref[pl.ds(b * t_lat + i, n_lr, stride=ln), :] = y[n_cr:, lanes]


def _s5_core(p, col_blk0, mats, spreads, decs, *, n_batch, t_lat, t_ctx):
    rows = p.shape[0]
    n_tile = mats[0].shape[0]
    n_chunk = (t_lat + t_ctx) // S5_L
    assert 8 % n_batch == 0 and (t_ctx // S5_L) % (8 // n_batch) == 0 and (t_lat // S5_L) % (8 // n_batch) == 0
    state = spreads[1].shape[1]
    wide = S5_L * LANES
    mat_specs = [pl.BlockSpec((None,) + m.shape[1:], lambda t: (t, 0, 0)) for m in mats]
    spread_specs = [pl.BlockSpec(sp.shape, lambda t: (0, 0)) for sp in spreads]
    dec_specs = [pl.BlockSpec((None, 1, state // 2), lambda t: (t, 0, 0)) for _ in decs]
    est = (4 * _nbytes((rows, LANES), F32) + len(mats) * 2 * _nbytes((wide, wide), F32)
           + 2 * _nbytes((n_chunk * n_batch, state), F32) + 6 * _nbytes((n_chunk, state), F32))
    return pl.pallas_call(
        functools.partial(_s5_kernel, n_batch=n_batch, t_lat=t_lat, t_ctx=t_ctx),
        grid=(n_tile,),
        in_specs=([pl.BlockSpec((rows, LANES), lambda t: (0, col_blk0 + t))] + mat_specs + spread_specs
                  + dec_specs),
        out_specs=pl.BlockSpec((rows, LANES), lambda t: (0, t)),
        out_shape=jax.ShapeDtypeStruct((rows, n_tile * LANES), F32),
        scratch_shapes=[pltpu.VMEM((state // LANES, n_chunk * n_batch, LANES), F32) for _ in range(2)],
        compiler_params=pltpu.CompilerParams(dimension_semantics=("arbitrary",),
                                             vmem_limit_bytes=_vmem_limit(est)),
        name="s5_core",
    )(p, *mats, *spreads, *decs)


def _s5_finish_kernel(y_ref, ua_ref, ub_ref, d_ref, w_ref, o_ref, *, out_col0):
    u = jnp.concatenate([ua_ref[...], ub_ref[...]], axis=1)
    z = jax.nn.gelu(y_ref[...] + d_ref[...] * u)
    gl = jnp.dot(z.astype(BF16), w_ref[...].astype(BF16), preferred_element_type=F32)
    o_ref[...] = jnp.zeros_like(o_ref)
    o_ref[:, out_col0:out_col0 + z.shape[1]] = (z * jax.nn.sigmoid(gl)).astype(o_ref.dtype)


def _s5_finish(y, p, col_half0, d_skip, w_glu, wlead, *, out_rows, out_cols, out_col0, tm=512):
    w = y.shape[1]
    m_total = out_rows
    tm = _tile(m_total, tm)
    row = pl.BlockSpec((tm, w), lambda m: (m, 0))
    nlead = len(wlead)
    return pl.pallas_call(
        functools.partial(_s5_finish_kernel, out_col0=out_col0), grid=(m_total // tm,),
        in_specs=[row, pl.BlockSpec((tm, w // 2), lambda m: (m, col_half0)),
                  pl.BlockSpec((tm, w // 2), lambda m: (m, col_half0 + 1)),
                  pl.BlockSpec((1, w), lambda m: (0, 0)),
                  pl.BlockSpec((None,) * nlead + (w, w), lambda m: tuple(wlead) + (0, 0))],
        out_specs=pl.BlockSpec((tm, out_cols), lambda m: (m, 0)),
        out_shape=jax.ShapeDtypeStruct((m_total, out_cols), BF16),
        compiler_params=pltpu.CompilerParams(
            dimension_semantics=("parallel",),
            vmem_limit_bytes=_vmem_limit(8 * _nbytes((tm, w), F32) + 3 * _nbytes((w, w), F32)
                                         + 2 * _nbytes((tm, out_cols), BF16))),
        name="s5_finish",
    )(y, p, p, d_skip.reshape(1, w), w_glu)


def _s5_branch(p, col0, bw, params, layer, *, n_batch, t_lat, t_ctx, out_rows, out_cols, out_col0):
    mats, spreads, decs = _s5_tables(*(params[k][layer] for k in
                                       ("s5_a_re", "s5_a_im", "s5_log_step", "s5_b_re", "s5_b_im", "s5_c_re",
                                        "s5_c_im")))
    y = _s5_core(p, col0 // LANES, mats, spreads, decs, n_batch=n_batch, t_lat=t_lat, t_ctx=t_ctx)
    return _s5_finish(y, p, col0 // (bw // 2), params["s5_d"][layer], params["s5_w_glu"], (layer,),
                      out_rows=out_rows, out_cols=out_cols, out_col0=out_col0)


def _merge_kernel(h_ref, y_ref, wg_ref, wb_ref, o_ref, acc_ref, *, n_branch):
    j = pl.program_id(2)
    gate = jnp.dot(h_ref[...], wg_ref[...].astype(BF16), preferred_element_type=F32)
    val = jnp.dot(y_ref[...], wb_ref[...].astype(BF16), preferred_element_type=F32)
    term = jax.nn.sigmoid(gate) * val

    @pl.when(j == 0)
    def _():
        acc_ref[...] = term

    @pl.when(j > 0)
    def _():
        acc_ref[...] += term

    @pl.when(j == n_branch - 1)
    def _():
        o_ref[...] = acc_ref[...].astype(o_ref.dtype)


def _merge(h, ycat, w_gate, w_branch, layer, *, m_rows, tm=1024, tn=256):
    d = h.shape[1]
    tm = _tile(m_rows, tm)
    n_branch, bw = w_branch.shape[1], w_branch.shape[2]
    est = (2 * _nbytes((tm, d), BF16) + 2 * _nbytes((tm, bw), BF16) + 3 * _nbytes((d, tn), F32)
           + 3 * _nbytes((bw, tn), F32) + 5 * _nbytes((tm, tn), F32))
    return pl.pallas_call(
        functools.partial(_merge_kernel, n_branch=n_branch),
        grid=(m_rows // tm, d // tn, n_branch),
        in_specs=[pl.BlockSpec((tm, d), lambda m, n, j: (m, 0)),
                  pl.BlockSpec((tm, bw), lambda m, n, j: (m, j)),
                  pl.BlockSpec((None, None, d, tn), lambda m, n, j: (layer, j, 0, n)),
                  pl.BlockSpec((None, None, bw, tn), lambda m, n, j: (layer, j, 0, n))],
        out_specs=pl.BlockSpec((tm, tn), lambda m, n, j: (m, n)),
        out_shape=jax.ShapeDtypeStruct((m_rows, d), BF16),
        scratch_shapes=[pltpu.VMEM((tm, tn), F32)],
        compiler_params=pltpu.CompilerParams(dimension_semantics=("parallel", "parallel", "arbitrary"),
                                             vmem_limit_bytes=_vmem_limit(est)),
        name="merge",
    )(h, ycat, w_gate, w_branch)


def _swiglu_up_kernel(x_ref, w1_ref, w3_ref, o_ref):
    x = x_ref[...]
    a = jnp.dot(x, w1_ref[...].astype(BF16), preferred_element_type=F32)
    b = jnp.dot(x, w3_ref[...].astype(BF16), preferred_element_type=F32)
    o_ref[...] = (a * jax.nn.sigmoid(a) * b).astype(o_ref.dtype)


def _swiglu_up(x, w1, w3, wlead, *, tm=1024, tn=256):
    m_total, d = x.shape
    tm = _tile(m_total, tm)
    ff = w1.shape[-1]
    nlead = len(wlead)
    w_spec = pl.BlockSpec((None,) * nlead + (d, tn), lambda n, m: tuple(wlead) + (0, n))
    est = 2 * _nbytes((tm, d), BF16) + 6 * _nbytes((d, tn), F32) + 6 * _nbytes((tm, tn), F32)
    return pl.pallas_call(
        _swiglu_up_kernel,
        grid=(ff // tn, m_total // tm),
        in_specs=[pl.BlockSpec((tm, d), lambda n, m: (m, 0)), w_spec, w_spec],
        out_specs=pl.BlockSpec((tm, tn), lambda n, m: (m, n)),
        out_shape=jax.ShapeDtypeStruct((m_total, ff), BF16),
        compiler_params=pltpu.CompilerParams(dimension_semantics=("parallel", "arbitrary"),
                                             vmem_limit_bytes=_vmem_limit(est)),
        name="ffn_up",
    )(x, w1, w3)


MOE_TOP_K = 2
MOE_ROW_TILE = 512
MOE_GATHER_ROWS = 256
MOE_COMBINE_ROWS = 128
MOE_DMA_UNROLL = 8


def _moe_routing(gates, n_tok):
    tile = MOE_ROW_TILE
    n_rows = MOE_TOP_K * n_tok + N_EXPERTS * tile
    experts = jnp.concatenate([gates[:, N_EXPERTS + k] for k in range(MOE_TOP_K)]).astype(jnp.int32)
    onehot = (experts[:, None] == jnp.arange(N_EXPERTS, dtype=jnp.int32)[None, :]).astype(jnp.int32)
    before = jnp.cumsum(onehot, axis=0) - onehot
    counts = jnp.sum(onehot, axis=0)
    padded = ((counts + tile - 1) // tile) * tile
    ends = jnp.cumsum(padded)
    dest = jnp.sum(onehot * (before + (ends - padded)[None, :]), axis=1)
    tokens = jnp.tile(jnp.arange(n_tok, dtype=jnp.int32), MOE_TOP_K)
    src = jnp.zeros((n_rows,), jnp.int32).at[dest].set(tokens)
    tile_start = jnp.arange(n_rows // tile, dtype=jnp.int32) * tile
    tile_expert = jnp.minimum(jnp.sum((tile_start[:, None] >= ends[None, :]).astype(jnp.int32), axis=1),
                              N_EXPERTS - 1)
    return src, dest, tile_expert, (ends[-1:] // tile).astype(jnp.int32)


def _row_copy(src_hbm, row, dst_vmem, slot, sem):
    return pltpu.make_async_copy(src_hbm.at[pl.ds(row, 1)], dst_vmem.at[pl.ds(slot, 1)], sem)


def _moe_gather_kernel(src_ref, nused_ref, x_hbm, o_ref, sem, *, rows, steps_per_tile):
    step = pl.program_id(0)
    used = step < nused_ref[0] * steps_per_tile

    @pl.when(used)
    def _():
        base = step * rows

        def start(r, carry):
            _row_copy(x_hbm, src_ref[base + r], o_ref, r, sem).start()
            return carry

        def wait(r, carry):
            _row_copy(x_hbm, 0, o_ref, r, sem).wait()
            return carry

        lax.fori_loop(0, rows, start, 0, unroll=MOE_DMA_UNROLL)
        lax.fori_loop(0, rows, wait, 0, unroll=MOE_DMA_UNROLL)

    @pl.when(jnp.logical_not(used))
    def _():
        o_ref[...] = jnp.zeros_like(o_ref)


def _moe_gather(x, src, n_used):
    n_rows, w = src.shape[0], x.shape[1]
    rows = MOE_GATHER_ROWS
    return pl.pallas_call(
        functools.partial(_moe_gather_kernel, rows=rows, steps_per_tile=MOE_ROW_TILE // rows),
        grid_spec=pltpu.PrefetchScalarGridSpec(
            num_scalar_prefetch=2, grid=(n_rows // rows,),
            in_specs=[pl.BlockSpec(memory_space=pl.ANY)],
            out_specs=pl.BlockSpec((rows, w), lambda i, s, nu: (i, 0)),
            scratch_shapes=[pltpu.SemaphoreType.DMA]),
        out_shape=jax.ShapeDtypeStruct((n_rows, w), x.dtype),
        compiler_params=pltpu.CompilerParams(
            dimension_semantics=("arbitrary",),
            vmem_limit_bytes=_vmem_limit(2 * _nbytes((rows, w), x.dtype))),
        name="moe_gather",
    )(src, n_used, x)


def _moe_up_kernel(te_ref, nused_ref, x_ref, w1_ref, w3_ref, o_ref):
    used = pl.program_id(1) < nused_ref[0]

    @pl.when(used)
    def _():
        half = w1_ref.shape[0] // 2
        x_lo, x_hi = (v.astype(BF16) for v in _unpack_bf16_pairs(x_ref[...]))

        def proj(w_ref):
            return (jnp.dot(x_lo, w_ref[:half, :].astype(BF16), preferred_element_type=F32)
                    + jnp.dot(x_hi, w_ref[half:, :].astype(BF16), preferred_element_type=F32))

        a, b = proj(w1_ref), proj(w3_ref)
        o_ref[...] = (a * jax.nn.sigmoid(a) * b).astype(o_ref.dtype)

    @pl.when(jnp.logical_not(used))
    def _():
        o_ref[...] = jnp.zeros_like(o_ref)


def _moe_down_kernel(te_ref, nused_ref, u_ref, w2_ref, o_ref):
    used = pl.program_id(1) < nused_ref[0]

    @pl.when(used)
    def _():
        y = jnp.dot(u_ref[...], w2_ref[...].astype(BF16), preferred_element_type=F32)
        o_ref[...] = _pack_bf16_pairs(y)

    @pl.when(jnp.logical_not(used))
    def _():
        o_ref[...] = jnp.zeros_like(o_ref)


def _moe_grouped(kernel, x, ws, layer, tile_expert, n_used, *, tn, out_tn, out_dtype, name):
    n_rows, x_w = x.shape
    kdim, n_out = ws[0].shape[-2:]
    tm = MOE_ROW_TILE

    def row_blk(m, nu):
        return jnp.minimum(m, nu[0] - 1)

    w_spec = pl.BlockSpec((None, None, kdim, tn), lambda n, m, te, nu: (layer, te[row_blk(m, nu)], 0, n))
    est = (2 * _nbytes((tm, x_w), x.dtype) + len(ws) * 3 * _nbytes((kdim, tn), F32)
           + (3 + len(ws)) * _nbytes((tm, tn), F32) + 2 * _nbytes((tm, kdim), BF16))
    return pl.pallas_call(
        kernel,
        grid_spec=pltpu.PrefetchScalarGridSpec(
            num_scalar_prefetch=2, grid=(n_out // tn, n_rows // tm),
            in_specs=[pl.BlockSpec((tm, x_w), lambda n, m, te, nu: (row_blk(m, nu), 0))] + [w_spec] * len(ws),
            out_specs=pl.BlockSpec((tm, out_tn), lambda n, m, te, nu: (m, n))),
        out_shape=jax.ShapeDtypeStruct((n_rows, n_out // tn * out_tn), out_dtype),
        compiler_params=pltpu.CompilerParams(dimension_semantics=("parallel", "arbitrary"),
                                             vmem_limit_bytes=_vmem_limit(est)),
        name=name,
    )(tile_expert, n_used, x, *ws)


def _moe_combine_kernel(dest_ref, y_hbm, xs_ref, route_ref, gate_ref, *rest, rows, n_tok, pack_w, final_norm):
    final_gain_ref = rest[0] if final_norm else None
    o_ref, buf, sem = rest[-3:]
    base = pl.program_id(0) * rows
    for k in range(MOE_TOP_K):
        def start(r, carry, k=k):
            _row_copy(y_hbm, dest_ref[k * n_tok + base + r], buf.at[k], r, sem).start()
            return carry

        lax.fori_loop(0, rows, start, 0, unroll=MOE_DMA_UNROLL)
    for k in range(MOE_TOP_K):
        def wait(r, carry, k=k):
            _row_copy(y_hbm, 0, buf.at[k], r, sem).wait()
            return carry

        lax.fori_loop(0, rows, wait, 0, unroll=MOE_DMA_UNROLL)
    route = route_ref[...]
    w_lane0 = N_EXPERTS + MOE_TOP_K

    def expert_rows(k):
        halves = [_unpack_bf16_pairs(buf[k, :, c:c + pack_w]) for c in range(0, buf.shape[2], pack_w)]
        return jnp.concatenate([h for pair in halves for h in pair], axis=1)

    mix = functools.reduce(jnp.add, [route[:, w_lane0 + k:w_lane0 + k + 1] * expert_rows(k)
                                     for k in range(MOE_TOP_K)])
    out = xs_ref[...] + gate_ref[...] * mix
    o_ref[...] = _rms(out) * final_gain_ref[...] if final_norm else out


def _moe_combine(xs, y_rows, pack_w, dest, route, gate, tab, final_gain=None):
    n_tok, d = xs.shape
    rows = _tile(tab.group_rows, MOE_COMBINE_ROWS)
    row = lambda w: pl.BlockSpec((rows, w), lambda i, dst: (i, 0))
    est = 8 * _nbytes((rows, d), F32)
    final_norm = final_gain is not None
    extra_specs = [pl.BlockSpec((1, d), lambda i, dst: (0, 0))] if final_norm else []
    extra_args = [final_gain.reshape(1, d)] if final_norm else []
    return pl.pallas_call(
        functools.partial(_moe_combine_kernel, rows=rows, n_tok=n_tok, pack_w=pack_w, final_norm=final_norm),
        grid_spec=pltpu.PrefetchScalarGridSpec(
            num_scalar_prefetch=1, grid=(n_tok // rows,),
            in_specs=[pl.BlockSpec(memory_space=pl.ANY), row(d), row(route.shape[1]),
                      pl.BlockSpec((None, 1, d), lambda i, dst: (tab.index_of_row(i * rows), 0, 0))]
                     + extra_specs,
            out_specs=row(d),
            scratch_shapes=[pltpu.VMEM((MOE_TOP_K, rows, y_rows.shape[1]), y_rows.dtype),
                            pltpu.SemaphoreType.DMA]),
        out_shape=jax.ShapeDtypeStruct((n_tok, d), F32),
        compiler_params=pltpu.CompilerParams(dimension_semantics=("arbitrary",),
                                             vmem_limit_bytes=_vmem_limit(est)),
        name="moe_combine",
    )(dest, y_rows, xs, route, gate, *extra_args)


def _moe_sparse(xs, h2p, route, w1, w3, w2, layer, gate, tab, final_gain=None):
    n_tok = xs.shape[0]
    src, dest, tile_expert, n_used = _moe_routing(route, n_tok)
    xg = _moe_gather(h2p, src, n_used)
    up_tn, down_tn = 256, 1024
    ug = _moe_grouped(_moe_up_kernel, xg, (w1, w3), layer, tile_expert, n_used, tn=up_tn, out_tn=up_tn,
                      out_dtype=BF16, name="moe_up")
    yg = _moe_grouped(_moe_down_kernel, ug, (w2,), layer, tile_expert, n_used, tn=down_tn, out_tn=down_tn // 2,
                      out_dtype=jnp.uint32, name="moe_down")
    return _moe_combine(xs, yg, down_tn // 2, dest, route, gate, tab, final_gain)


def kernel(x, c, ctx, c_ctx, ada_w, ada_b, norm1_g, norm2_g, w_in, attn_q_norm, attn_k_norm, s5_a_re, s5_a_im, s5_log_step, s5_b_re, s5_b_im, s5_c_re, s5_c_im, s5_d, s5_w_glu, diff_lambda, diff_norm, ret_decay_logit, ret_norm, w_branch, w_merge_gate, w_out, ffn_w1, ffn_w3, ffn_w2, moe_router_w, moe_router_b, moe_w1, moe_w3, moe_w2, final_norm_g):
    n_batch, t_lat, d = x.shape
    t_ctx = ctx.shape[1]
    depth = w_in.shape[0]
    n_lat, n_ctx = n_batch * t_lat, n_batch * t_ctx
    bw = d // 4
    s5_params = dict(s5_a_re=s5_a_re, s5_a_im=s5_a_im, s5_log_step=s5_log_step, s5_b_re=s5_b_re,
                     s5_b_im=s5_b_im, s5_c_re=s5_c_re, s5_c_im=s5_c_im, s5_d=s5_d, s5_w_glu=s5_w_glu)

    tab = RowTable(lambda row: jnp.where(row < n_lat, row // t_lat, n_batch), t_lat)

    cos_t, sin_t = _rope_tables(t_lat, n_batch, t_ctx)
    ada_rows = 8
    c_all = jnp.zeros((ada_rows, d), F32).at[:n_batch].set(c).at[n_batch].set(c_ctx)
    xs = jnp.concatenate([x.reshape(n_lat, d), ctx.reshape(n_ctx, d)])

    for i in range(depth):
        need_ctx = i < depth - 1
        lam_init = 0.8 - 0.6 * math.exp(-0.3 * i)
        mod = _mm(c_all, ada_w, (i,), name="ada_mod", tm=ada_rows, tn=512, epi="bias", x_silu=True,
                  bias=ada_b[i].reshape(1, -1))
        tabs = [mod[:n_batch + 1, k * d:(k + 1) * d].reshape(n_batch + 1, 1, d) for k in range(6)]
        shift1, scale1, gate1, shift2, scale2, gate2 = tabs

        h = _modnorm(xs, norm1_g[i], shift1, scale1, tab)
        p = _mm(h, w_in, (i,), name="in_proj", tm=1024, tn=512)
        qk = _prep(p, cos_t, sin_t, attn_q_norm[i], attn_k_norm[i])

        att = functools.partial(_attention, qk=qk, p=p, n_batch=n_batch, t_lat=t_lat, t_ctx=t_ctx, tq=512)
        gqa = functools.partial(att, _gqa_kernel, name="gqa", heads=2,
                                q_w=4 * HEAD_DIM, q_blk0=0, k_w=HEAD_DIM, k_blk0=24, v_w=HEAD_DIM, v_blk0=10,
                                out_w=4 * HEAD_DIM, out_blk0=0)
        lp = diff_lambda[i].astype(F32)
        lam = (jnp.exp(jnp.sum(lp[0] * lp[1])) - jnp.exp(jnp.sum(lp[2] * lp[3])) + lam_init).reshape(1)
        dgain = diff_norm[i].reshape(1, 2 * HEAD_DIM)
        dif = functools.partial(att, functools.partial(_diff_kernel, out_scale=1.0 - lam_init),
                                name="diff",
                                heads=2, q_w=4 * HEAD_DIM, q_blk0=2, k_w=4 * HEAD_DIM, k_blk0=4,
                                v_w=4 * HEAD_DIM, v_blk0=9, out_w=4 * HEAD_DIM, out_blk0=2 * bw // (4 * HEAD_DIM),
                                lead_in=(lam,), lead_specs=(pl.BlockSpec(memory_space=pltpu.SMEM),),
                                extra_in=(dgain,),
                                extra_specs=(pl.BlockSpec((1, 2 * HEAD_DIM), lambda b, hh, ii: (0, 0)),))
        rows = n_lat + n_ctx if need_ctx else n_lat
        s5_col0 = 12 * HEAD_DIM
        ycat = _s5_branch(p, s5_col0, bw, s5_params, i, n_batch=n_batch, t_lat=t_lat, t_ctx=t_ctx,
                          out_rows=rows, out_cols=4 * bw, out_col0=bw)
        ycat = gqa(ctx_queries=False, y_buf=ycat)
        ycat = dif(ctx_queries=False, y_buf=ycat)
        log_gamma = jax.nn.log_sigmoid(ret_decay_logit[i].astype(F32))
        ycat = _retention(qk, p, log_gamma, ret_norm[i], ycat, 3 * bw // (2 * HEAD_DIM), n_batch=n_batch,
                          t_lat=t_lat, t_ctx=t_ctx, heads=4, q_blk0=26, k_blk0=30, v_blk0=26, g_blk0=30,
                          dv=2 * HEAD_DIM)
        if need_ctx:
            ycat = gqa(ctx_queries=True, y_buf=ycat)
            ycat = dif(ctx_queries=True, y_buf=ycat)
        acc = _merge(h, ycat, w_merge_gate, w_branch, i, m_rows=rows)
        xs = _mm(acc, w_out, (i,), name="out_proj", tm=1024, tn=512, epi="resgate", res=xs, gate=gate1,
                 tab=tab, m_rows=rows)

        if i % 2 == 0:
            j = i // 2
            h2 = _modnorm(xs, norm2_g[i], shift2, scale2, tab)
            u = _swiglu_up(h2, ffn_w1, ffn_w3, (j,))
            half = u.shape[1] // 2
            for kc in range(2):
                xs = _mm(u, ffn_w2, (j,), name="ffn_down", tm=512, tn=512, tk=half, k0=kc, epi="resgate",
                         res=xs, gate=gate2, tab=tab)
        else:
            j = i // 2
            h2, route = _modnorm(xs, norm2_g[i], shift2, scale2, tab, tm=256,
                                 router_w=moe_router_w[j], router_b=moe_router_b[j])
            closing = final_norm_g if (i == depth - 1 and xs.shape[0] == n_lat) else None
            xs = _moe_sparse(xs, h2, route, moe_w1, moe_w3, moe_w2, j, gate2, tab, closing)
            if closing is not None:
                return xs.reshape(n_batch, t_lat, d)

    return _final_norm(xs[:n_lat], final_norm_g).reshape(n_batch, t_lat, d)
```
